```python
import math
import jax, jax.numpy as jnp
from jax import lax
import numpy as np

D_MODEL = 1024
BATCH = 8
SEQ = 2048
DEPTH = 4

HEAD_DIM = 64
GROUP_HEADS = 4
GROUP_WIDTH = GROUP_HEADS * HEAD_DIM
MIX_WIDTH = 4 * GROUP_WIDTH
BLOCK = 128
DIFF_HEADS = GROUP_HEADS
DIFF_HALF = HEAD_DIM // 2
WIN_HEADS = GROUP_HEADS
WIN_KV_HEADS = 2
WINDOW = 128
AX_HEADS = GROUP_HEADS
AX_KV_HEADS = 2
GRID_W = 64
ROPE_THETA = 10000.0
MLA_HEADS = GROUP_HEADS
MLA_Q_RANK = 256
MLA_KV_RANK = 128
MLA_NOPE = 64
MLA_ROPE = 32
MLA_V = HEAD_DIM
REL_BUCKETS = 32
REL_MAX_DIST = 128
N_BIAS_HEADS = DIFF_HEADS + WIN_HEADS
D_FF = 128 * ((8 * D_MODEL // 3 + 127) // 128)
CONV_W = 3
EPS = 1e-6
IN_SPLITS = (
    DIFF_HEADS * HEAD_DIM, DIFF_HEADS * HEAD_DIM, DIFF_HEADS * HEAD_DIM,
    WIN_HEADS * HEAD_DIM, WIN_KV_HEADS * HEAD_DIM, WIN_KV_HEADS * HEAD_DIM,
    AX_HEADS * HEAD_DIM, AX_KV_HEADS * HEAD_DIM, AX_KV_HEADS * HEAD_DIM,
    MLA_Q_RANK, MLA_KV_RANK, MLA_ROPE,
)
IN_WIDTH = 3 * DIFF_HEADS * HEAD_DIM + 2 * (GROUP_WIDTH + 2 * 2 * HEAD_DIM) + MLA_Q_RANK + MLA_KV_RANK + MLA_ROPE

kernel_name = "hybrid_parallel_group_encoder"


def _rmsnorm(x, g):
    xf = x.astype(jnp.float32)
    y = xf * lax.rsqrt(jnp.mean(xf * xf, axis=-1, keepdims=True) + EPS)
    return (y * g.astype(jnp.float32)).astype(x.dtype)


def _rope_angles(pos, dim):
    inv = ROPE_THETA ** (-jnp.arange(0, dim, 2, dtype=jnp.float32) / dim)
    ang = pos.astype(jnp.float32)[:, None] * inv[None, :]
    return jnp.cos(ang), jnp.sin(ang)


def _apply_rope(x, cos, sin):
    x1, x2 = jnp.split(x.astype(jnp.float32), 2, axis=-1)
    c, s = cos[None, :, None, :], sin[None, :, None, :]
    return jnp.concatenate([x1 * c - x2 * s, x2 * c + x1 * s], axis=-1).astype(x.dtype)


def _axial_rope_tables(S):
    rows = S // GRID_W
    row_ids = jnp.repeat(jnp.arange(rows, dtype=jnp.int32), GRID_W)
    col_ids = jnp.tile(jnp.arange(GRID_W, dtype=jnp.int32), rows)
    return _rope_angles(row_ids, HEAD_DIM // 2), _rope_angles(col_ids, HEAD_DIM // 2)


def _axial_rope(x, row_tab, col_tab):
    half = HEAD_DIM // 2
    return jnp.concatenate([_apply_rope(x[..., :half], *row_tab),
                            _apply_rope(x[..., half:], *col_tab)], axis=-1)


def _t5_bucket(rel):
    half = REL_BUCKETS // 2
    max_exact = half // 2
    n = jnp.abs(rel)
    nf = jnp.maximum(n, 1).astype(jnp.float32)
    large = max_exact + (jnp.log(nf / max_exact) / math.log(REL_MAX_DIST / max_exact)
                         * (half - max_exact)).astype(jnp.int32)
    large = jnp.minimum(large, half - 1)
    return jnp.where(rel > 0, half, 0) + jnp.where(n < max_exact, n, large)


def _to_blocks(x):
    B, S = x.shape[:2]
    return x.reshape((B, S // BLOCK, BLOCK) + x.shape[2:]).swapaxes(0, 1)


def _from_blocks(y):
    nblk, B = y.shape[:2]
    return y.swapaxes(0, 1).reshape((B, nblk * BLOCK) + y.shape[3:])


def _diff_attention(q, k, v, lam, rel_bias, pos):
    d = q.shape[-1]
    scale = d ** -0.5
    table = rel_bias[:, :DIFF_HEADS].astype(jnp.float32)

    def block(args):
        qb, qpos = args
        s = jnp.einsum('bqhcd,bkhcd->bhcqk', qb, k, preferred_element_type=jnp.float32) * scale
        bias = table[_t5_bucket(pos[None, :] - qpos[:, None])]
        s = s + jnp.transpose(bias, (2, 0, 1))[None, :, None]
        p = jax.nn.softmax(s, axis=-1)
        a = p[:, :, 0] - lam * p[:, :, 1]
        return jnp.einsum('bhqk,bkhe->bqhe', a.astype(v.dtype), v)

    out = lax.map(block, (_to_blocks(q), pos.reshape(-1, BLOCK)))
    return _from_blocks(out)


def _window_attention(q, k, v, sink, rel_bias):
    B, S, Hq, d = q.shape
    Hkv = k.shape[2]
    G = Hq // Hkv
    nblk = S // BLOCK
    qb = q.reshape(B, nblk, BLOCK, Hkv, G, d)

    def band(t):
        tp = jnp.pad(t, ((0, 0), (BLOCK, BLOCK), (0, 0), (0, 0))).reshape(B, nblk + 2, BLOCK, Hkv, d)
        return jnp.concatenate([tp[:, :-2], tp[:, 1:-1], tp[:, 2:]], axis=2)

    kb, vb = band(k), band(v)
    s = jnp.einsum('bnqhgd,bnkhd->bnhgqk', qb, kb, preferred_element_type=jnp.float32) * d ** -0.5
    qi = jnp.arange(BLOCK, dtype=jnp.int32)
    ki = jnp.arange(3 * BLOCK, dtype=jnp.int32)
    rel = ki[None, :] - BLOCK - qi[:, None]
    bias = rel_bias[:, DIFF_HEADS:].astype(jnp.float32)[_t5_bucket(rel)]
    bias = jnp.transpose(bias, (2, 0, 1)).reshape(Hkv, G, BLOCK, 3 * BLOCK)
    kpos = (jnp.arange(nblk, dtype=jnp.int32)[:, None] - 1) * BLOCK + ki[None, :]
    valid = (jnp.abs(rel) <= WINDOW)[None] & ((kpos >= 0) & (kpos < S))[:, None, :]
    s = jnp.where(valid[None, :, None, None], s + bias[None, None], -jnp.inf)
    sk = jnp.broadcast_to(sink.astype(jnp.float32).reshape(1, 1, Hkv, G, 1, 1), s.shape[:-1] + (1,))
    p = jax.nn.softmax(jnp.concatenate([s, sk], axis=-1), axis=-1)[..., :-1]
    o = jnp.einsum('bnhgqk,bnkhd->bnqhgd', p.astype(v.dtype), vb)
    return o.reshape(B, S, Hq, d)


def _dense_gqa(q, k, v):
    B, S, Hq, dk = q.shape
    Hkv = k.shape[2]
    G = Hq // Hkv
    scale = dk ** -0.5

    def block(qb):
        s = jnp.einsum('bqhgd,bkhd->bhgqk', qb, k, preferred_element_type=jnp.float32) * scale
        p = jax.nn.softmax(s, axis=-1)
        return jnp.einsum('bhgqk,bkhd->bqhgd', p.astype(v.dtype), v)

    out = lax.map(block, _to_blocks(q.reshape(B, S, Hkv, G, dk)))
    return _from_blocks(out).reshape(B, S, Hq, v.shape[-1])


def _mla(c_q, c_kv, k_rope, q_norm, kv_norm, w_uq, w_ukv, cos, sin):
    B, S, _ = c_q.shape
    q = jnp.einsum('bsr,rf->bsf', _rmsnorm(c_q, q_norm), w_uq).reshape(B, S, MLA_HEADS, MLA_NOPE + MLA_ROPE)
    q = jnp.concatenate([q[..., :MLA_NOPE], _apply_rope(q[..., MLA_NOPE:], cos, sin)], axis=-1)
    kv = jnp.einsum('bsr,rf->bsf', _rmsnorm(c_kv, kv_norm), w_ukv).reshape(B, S, MLA_HEADS, MLA_NOPE + MLA_V)
    kr = jnp.broadcast_to(_apply_rope(k_rope[:, :, None, :], cos, sin), (B, S, MLA_HEADS, MLA_ROPE))
    k = jnp.concatenate([kv[..., :MLA_NOPE], kr], axis=-1)
    return _dense_gqa(q, k, kv[..., MLA_NOPE:])


def _conv_glu(h, w_up, conv_w, conv_b, w_down):
    up = jnp.einsum('bsd,df->bsf', h, w_up)
    gate, val = up[..., :D_FF], up[..., D_FF:]
    gate = lax.conv_general_dilated(gate, conv_w.astype(gate.dtype)[:, None, :], window_strides=(1,),
                                    padding=((CONV_W // 2, CONV_W // 2),),
                                    dimension_numbers=('NWC', 'WIO', 'NWC'),
                                    feature_group_count=D_FF) + conv_b
    return jnp.einsum('bsf,fd->bsd', jax.nn.gelu(gate, approximate=False) * val, w_down)


def setup_inputs(seed: int = 0) -> dict:
    key = jax.random.key(seed)
    ks = jax.random.split(key, 21)
    f32 = jnp.float32

    def nrm(k, shape, scale):
        return jax.random.normal(k, shape, f32) * scale

    def gain(k, shape):
        return 1.0 + 0.05 * jax.random.normal(k, shape, f32)

    L = DEPTH
    return {
        'x': nrm(ks[0], (BATCH, SEQ, D_MODEL), 1.0),
        'rel_bias': nrm(ks[1], (REL_BUCKETS, N_BIAS_HEADS), 0.5),
        'attn_pre_norm': gain(ks[2], (L, D_MODEL)),
        'w_in': nrm(ks[3], (L, D_MODEL, IN_WIDTH), D_MODEL ** -0.5),
        'diff_lambda': nrm(ks[4], (L, 4, DIFF_HALF), 0.1),
        'diff_subln': gain(ks[5], (L, HEAD_DIM)),
        'win_sink': nrm(ks[6], (L, WIN_HEADS), 1.0),
        'ax_q_norm': gain(ks[7], (L, HEAD_DIM)),
        'ax_k_norm': gain(ks[8], (L, HEAD_DIM)),
        'mla_q_norm': gain(ks[9], (L, MLA_Q_RANK)),
        'mla_kv_norm': gain(ks[10], (L, MLA_KV_RANK)),
        'mla_w_uq': nrm(ks[11], (L, MLA_Q_RANK, MLA_HEADS * (MLA_NOPE + MLA_ROPE)), MLA_Q_RANK ** -0.5),
        'mla_w_ukv': nrm(ks[12], (L, MLA_KV_RANK, MLA_HEADS * (MLA_NOPE + MLA_V)), MLA_KV_RANK ** -0.5),
        'w_out': nrm(ks[13], (L, MIX_WIDTH, D_MODEL), MIX_WIDTH ** -0.5),
        'attn_post_norm': gain(ks[14], (L, D_MODEL)),
        'ffn_pre_norm': gain(ks[15], (L, D_MODEL)),
        'ffn_w_up': nrm(ks[16], (L, D_MODEL, 2 * D_FF), D_MODEL ** -0.5),
        'ffn_conv_w': nrm(ks[17], (L, CONV_W, D_FF), CONV_W ** -0.5),
        'ffn_conv_b': nrm(ks[18], (L, D_FF), 0.02),
        'ffn_w_down': nrm(ks[19], (L, D_FF, D_MODEL), D_FF ** -0.5),
        'ffn_post_norm': gain(ks[20], (L, D_MODEL)),
    }


def reference(x, rel_bias, attn_pre_norm, w_in, diff_lambda, diff_subln, win_sink, ax_q_norm, ax_k_norm,
              mla_q_norm, mla_kv_norm, mla_w_uq, mla_w_ukv, w_out, attn_post_norm, ffn_pre_norm,
              ffn_w_up, ffn_conv_w, ffn_conv_b, ffn_w_down, ffn_post_norm):
    B, S, _ = x.shape
    pos = jnp.arange(S, dtype=jnp.int32)
    mla_cos, mla_sin = _rope_angles(pos, MLA_ROPE)
    row_tab, col_tab = _axial_rope_tables(S)
    split_at = [int(i) for i in np.cumsum(IN_SPLITS)[:-1]]
    for l in range(DEPTH):
        lam_init = 0.8 - 0.6 * math.exp(-0.3 * l)
        h = _rmsnorm(x, attn_pre_norm[l])
        proj = jnp.einsum('bsd,df->bsf', h, w_in[l])
        (a_q, a_k, a_v, b_q, b_k, b_v, c_q, c_k, c_v, d_cq, d_ckv, d_kr) = jnp.split(proj, split_at, axis=-1)

        lq1, lk1, lq2, lk2 = diff_lambda[l].astype(jnp.float32)
        lam = jnp.exp(jnp.sum(lq1 * lk1)) - jnp.exp(jnp.sum(lq2 * lk2)) + lam_init
        o_a = _diff_attention(a_q.reshape(B, S, DIFF_HEADS, 2, DIFF_HALF),
                              a_k.reshape(B, S, DIFF_HEADS, 2, DIFF_HALF),
                              a_v.reshape(B, S, DIFF_HEADS, HEAD_DIM), lam, rel_bias, pos)
        o_a = _rmsnorm(o_a, diff_subln[l]) * (1.0 - lam_init)

        o_b = _window_attention(b_q.reshape(B, S, WIN_HEADS, HEAD_DIM),
                                b_k.reshape(B, S, WIN_KV_HEADS, HEAD_DIM),
                                b_v.reshape(B, S, WIN_KV_HEADS, HEAD_DIM), win_sink[l], rel_bias)

        qc = _axial_rope(_rmsnorm(c_q.reshape(B, S, AX_HEADS, HEAD_DIM), ax_q_norm[l]), row_tab, col_tab)
        kc = _axial_rope(_rmsnorm(c_k.reshape(B, S, AX_KV_HEADS, HEAD_DIM), ax_k_norm[l]), row_tab, col_tab)
        o_c = _dense_gqa(qc, kc, c_v.reshape(B, S, AX_KV_HEADS, HEAD_DIM))

        o_d = _mla(d_cq, d_ckv, d_kr, mla_q_norm[l], mla_kv_norm[l], mla_w_uq[l], mla_w_ukv[l], mla_cos, mla_sin)

        mixed = jnp.concatenate([o_a.reshape(B, S, GROUP_WIDTH), o_b.reshape(B, S, GROUP_WIDTH),
                                 o_c.reshape(B, S, GROUP_WIDTH), o_d.reshape(B, S, GROUP_WIDTH)], axis=-1)
        x = x + _rmsnorm(jnp.einsum('bsf,fd->bsd', mixed, w_out[l]), attn_post_norm[l])

        h = _rmsnorm(x, ffn_pre_norm[l])
        x = x + _rmsnorm(_conv_glu(h, ffn_w_up[l], ffn_conv_w[l], ffn_conv_b[l], ffn_w_down[l]), ffn_post_norm[l])
    return x
```

```python
import functools
import math

import jax
import jax.numpy as jnp
import numpy as np
from jax import lax
from jax.experimental import pallas as pl
from jax.experimental.pallas import tpu as pltpu

D_MODEL = 1024
HEAD_DIM = 64
GROUP_WIDTH = 256
BLOCK = 128
DIFF_HALF = 32
WINDOW = 128
GRID_W = 64
ROPE_THETA = 10000.0
MLA_Q_RANK = 256
MLA_KV_RANK = 128
MLA_NOPE = 64
MLA_ROPE = 32
REL_BUCKETS = 32
REL_MAX_DIST = 128
D_FF = 2816
EPS = 1e-6
LANES = 128
NEG_BIG = -1e30
VMEM_LIMIT = 56 * 1024 * 1024

PROJ_W = 768 + 512 + 512 + 256 + 128 + 128


def _cparams(sem):
    return pltpu.CompilerParams(dimension_semantics=sem, vmem_limit_bytes=VMEM_LIMIT)


def _rms(x, g):
    return x * lax.rsqrt(jnp.mean(x * x, axis=-1, keepdims=True) + EPS) * g


def _group_rms(x, ngroups):
    grp = lax.broadcasted_iota(jnp.int32, x.shape, 1) // HEAD_DIM
    x2 = x * x
    inv = jnp.zeros_like(x)
    for h in range(ngroups):
        m = grp == h
        ss = jnp.sum(jnp.where(m, x2, 0.0), axis=-1, keepdims=True)
        inv = jnp.where(m, lax.rsqrt(ss * (1.0 / HEAD_DIM) + EPS), inv)
    return x * inv


def _rope(x, c, sn, sp, half):
    w = x.shape[-1]
    return x * c + pltpu.roll(x, w - half, 1) * sn + pltpu.roll(x, half, 1) * sp


def _bias_kernel(tab_ref, idx_ref, o_ref, *, head0):
    h = pl.program_id(0)
    idx = idx_ref[...]
    acc = jnp.zeros(idx.shape, jnp.float32)
    for b in range(REL_BUCKETS):
        acc = jnp.where(idx == b, tab_ref[b, head0 + h], acc)
    o_ref[0] = jnp.where(idx < 0, NEG_BIG, acc)


def _bias_tiles(rel_bias, idx, head0, nheads):
    r, c = idx.shape
    return pl.pallas_call(
        functools.partial(_bias_kernel, head0=head0),
        grid=(nheads,),
        in_specs=[pl.BlockSpec(memory_space=pltpu.SMEM),
                  pl.BlockSpec((r, c), lambda h: (0, 0))],
        out_specs=pl.BlockSpec((1, r, c), lambda h: (h, 0, 0)),
        out_shape=jax.ShapeDtypeStruct((nheads, r, c), jnp.float32),
        compiler_params=_cparams(("arbitrary",)),
        name="rel_bias_tiles",
    )(rel_bias, idx)


def _bucket(rel):
    half = REL_BUCKETS // 2
    max_exact = half // 2
    n = jnp.abs(rel)
    nf = jnp.maximum(n, 1).astype(jnp.float32)
    large = max_exact + (jnp.log(nf / max_exact) / math.log(REL_MAX_DIST / max_exact)
                         * (half - max_exact)).astype(jnp.int32)
    large = jnp.minimum(large, half - 1)
    return jnp.where(rel > 0, half, 0) + jnp.where(n < max_exact, n, large)


def _pre_kernel(x_ref, g_ref, w_ref, cq_g_ref, ck_g_ref, ctab_ref, dq_g_ref, dkv_g_ref,
                wuq_ref, wukvk_ref, wukvv_ref, dtab_ref,
                aq_ref, ak_ref, av_ref, bq_ref, bk_ref, bv_ref, cq_ref, ck_ref, cv_ref,
                dq_ref, dk_ref, dv_ref):
    bf = jnp.bfloat16
    h = _rms(x_ref[...], g_ref[...]).astype(bf)
    proj = jnp.dot(h, w_ref[...], preferred_element_type=jnp.float32)
    aq_ref[...] = proj[:, 0:256].astype(bf)
    ak_ref[...] = proj[:, 256:512].astype(bf)
    av_ref[...] = proj[:, 512:768].astype(bf)
    bq_ref[...] = (proj[:, 768:1024] * 0.125).astype(bf)
    bk_ref[...] = proj[:, 1024:1152].astype(bf)
    bv_ref[...] = proj[:, 1152:1280].astype(bf)
    cc, csn, csp = ctab_ref[0], ctab_ref[1], ctab_ref[2]
    q = _group_rms(proj[:, 1280:1536], 4) * cq_g_ref[...]
    cq_ref[...] = (_rope(q, cc, csn, csp, 32) * 0.125).astype(bf)
    k = _group_rms(proj[:, 1536:1664], 2) * ck_g_ref[...]
    ck_ref[...] = _rope(k, cc[:, :128], csn[:, :128], csp[:, :128], 32).astype(bf)
    cv_ref[...] = proj[:, 1664:1792].astype(bf)
    dc, dsn, dsp = dtab_ref[0], dtab_ref[1], dtab_ref[2]
    cq = _rms(proj[:, 1792:2048], dq_g_ref[...]).astype(bf)
    q = jnp.dot(cq, wuq_ref[...], preferred_element_type=jnp.float32)
    rep = lambda t: jnp.concatenate([t] * 4, axis=1)
    dq_ref[...] = _rope(q, rep(dc), rep(dsn), rep(dsp), 16).astype(bf)
    ckv = _rms(proj[:, 2048:2176], dkv_g_ref[...]).astype(bf)
    kr = _rope(proj[:, 2176:2304], dc, dsn, dsp, 16)
    kn = jnp.dot(ckv, wukvk_ref[...], preferred_element_type=jnp.float32)
    dk_ref[...] = (kn + rep(kr)).astype(bf)
    dv_ref[...] = jnp.dot(ckv, wukvv_ref[...], preferred_element_type=jnp.float32).astype(bf)


def _pre(x2, g, w, cq_g, ck_g, ctab, dq_g, dkv_g, wuq, wukvk, wukvv, dtab, seq, tm):
    n = x2.shape[0]
    nt = seq // tm
    row = lambda wd: pl.BlockSpec((tm, wd), lambda i: (i, 0))
    full = lambda a: pl.BlockSpec(a.shape, lambda i: (0,) * a.ndim)
    widths = (256, 256, 256, 256, 128, 128, 256, 128, 128, 512, 512, 256)
    return pl.pallas_call(
        _pre_kernel,
        grid=(n // tm,),
        in_specs=[row(D_MODEL), full(g), full(w), full(cq_g), full(ck_g),
                  pl.BlockSpec((3, tm, 256), lambda i: (0, i % nt, 0)),
                  full(dq_g), full(dkv_g), full(wuq), full(wukvk), full(wukvv),
                  pl.BlockSpec((3, tm, 128), lambda i: (0, i % nt, 0))],
        out_specs=[row(wd) for wd in widths],
        out_shape=[jax.ShapeDtypeStruct((n, wd), jnp.bfloat16) for wd in widths],
        compiler_params=_cparams(("parallel",)),
        name="pre_attention",
    )(x2, g, w, cq_g, ck_g, ctab, dq_g, dkv_g, wuq, wukvk, wukvv, dtab)


def _softmax_parts(s, extra=None):
    m = jnp.max(s, axis=-1, keepdims=True)
    if extra is not None:
        m = jnp.maximum(m, extra)
    p = jnp.exp(s - m)
    l = jnp.sum(p, axis=-1, keepdims=True)
    if extra is not None:
        l = l + jnp.exp(extra - m)
    return p.astype(jnp.bfloat16), 1.0 / l


_NT = (((1,), (1,)), ((), ()))


def _diff_kernel(q_ref, k_ref, v_ref, g_ref, dl_ref, sub_ref, o_ref, *, lam_init, seq, tq):
    i = pl.program_id(2)
    nq = seq // tq
    q = q_ref[0]
    k = k_ref[0]
    v = v_ref[0]
    lane = lax.broadcasted_iota(jnp.int32, (tq, LANES), 1)
    dl = dl_ref[0]
    lam = (jnp.exp(jnp.sum(dl[0:1] * dl[1:2], axis=-1, keepdims=True))
           - jnp.exp(jnp.sum(dl[2:3] * dl[3:4], axis=-1, keepdims=True)) + lam_init)
    qs = jnp.concatenate(
        [jnp.where((lane >= lo) & (lane < lo + DIFF_HALF), q, jnp.zeros_like(q))
         for lo in (0, 32, 64, 96)], axis=0)
    s = lax.dot_general(qs, k, _NT, preferred_element_type=jnp.float32) * (DIFF_HALF ** -0.5)
    start = pl.multiple_of((nq - 1 - i) * tq, LANES)
    ps, rs = [], []
    for hc in range(4):
        bias = g_ref[hc // 2, :, pl.ds(start, seq)]
        p, r = _softmax_parts(s[hc * tq:(hc + 1) * tq] + bias)
        ps.append(p)
        rs.append(r)
    o = jnp.dot(jnp.concatenate(ps, axis=0), v, preferred_element_type=jnp.float32)
    out = jnp.zeros((tq, LANES), jnp.float32)
    for hh in range(2):
        o1 = o[(2 * hh) * tq:(2 * hh + 1) * tq] * rs[2 * hh]
        o2 = o[(2 * hh + 1) * tq:(2 * hh + 2) * tq] * rs[2 * hh + 1]
        oh = o1 - lam * o2
        mh = (lane >= hh * HEAD_DIM) & (lane < (hh + 1) * HEAD_DIM)
        ss = jnp.sum(jnp.where(mh, oh * oh, 0.0), axis=-1, keepdims=True) * (1.0 / HEAD_DIM)
        out = jnp.where(mh, oh * lax.rsqrt(ss + EPS), out)
    o_ref[0] = (out * sub_ref[...] * (1.0 - lam_init)).astype(jnp.bfloat16)


def _diff_attention(q, k, v, gbias, dl, sub, lam_init, tq):
    b, seq, _ = q.shape
    gw = gbias.shape[-1]
    return pl.pallas_call(
        functools.partial(_diff_kernel, lam_init=lam_init, seq=seq, tq=tq),
        grid=(b, 2, seq // tq),
        in_specs=[pl.BlockSpec((1, tq, LANES), lambda bi, p, i: (bi, i, p)),
                  pl.BlockSpec((1, seq, LANES), lambda bi, p, i: (bi, 0, p)),
                  pl.BlockSpec((1, seq, LANES), lambda bi, p, i: (bi, 0, p)),
                  pl.BlockSpec((2, tq, gw), lambda bi, p, i: (p, 0, 0)),
                  pl.BlockSpec((1, 4, DIFF_HALF), lambda bi, p, i: (0, 0, 0)),
                  pl.BlockSpec((1, LANES), lambda bi, p, i: (0, 0))],
        out_specs=pl.BlockSpec((1, tq, LANES), lambda bi, p, i: (bi, i, p)),
        out_shape=jax.ShapeDtypeStruct((b, seq, GROUP_WIDTH), jnp.bfloat16),
        compiler_params=_cparams(("parallel", "parallel", "arbitrary")),
        name="diff_attention",
    )(q, k, v, gbias, dl, sub)


def _win_kernel(sink_ref, q_ref, kp_ref, kc_ref, kn_ref, vp_ref, vc_ref, vn_ref, wb_ref, o_ref,
                *, nblk):
    n = pl.program_id(1)
    kcat = jnp.concatenate([kp_ref[0], kc_ref[0], kn_ref[0]], axis=0)
    vcat = jnp.concatenate([vp_ref[0], vc_ref[0], vn_ref[0]], axis=0)
    lane = lax.broadcasted_iota(jnp.int32, (BLOCK, LANES), 1)
    col = lax.broadcasted_iota(jnp.int32, (BLOCK, 3 * BLOCK), 1)
    edge = ((col < BLOCK) & (n == 0)) | ((col >= 2 * BLOCK) & (n == nblk - 1))
    for j in range(2):
        q = q_ref[0, :, j * LANES:(j + 1) * LANES]
        zero = jnp.zeros_like(q)
        qs = jnp.concatenate([jnp.where(lane < HEAD_DIM, q, zero),
                              jnp.where(lane >= HEAD_DIM, q, zero)], axis=0)
        s = lax.dot_general(qs, kcat, _NT, preferred_element_type=jnp.float32)
        ps, rs = [], []
        for g in range(2):
            head = j + 2 * g
            sg = jnp.where(edge, NEG_BIG, s[g * BLOCK:(g + 1) * BLOCK] + wb_ref[head])
            p, r = _softmax_parts(sg, extra=sink_ref[head])
            ps.append(p)
            rs.append(r)
        o = jnp.dot(jnp.concatenate(ps, axis=0), vcat, preferred_element_type=jnp.float32)
        out = jnp.where(lane < HEAD_DIM, o[:BLOCK] * rs[0], o[BLOCK:] * rs[1])
        o_ref[0, :, j * LANES:(j + 1) * LANES] = out.astype(jnp.bfloat16)


def _win_attention(q, k, v, sink, wbias):
    b, seq, _ = q.shape
    nblk = seq // BLOCK
    prev = lambda bi, n: (bi, jnp.maximum(n - 1, 0), 0)
    cur = lambda bi, n: (bi, n, 0)
    nxt = lambda bi, n: (bi, jnp.minimum(n + 1, nblk - 1), 0)
    kv = lambda f: pl.BlockSpec((1, BLOCK, LANES), f)
    return pl.pallas_call(
        functools.partial(_win_kernel, nblk=nblk),
        grid=(b, nblk),
        in_specs=[pl.BlockSpec(memory_space=pltpu.SMEM),
                  pl.BlockSpec((1, BLOCK, GROUP_WIDTH), cur),
                  kv(prev), kv(cur), kv(nxt), kv(prev), kv(cur), kv(nxt),
                  pl.BlockSpec((4, BLOCK, 3 * BLOCK), lambda bi, n: (0, 0, 0))],
        out_specs=pl.BlockSpec((1, BLOCK, GROUP_WIDTH), cur),
        out_shape=jax.ShapeDtypeStruct((b, seq, GROUP_WIDTH), jnp.bfloat16),
        compiler_params=_cparams(("parallel", "arbitrary")),
        name="window_attention",
    )(sink, q, k, k, k, v, v, v, wbias)


def _gqa_kernel(q_ref, k_ref, v_ref, o_ref, *, tq):
    q = q_ref[0]
    lane = lax.broadcasted_iota(jnp.int32, (tq, LANES), 1)
    zero = jnp.zeros_like(q)
    qs = jnp.concatenate([jnp.where(lane < HEAD_DIM, q, zero),
                          jnp.where(lane >= HEAD_DIM, q, zero)], axis=0)
    s = lax.dot_general(qs, k_ref[0], _NT, preferred_element_type=jnp.float32)
    p0, r0 = _softmax_parts(s[:tq])
    p1, r1 = _softmax_parts(s[tq:])
    o = jnp.dot(jnp.concatenate([p0, p1], axis=0), v_ref[0], preferred_element_type=jnp.float32)
    o_ref[0] = jnp.where(lane < HEAD_DIM, o[:tq] * r0, o[tq:] * r1).astype(jnp.bfloat16)


def _gqa_attention(q, k, v, tq):
    b, seq, _ = q.shape
    return pl.pallas_call(
        functools.partial(_gqa_kernel, tq=tq),
        grid=(b, 2, seq // tq),
        in_specs=[pl.BlockSpec((1, tq, LANES), lambda bi, j, i: (bi, i, j)),
                  pl.BlockSpec((1, seq, LANES), lambda bi, j, i: (bi, 0, 0)),
                  pl.BlockSpec((1, seq, LANES), lambda bi, j, i: (bi, 0, 0))],
        out_specs=pl.BlockSpec((1, tq, LANES), lambda bi, j, i: (bi, i, j)),
        out_shape=jax.ShapeDtypeStruct((b, seq, GROUP_WIDTH), jnp.bfloat16),
        compiler_params=_cparams(("parallel", "parallel", "arbitrary")),
        name="axial_gqa_attention",
    )(q, k, v)


def _mla_kernel(q_ref, k_ref, v_ref, o_ref, *, tq):
    lane = lax.broadcasted_iota(jnp.int32, (tq, LANES), 1)
    scale = (MLA_NOPE + MLA_ROPE) ** -0.5
    outs = []
    for hh in range(2):
        q = q_ref[0, :, hh * LANES:(hh + 1) * LANES]
        k = k_ref[0, :, hh * LANES:(hh + 1) * LANES]
        s = lax.dot_general(q, k, _NT, preferred_element_type=jnp.float32) * scale
        p, r = _softmax_parts(s)
        outs.append(jnp.dot(p, v_ref[0], preferred_element_type=jnp.float32) * r)
    o_ref[0] = jnp.where(lane < HEAD_DIM, outs[0], outs[1]).astype(jnp.bfloat16)


def _mla_attention(q, k, v, tq):
    b, seq, _ = q.shape
    return pl.pallas_call(
        functools.partial(_mla_kernel, tq=tq),
        grid=(b, 2, seq // tq),
        in_specs=[pl.BlockSpec((1, tq, 2 * LANES), lambda bi, p, i: (bi, i, p)),
                  pl.BlockSpec((1, seq, 2 * LANES), lambda bi, p, i: (bi, 0, p)),
                  pl.BlockSpec((1, seq, LANES), lambda bi, p, i: (bi, 0, p))],
        out_specs=pl.BlockSpec((1, tq, LANES), lambda bi, p, i: (bi, i, p)),
        out_shape=jax.ShapeDtypeStruct((b, seq, GROUP_WIDTH), jnp.bfloat16),
        compiler_params=_cparams(("parallel", "parallel", "arbitrary")),
        name="latent_attention",
    )(q, k, v)


def _outproj_kernel(x_ref, a_ref, b_ref, c_ref, d_ref, w_ref, g_ref, o_ref):
    y = jnp.dot(a_ref[...], w_ref[0], preferred_element_type=jnp.float32)
    y += jnp.dot(b_ref[...], w_ref[1], preferred_element_type=jnp.float32)
    y += jnp.dot(c_ref[...], w_ref[2], preferred_element_type=jnp.float32)
    y += jnp.dot(d_ref[...], w_ref[3], preferred_element_type=jnp.float32)
    o_ref[...] = x_ref[...] + _rms(y, g_ref[...])


def _outproj(x2, oa, ob, oc, od, w, g, tm):
    n = x2.shape[0]
    row = lambda wd: pl.BlockSpec((tm, wd), lambda i: (i, 0))
    return pl.pallas_call(
        _outproj_kernel,
        grid=(n // tm,),
        in_specs=[row(D_MODEL), row(256), row(256), row(256), row(256),
                  pl.BlockSpec(w.shape, lambda i: (0, 0, 0)),
                  pl.BlockSpec(g.shape, lambda i: (0, 0))],
        out_specs=row(D_MODEL),
        out_shape=jax.ShapeDtypeStruct((n, D_MODEL), jnp.float32),
        compiler_params=_cparams(("parallel",)),
        name="out_projection",
    )(x2, oa, ob, oc, od, w, g)


HALO = 8


def _ffn_kernel(x_ref, xp_ref, xn_ref, gpre_ref, wup_ref, cw_ref, cb_ref, wdn_ref, gpost_ref,
                o_ref, h_ref, gate_ref, acc_ref, *, tm, fc, tiles_per_seq):
    i = pl.program_id(0)
    j = pl.program_id(1)
    bf = jnp.bfloat16

    @pl.when(j == 0)
    def _():
        g = gpre_ref[...]
        first = (i % tiles_per_seq) == 0
        last = (i % tiles_per_seq) == tiles_per_seq - 1
        hp = _rms(xp_ref[...], g)
        hn = _rms(xn_ref[...], g)
        h_ref[0:HALO, :] = jnp.where(first, 0.0, hp).astype(bf)
        h_ref[HALO:HALO + tm, :] = _rms(x_ref[...], g).astype(bf)
        h_ref[HALO + tm:, :] = jnp.where(last, 0.0, hn).astype(bf)
        acc_ref[...] = jnp.zeros_like(acc_ref)

    up = jnp.dot(h_ref[...], wup_ref[...], preferred_element_type=jnp.float32)
    gate_ref[...] = up[:, :fc]
    val = up[HALO:HALO + tm, fc:]
    cw = cw_ref[...]
    gate = (gate_ref[pl.ds(HALO - 1, tm), :] * cw[0:1]
            + gate_ref[pl.ds(HALO, tm), :] * cw[1:2]
            + gate_ref[pl.ds(HALO + 1, tm), :] * cw[2:3] + cb_ref[...])
    act = 0.5 * gate * (1.0 + lax.erf(gate * (2.0 ** -0.5))) * val
    acc_ref[...] += jnp.dot(act.astype(bf), wdn_ref[...], preferred_element_type=jnp.float32)

    @pl.when(j == pl.num_programs(1) - 1)
    def _():
        o_ref[...] = x_ref[...] + _rms(acc_ref[...], gpost_ref[...])


def _ffn(x2, gpre, wup, cw, cb, wdn, gpost, seq, tm, fc):
    n = x2.shape[0]
    nf = D_FF // fc
    tph = tm // HALO
    nh = n // HALO
    return pl.pallas_call(
        functools.partial(_ffn_kernel, tm=tm, fc=fc, tiles_per_seq=seq // tm),
        grid=(n // tm, nf),
        in_specs=[pl.BlockSpec((tm, D_MODEL), lambda i, j: (i, 0)),
                  pl.BlockSpec((HALO, D_MODEL), lambda i, j: (jnp.maximum(i * tph - 1, 0), 0)),
                  pl.BlockSpec((HALO, D_MODEL), lambda i, j: (jnp.minimum((i + 1) * tph, nh - 1), 0)),
                  pl.BlockSpec((1, D_MODEL), lambda i, j: (0, 0)),
                  pl.BlockSpec((D_MODEL, 2 * fc), lambda i, j: (0, j)),
                  pl.BlockSpec((3, fc), lambda i, j: (0, j)),
                  pl.BlockSpec((1, fc), lambda i, j: (0, j)),
                  pl.BlockSpec((fc, D_MODEL), lambda i, j: (j, 0)),
                  pl.BlockSpec((1, D_MODEL), lambda i, j: (0, 0))],
        out_specs=pl.BlockSpec((tm, D_MODEL), lambda i, j: (i, 0)),
        out_shape=jax.ShapeDtypeStruct((n, D_MODEL), jnp.float32),
        scratch_shapes=[pltpu.VMEM((tm + 2 * HALO, D_MODEL), jnp.bfloat16),
                        pltpu.VMEM((tm + 2 * HALO, fc), jnp.float32),
                        pltpu.VMEM((tm, D_MODEL), jnp.float32)],
        compiler_params=_cparams(("parallel", "arbitrary")),
        name="conv_glu",
    )(x2, x2, x2, gpre, wup, cw, cb, wdn, gpost)


_GQA_HEAD_ORDER = (0, 2, 1, 3)


def _reorder_heads(w, order):
    lead = w.shape[:-1]
    nh = w.shape[-1] // HEAD_DIM
    w = w.reshape(lead + (nh, HEAD_DIM))
    return jnp.take(w, jnp.array(order), axis=-2).reshape(lead + (nh * HEAD_DIM,))


def _axial_perm(w):
    lead = w.shape[:-1]
    nh = w.shape[-1] // HEAD_DIM
    w = w.reshape(lead + (nh, 2, 2, 16))
    return jnp.swapaxes(w, -3, -2).reshape(lead + (nh * HEAD_DIM,))


def _prep_w_in(w_in):
    bq = _reorder_heads(w_in[..., 768:1024], _GQA_HEAD_ORDER)
    cq = _axial_perm(_reorder_heads(w_in[..., 1280:1536], _GQA_HEAD_ORDER))
    ck = _axial_perm(w_in[..., 1536:1664])
    kr = w_in[..., 2176:2208]
    z = lambda wd: jnp.zeros(w_in.shape[:-1] + (wd,), w_in.dtype)
    return jnp.concatenate(
        [w_in[..., 0:768], bq, w_in[..., 1024:1280], cq, ck, w_in[..., 1664:1792],
         w_in[..., 1792:2176], z(MLA_NOPE), kr, z(LANES - MLA_NOPE - MLA_ROPE)],
        axis=-1).astype(jnp.bfloat16)


def _rope_tables(cos, sin, lead_ones, trail_ones):
    s = cos.shape[0]
    one = lambda wd: jnp.ones((s, wd), jnp.float32)
    zero = lambda wd: jnp.zeros((s, wd), jnp.float32)
    c = jnp.concatenate([one(lead_ones), cos, cos, one(trail_ones)], axis=1)
    sn = jnp.concatenate([zero(lead_ones), -sin, zero(cos.shape[1]), zero(trail_ones)], axis=1)
    sp = jnp.concatenate([zero(lead_ones), zero(cos.shape[1]), sin, zero(trail_ones)], axis=1)
    return jnp.stack([c, sn, sp])


def _angles(pos, dim):
    inv = ROPE_THETA ** (-jnp.arange(0, dim, 2, dtype=jnp.float32) / dim)
    ang = pos.astype(jnp.float32)[:, None] * inv[None, :]
    return jnp.cos(ang), jnp.sin(ang)


def _forward(x, rel_bias, attn_pre_norm, w_in, diff_lambda, diff_subln, win_sink, ax_q_norm, ax_k_norm,
             mla_q_norm, mla_kv_norm, mla_w_uq, mla_w_ukv, w_out, attn_post_norm, ffn_pre_norm,
             ffn_w_up, ffn_conv_w, ffn_conv_b, ffn_w_down, ffn_post_norm, *, tq_a, tq, tm, fc):
    bsz, seq, _ = x.shape
    depth = w_in.shape[0]
    n = bsz * seq
    bf = jnp.bfloat16

    pos = jnp.arange(seq, dtype=jnp.int32)
    rows = seq // GRID_W
    row_ids = jnp.repeat(jnp.arange(rows, dtype=jnp.int32), GRID_W)
    col_ids = jnp.tile(jnp.arange(GRID_W, dtype=jnp.int32), rows)
    rcos, rsin = _angles(row_ids, HEAD_DIM // 2)
    ccos, csin = _angles(col_ids, HEAD_DIM // 2)
    ctab = _rope_tables(jnp.concatenate([rcos, ccos], 1), jnp.concatenate([rsin, csin], 1), 0, 0)
    ctab = jnp.tile(ctab, (1, 1, 4))
    mcos, msin = _angles(pos, MLA_ROPE)
    dtab = _rope_tables(mcos, msin, MLA_NOPE, LANES - MLA_NOPE - MLA_ROPE)

    gw = 2 * seq - tq_a
    qi = jnp.arange(tq_a, dtype=jnp.int32)[:, None]
    rel_a = jnp.arange(gw, dtype=jnp.int32)[None, :] - (seq - tq_a) - qi
    gbias = _bias_tiles(rel_bias, _bucket(rel_a), 0, 4)
    rel_b = jnp.arange(3 * BLOCK, dtype=jnp.int32)[None, :] - BLOCK - jnp.arange(BLOCK, dtype=jnp.int32)[:, None]
    idx_b = jnp.where(jnp.abs(rel_b) <= WINDOW, _bucket(rel_b), -1)
    wbias = _bias_tiles(rel_bias, idx_b, 4, 4)

    w_in_p = _prep_w_in(w_in)
    cq_g = jnp.tile(_axial_perm(ax_q_norm), (1, 4))[:, None, :]
    ck_g = jnp.tile(_axial_perm(ax_k_norm), (1, 2))[:, None, :]
    wuq = jnp.pad(mla_w_uq.reshape(depth, MLA_Q_RANK, 4, MLA_NOPE + MLA_ROPE),
                  ((0, 0), (0, 0), (0, 0), (0, LANES - MLA_NOPE - MLA_ROPE))
                  ).reshape(depth, MLA_Q_RANK, 4 * LANES).astype(bf)
    wukv = mla_w_ukv.reshape(depth, MLA_KV_RANK, 4, 2, HEAD_DIM)
    wukvk = jnp.pad(wukv[:, :, :, 0], ((0, 0), (0, 0), (0, 0), (0, HEAD_DIM))
                    ).reshape(depth, MLA_KV_RANK, 4 * LANES).astype(bf)
    wukvv = wukv[:, :, :, 1].reshape(depth, MLA_KV_RANK, GROUP_WIDTH).astype(bf)
    wo = w_out.reshape(depth, 4, 4, HEAD_DIM, D_MODEL)
    wo = jnp.stack([wo[:, 0], jnp.take(wo[:, 1], jnp.array(_GQA_HEAD_ORDER), axis=1),
                    jnp.take(wo[:, 2], jnp.array(_GQA_HEAD_ORDER), axis=1), wo[:, 3]], axis=1)
    wo = wo.reshape(depth, 4, GROUP_WIDTH, D_MODEL).astype(bf)
    nf = D_FF // fc
    wup = ffn_w_up.reshape(depth, D_MODEL, 2, nf, fc)
    wup = jnp.swapaxes(wup, 2, 3).reshape(depth, D_MODEL, 2 * D_FF).astype(bf)
    wdn = ffn_w_down.astype(bf)
    sub = jnp.tile(diff_subln, (1, 2))[:, None, :]

    x2 = x.reshape(n, D_MODEL)
    r3 = lambda a: a.reshape(bsz, seq, a.shape[-1])
    for l in range(depth):
        lam_init = 0.8 - 0.6 * math.exp(-0.3 * l)
        (aq, ak, av, bq, bk, bv, cq, ck, cv, dq, dk, dv) = _pre(
            x2, attn_pre_norm[l][None], w_in_p[l], cq_g[l], ck_g[l], ctab,
            mla_q_norm[l][None], mla_kv_norm[l][None], wuq[l], wukvk[l], wukvv[l], dtab, seq, tm)
        oa = _diff_attention(r3(aq), r3(ak), r3(av), gbias, diff_lambda[l][None], sub[l],
                             lam_init, tq_a)
        ob = _win_attention(r3(bq), r3(bk), r3(bv), win_sink[l], wbias)
        oc = _gqa_attention(r3(cq), r3(ck), r3(cv), tq)
        od = _mla_attention(r3(dq), r3(dk), r3(dv), tq)
        x2 = _outproj(x2, oa.reshape(n, -1), ob.reshape(n, -1), oc.reshape(n, -1),
                      od.reshape(n, -1), wo[l], attn_post_norm[l][None], tm)
        x2 = _ffn(x2, ffn_pre_norm[l][None], wup[l], ffn_conv_w[l], ffn_conv_b[l][None], wdn[l],
                  ffn_post_norm[l][None], seq, tm, fc)
    return x2.reshape(bsz, seq, D_MODEL)


def kernel(x, rel_bias, attn_pre_norm, w_in, diff_lambda, diff_subln, win_sink, ax_q_norm, ax_k_norm,
           mla_q_norm, mla_kv_norm, mla_w_uq, mla_w_ukv, w_out, attn_post_norm, ffn_pre_norm,
           ffn_w_up, ffn_conv_w, ffn_conv_b, ffn_w_down, ffn_post_norm):
    return _forward(x, rel_bias, attn_pre_norm, w_in, diff_lambda, diff_subln, win_sink, ax_q_norm,
                    ax_k_norm, mla_q_norm, mla_kv_norm, mla_w_uq, mla_w_ukv, w_out, attn_post_norm,
                    ffn_pre_norm, ffn_w_up, ffn_conv_w, ffn_conv_b, ffn_w_down, ffn_post_norm,
                    tq_a=BLOCK, tq=256, tm=512, fc=256)
```

```python
import functools
import math

import jax
import jax.numpy as jnp
import numpy as np
from jax import lax
from jax.experimental import pallas as pl
from jax.experimental.pallas import tpu as pltpu

D_MODEL = 1024
HEAD_DIM = 64
GROUP_WIDTH = 256
BLOCK = 128
DIFF_HALF = 32
WINDOW = 128
GRID_W = 64
ROPE_THETA = 10000.0
MLA_Q_RANK = 256
MLA_KV_RANK = 128
MLA_NOPE = 64
MLA_ROPE = 32
REL_BUCKETS = 32
REL_MAX_DIST = 128
D_FF = 2816
EPS = 1e-6
LANES = 128
LOG2E = 1.4426950408889634
NEG_BIG = -1e30
VMEM_LIMIT = 56 * 1024 * 1024

PROJ_W = 768 + 512 + 512 + 256 + 128 + 128


def _cparams(sem):
    return pltpu.CompilerParams(dimension_semantics=sem, vmem_limit_bytes=VMEM_LIMIT)


def _rms(x, g):
    return x * lax.rsqrt(jnp.mean(x * x, axis=-1, keepdims=True) + EPS) * g


def _group_rms(x, ngroups):
    grp = lax.broadcasted_iota(jnp.int32, x.shape, 1) // HEAD_DIM
    x2 = x * x
    inv = jnp.zeros_like(x)
    for h in range(ngroups):
        m = grp == h
        ss = jnp.sum(jnp.where(m, x2, 0.0), axis=-1, keepdims=True)
        inv = jnp.where(m, lax.rsqrt(ss * (1.0 / HEAD_DIM) + EPS), inv)
    return x * inv


def _rope(x, c, sn, sp, half):
    w = x.shape[-1]
    return x * c + pltpu.roll(x, w - half, 1) * sn + pltpu.roll(x, half, 1) * sp


def _bias_kernel(tab_ref, idx_ref, o_ref, *, head0, scale):
    h = pl.program_id(0)
    idx = idx_ref[...]
    acc = jnp.zeros(idx.shape, jnp.float32)
    for b in range(REL_BUCKETS):
        acc = jnp.where(idx == b, tab_ref[b, head0 + h], acc)
    o_ref[0] = jnp.where(idx < 0, NEG_BIG, acc * scale)


def _bias_tiles(rel_bias, idx, head0, nheads, scale):
    r, c = idx.shape
    return pl.pallas_call(
        functools.partial(_bias_kernel, head0=head0, scale=scale),
        grid=(nheads,),
        in_specs=[pl.BlockSpec(memory_space=pltpu.SMEM),
                  pl.BlockSpec((r, c), lambda h: (0, 0))],
        out_specs=pl.BlockSpec((1, r, c), lambda h: (h, 0, 0)),
        out_shape=jax.ShapeDtypeStruct((nheads, r, c), jnp.float32),
        compiler_params=_cparams(("arbitrary",)),
        name="rel_bias_tiles",
    )(rel_bias, idx)


def _bucket(rel):
    half = REL_BUCKETS // 2
    max_exact = half // 2
    n = jnp.abs(rel)
    nf = jnp.maximum(n, 1).astype(jnp.float32)
    large = max_exact + (jnp.log(nf / max_exact) / math.log(REL_MAX_DIST / max_exact)
                         * (half - max_exact)).astype(jnp.int32)
    large = jnp.minimum(large, half - 1)
    return jnp.where(rel > 0, half, 0) + jnp.where(n < max_exact, n, large)


def _pre_kernel(x_ref, g_ref, w_ref, cq_g_ref, ck_g_ref, ctab_ref, dq_g_ref, dkv_g_ref,
                wuq_ref, wukvk_ref, wukvv_ref, dtab_ref,
                aq_ref, ak_ref, av_ref, bq_ref, bk_ref, bv_ref, cq_ref, ck_ref, cv_ref,
                dq_ref, dk_ref, dv_ref):
    bf = jnp.bfloat16
    h = _rms(x_ref[...], g_ref[...]).astype(bf)
    proj = jnp.dot(h, w_ref[...], preferred_element_type=jnp.float32)
    aq_ref[...] = proj[:, 0:256].astype(bf)
    ak_ref[...] = proj[:, 256:512].astype(bf)
    av_ref[...] = proj[:, 512:768].astype(bf)
    bq_ref[...] = (proj[:, 768:1024] * 0.125).astype(bf)
    bk_ref[...] = proj[:, 1024:1152].astype(bf)
    bv_ref[...] = proj[:, 1152:1280].astype(bf)
    cc, csn, csp = ctab_ref[0], ctab_ref[1], ctab_ref[2]
    q = _group_rms(proj[:, 1280:1536], 4) * cq_g_ref[...]
    cq_ref[...] = (_rope(q, cc, csn, csp, 32) * 0.125).astype(bf)
    k = _group_rms(proj[:, 1536:1664], 2) * ck_g_ref[...]
    ck_ref[...] = _rope(k, cc[:, :128], csn[:, :128], csp[:, :128], 32).astype(bf)
    cv_ref[...] = proj[:, 1664:1792].astype(bf)
    dc, dsn, dsp = dtab_ref[0], dtab_ref[1], dtab_ref[2]
    cq = _rms(proj[:, 1792:2048], dq_g_ref[...]).astype(bf)
    q = jnp.dot(cq, wuq_ref[...], preferred_element_type=jnp.float32)
    rep = lambda t: jnp.concatenate([t] * 4, axis=1)
    dq_ref[...] = _rope(q, rep(dc), rep(dsn), rep(dsp), 16).astype(bf)
    ckv = _rms(proj[:, 2048:2176], dkv_g_ref[...]).astype(bf)
    kr = _rope(proj[:, 2176:2304], dc, dsn, dsp, 16)
    kn = jnp.dot(ckv, wukvk_ref[...], preferred_element_type=jnp.float32)
    dk_ref[...] = (kn + rep(kr)).astype(bf)
    dv_ref[...] = jnp.dot(ckv, wukvv_ref[...], preferred_element_type=jnp.float32).astype(bf)


def _pre(x2, g, w, cq_g, ck_g, ctab, dq_g, dkv_g, wuq, wukvk, wukvv, dtab, seq, tm):
    n = x2.shape[0]
    nt = seq // tm
    row = lambda wd: pl.BlockSpec((tm, wd), lambda i: (i, 0))
    full = lambda a: pl.BlockSpec(a.shape, lambda i: (0,) * a.ndim)
    widths = (256, 256, 256, 256, 128, 128, 256, 128, 128, 512, 512, 256)
    return pl.pallas_call(
        _pre_kernel,
        grid=(n // tm,),
        in_specs=[row(D_MODEL), full(g), full(w), full(cq_g), full(ck_g),
                  pl.BlockSpec((3, tm, 256), lambda i: (0, i % nt, 0)),
                  full(dq_g), full(dkv_g), full(wuq), full(wukvk), full(wukvv),
                  pl.BlockSpec((3, tm, 128), lambda i: (0, i % nt, 0))],
        out_specs=[row(wd) for wd in widths],
        out_shape=[jax.ShapeDtypeStruct((n, wd), jnp.bfloat16) for wd in widths],
        compiler_params=_cparams(("parallel",)),
        name="pre_attention",
    )(x2, g, w, cq_g, ck_g, ctab, dq_g, dkv_g, wuq, wukvk, wukvv, dtab)


def _softmax_parts(s, extra=None):
    m = jnp.max(s, axis=-1, keepdims=True)
    if extra is not None:
        m = jnp.maximum(m, extra)
    p = jnp.exp(s - m)
    l = jnp.sum(p, axis=-1, keepdims=True)
    if extra is not None:
        l = l + jnp.exp(extra - m)
    return p.astype(jnp.bfloat16), 1.0 / l


_NT = (((1,), (1,)), ((), ()))


def _diff_kernel(q_ref, k_ref, v_ref, g_ref, dl_ref, sub_ref, o_ref, *, lam_init, seq, tq):
    i = pl.program_id(2)
    nq = seq // tq
    q = q_ref[0]
    k = k_ref[0]
    v = v_ref[0]
    lane = lax.broadcasted_iota(jnp.int32, (tq, LANES), 1)
    dl = dl_ref[0]
    lam = (jnp.exp(jnp.sum(dl[0:1] * dl[1:2], axis=-1, keepdims=True))
           - jnp.exp(jnp.sum(dl[2:3] * dl[3:4], axis=-1, keepdims=True)) + lam_init)
    start = pl.multiple_of((nq - 1 - i) * tq, LANES)
    os = []
    for hc in range(4):
        lo = hc * DIFF_HALF
        qm = jnp.where((lane >= lo) & (lane < lo + DIFF_HALF), q, jnp.zeros_like(q))
        s = lax.dot_general(qm, k, _NT, preferred_element_type=jnp.float32)
        s = s * (DIFF_HALF ** -0.5 * LOG2E) + g_ref[hc // 2, :, pl.ds(start, seq)]
        p = jnp.exp2(s - jnp.max(s, axis=-1, keepdims=True))
        r = 1.0 / jnp.sum(p, axis=-1, keepdims=True)
        os.append(jnp.dot(p.astype(jnp.bfloat16), v, preferred_element_type=jnp.float32) * r)
    out = jnp.zeros((tq, LANES), jnp.float32)
    for hh in range(2):
        oh = os[2 * hh] - lam * os[2 * hh + 1]
        mh = (lane >= hh * HEAD_DIM) & (lane < (hh + 1) * HEAD_DIM)
        ss = jnp.sum(jnp.where(mh, oh * oh, 0.0), axis=-1, keepdims=True) * (1.0 / HEAD_DIM)
        out = jnp.where(mh, oh * lax.rsqrt(ss + EPS), out)
    o_ref[0] = (out * sub_ref[...] * (1.0 - lam_init)).astype(jnp.bfloat16)


def _diff_attention(q, k, v, gbias, dl, sub, lam_init, tq):
    b, seq, _ = q.shape
    gw = gbias.shape[-1]
    return pl.pallas_call(
        functools.partial(_diff_kernel, lam_init=lam_init, seq=seq, tq=tq),
        grid=(b, 2, seq // tq),
        in_specs=[pl.BlockSpec((1, tq, LANES), lambda bi, p, i: (bi, i, p)),
                  pl.BlockSpec((1, seq, LANES), lambda bi, p, i: (bi, 0, p)),
                  pl.BlockSpec((1, seq, LANES), lambda bi, p, i: (bi, 0, p)),
                  pl.BlockSpec((2, tq, gw), lambda bi, p, i: (p, 0, 0)),
                  pl.BlockSpec((1, 4, DIFF_HALF), lambda bi, p, i: (0, 0, 0)),
                  pl.BlockSpec((1, LANES), lambda bi, p, i: (0, 0))],
        out_specs=pl.BlockSpec((1, tq, LANES), lambda bi, p, i: (bi, i, p)),
        out_shape=jax.ShapeDtypeStruct((b, seq, GROUP_WIDTH), jnp.bfloat16),
        compiler_params=_cparams(("parallel", "parallel", "arbitrary")),
        name="diff_attention",
    )(q, k, v, gbias, dl, sub)


def _win_kernel(sink_ref, q_ref, kp_ref, kc_ref, kn_ref, vp_ref, vc_ref, vn_ref, wb_ref, o_ref,
                *, nblk):
    n = pl.program_id(1)
    kcat = jnp.concatenate([kp_ref[0], kc_ref[0], kn_ref[0]], axis=0)
    vcat = jnp.concatenate([vp_ref[0], vc_ref[0], vn_ref[0]], axis=0)
    lane = lax.broadcasted_iota(jnp.int32, (BLOCK, LANES), 1)
    col = lax.broadcasted_iota(jnp.int32, (BLOCK, 3 * BLOCK), 1)
    edge = ((col < BLOCK) & (n == 0)) | ((col >= 2 * BLOCK) & (n == nblk - 1))
    for j in range(2):
        q = q_ref[0, :, j * LANES:(j + 1) * LANES]
        zero = jnp.zeros_like(q)
        qs = jnp.concatenate([jnp.where(lane < HEAD_DIM, q, zero),
                              jnp.where(lane >= HEAD_DIM, q, zero)], axis=0)
        s = lax.dot_general(qs, kcat, _NT, preferred_element_type=jnp.float32)
        ps, rs = [], []
        for g in range(2):
            head = j + 2 * g
            sg = jnp.where(edge, NEG_BIG, s[g * BLOCK:(g + 1) * BLOCK] + wb_ref[head])
            p, r = _softmax_parts(sg, extra=sink_ref[head])
            ps.append(p)
            rs.append(r)
        o = jnp.dot(jnp.concatenate(ps, axis=0), vcat, preferred_element_type=jnp.float32)
        out = jnp.where(lane < HEAD_DIM, o[:BLOCK] * rs[0], o[BLOCK:] * rs[1])
        o_ref[0, :, j * LANES:(j + 1) * LANES] = out.astype(jnp.bfloat16)


def _win_attention(q, k, v, sink, wbias):
    b, seq, _ = q.shape
    nblk = seq // BLOCK
    prev = lambda bi, n: (bi, jnp.maximum(n - 1, 0), 0)
    cur = lambda bi, n: (bi, n, 0)
    nxt = lambda bi, n: (bi, jnp.minimum(n + 1, nblk - 1), 0)
    kv = lambda f: pl.BlockSpec((1, BLOCK, LANES), f)
    return pl.pallas_call(
        functools.partial(_win_kernel, nblk=nblk),
        grid=(b, nblk),
        in_specs=[pl.BlockSpec(memory_space=pltpu.SMEM),
                  pl.BlockSpec((1, BLOCK, GROUP_WIDTH), cur),
                  kv(prev), kv(cur), kv(nxt), kv(prev), kv(cur), kv(nxt),
                  pl.BlockSpec((4, BLOCK, 3 * BLOCK), lambda bi, n: (0, 0, 0))],
        out_specs=pl.BlockSpec((1, BLOCK, GROUP_WIDTH), cur),
        out_shape=jax.ShapeDtypeStruct((b, seq, GROUP_WIDTH), jnp.bfloat16),
        compiler_params=_cparams(("parallel", "arbitrary")),
        name="window_attention",
    )(sink, q, k, k, k, v, v, v, wbias)


def _dense_chain(q, k, v, log2_scale):
    s = lax.dot_general(q, k, _NT, preferred_element_type=jnp.float32) * log2_scale
    p = jnp.exp2(s - jnp.max(s, axis=-1, keepdims=True))
    r = 1.0 / jnp.sum(p, axis=-1, keepdims=True)
    return jnp.dot(p.astype(jnp.bfloat16), v, preferred_element_type=jnp.float32) * r


def _gqa_kernel(q_ref, k_ref, v_ref, o_ref, *, tq):
    k = k_ref[0]
    v = v_ref[0]
    lane = lax.broadcasted_iota(jnp.int32, (tq, LANES), 1)
    for j in range(2):
        q = q_ref[0, :, j * LANES:(j + 1) * LANES]
        zero = jnp.zeros_like(q)
        lo = _dense_chain(jnp.where(lane < HEAD_DIM, q, zero), k, v, LOG2E)
        hi = _dense_chain(jnp.where(lane >= HEAD_DIM, q, zero), k, v, LOG2E)
        o_ref[0, :, j * LANES:(j + 1) * LANES] = jnp.where(lane < HEAD_DIM, lo, hi).astype(jnp.bfloat16)


def _gqa_attention(q, k, v, tq):
    b, seq, _ = q.shape
    return pl.pallas_call(
        functools.partial(_gqa_kernel, tq=tq),
        grid=(b, seq // tq),
        in_specs=[pl.BlockSpec((1, tq, GROUP_WIDTH), lambda bi, i: (bi, i, 0)),
                  pl.BlockSpec((1, seq, LANES), lambda bi, i: (bi, 0, 0)),
                  pl.BlockSpec((1, seq, LANES), lambda bi, i: (bi, 0, 0))],
        out_specs=pl.BlockSpec((1, tq, GROUP_WIDTH), lambda bi, i: (bi, i, 0)),
        out_shape=jax.ShapeDtypeStruct((b, seq, GROUP_WIDTH), jnp.bfloat16),
        compiler_params=_cparams(("parallel", "arbitrary")),
        name="axial_gqa_attention",
    )(q, k, v)


def _mla_kernel(q_ref, k_ref, v_ref, o_ref, *, tq):
    lane = lax.broadcasted_iota(jnp.int32, (tq, LANES), 1)
    log2_scale = (MLA_NOPE + MLA_ROPE) ** -0.5 * LOG2E
    for pr in range(2):
        v = v_ref[0, :, pr * LANES:(pr + 1) * LANES]
        outs = []
        for hh in (2 * pr, 2 * pr + 1):
            q = q_ref[0, :, hh * LANES:(hh + 1) * LANES]
            k = k_ref[0, :, hh * LANES:(hh + 1) * LANES]
            outs.append(_dense_chain(q, k, v, log2_scale))
        o_ref[0, :, pr * LANES:(pr + 1) * LANES] = jnp.where(
            lane < HEAD_DIM, outs[0], outs[1]).astype(jnp.bfloat16)


def _mla_attention(q, k, v, tq):
    b, seq, _ = q.shape
    return pl.pallas_call(
        functools.partial(_mla_kernel, tq=tq),
        grid=(b, seq // tq),
        in_specs=[pl.BlockSpec((1, tq, 4 * LANES), lambda bi, i: (bi, i, 0)),
                  pl.BlockSpec((1, seq, 4 * LANES), lambda bi, i: (bi, 0, 0)),
                  pl.BlockSpec((1, seq, GROUP_WIDTH), lambda bi, i: (bi, 0, 0))],
        out_specs=pl.BlockSpec((1, tq, GROUP_WIDTH), lambda bi, i: (bi, i, 0)),
        out_shape=jax.ShapeDtypeStruct((b, seq, GROUP_WIDTH), jnp.bfloat16),
        compiler_params=_cparams(("parallel", "arbitrary")),
        name="latent_attention",
    )(q, k, v)


def _outproj_kernel(x_ref, a_ref, b_ref, c_ref, d_ref, w_ref, g_ref, o_ref):
    y = jnp.dot(a_ref[...], w_ref[0], preferred_element_type=jnp.float32)
    y += jnp.dot(b_ref[...], w_ref[1], preferred_element_type=jnp.float32)
    y += jnp.dot(c_ref[...], w_ref[2], preferred_element_type=jnp.float32)
    y += jnp.dot(d_ref[...], w_ref[3], preferred_element_type=jnp.float32)
    o_ref[...] = x_ref[...] + _rms(y, g_ref[...])


def _outproj(x2, oa, ob, oc, od, w, g, tm):
    n = x2.shape[0]
    row = lambda wd: pl.BlockSpec((tm, wd), lambda i: (i, 0))
    return pl.pallas_call(
        _outproj_kernel,
        grid=(n // tm,),
        in_specs=[row(D_MODEL), row(256), row(256), row(256), row(256),
                  pl.BlockSpec(w.shape, lambda i: (0, 0, 0)),
                  pl.BlockSpec(g.shape, lambda i: (0, 0))],
        out_specs=row(D_MODEL),
        out_shape=jax.ShapeDtypeStruct((n, D_MODEL), jnp.float32),
        compiler_params=_cparams(("parallel",)),
        name="out_projection",
    )(x2, oa, ob, oc, od, w, g)


HALO = 8


def _ffn_kernel(x_ref, xp_ref, xn_ref, gpre_ref, wup_ref, cw_ref, cb_ref, wdn_ref, gpost_ref,
                o_ref, h_ref, gate_ref, acc_ref, *, tm, fc, tiles_per_seq):
    i = pl.program_id(0)
    j = pl.program_id(1)
    bf = jnp.bfloat16

    @pl.when(j == 0)
    def _():
        g = gpre_ref[...]
        first = (i % tiles_per_seq) == 0
        last = (i % tiles_per_seq) == tiles_per_seq - 1
        hp = _rms(xp_ref[...], g)
        hn = _rms(xn_ref[...], g)
        h_ref[0:HALO, :] = jnp.where(first, 0.0, hp).astype(bf)
        h_ref[HALO:HALO + tm, :] = _rms(x_ref[...], g).astype(bf)
        h_ref[HALO + tm:, :] = jnp.where(last, 0.0, hn).astype(bf)
        acc_ref[...] = jnp.zeros_like(acc_ref)

    up = jnp.dot(h_ref[...], wup_ref[...], preferred_element_type=jnp.float32)
    gate_ref[...] = up[:, :fc]
    val = up[HALO:HALO + tm, fc:]
    cw = cw_ref[...]
    gate = (gate_ref[pl.ds(HALO - 1, tm), :] * cw[0:1]
            + gate_ref[pl.ds(HALO, tm), :] * cw[1:2]
            + gate_ref[pl.ds(HALO + 1, tm), :] * cw[2:3] + cb_ref[...])
    act = 0.5 * gate * (1.0 + lax.erf(gate * (2.0 ** -0.5))) * val
    acc_ref[...] += jnp.dot(act.astype(bf), wdn_ref[...], preferred_element_type=jnp.float32)

    @pl.when(j == pl.num_programs(1) - 1)
    def _():
        o_ref[...] = x_ref[...] + _rms(acc_ref[...], gpost_ref[...])


def _ffn(x2, gpre, wup, cw, cb, wdn, gpost, seq, tm, fc):
    n = x2.shape[0]
    nf = D_FF // fc
    tph = tm // HALO
    nh = n // HALO
    return pl.pallas_call(
        functools.partial(_ffn_kernel, tm=tm, fc=fc, tiles_per_seq=seq // tm),
        grid=(n // tm, nf),
        in_specs=[pl.BlockSpec((tm, D_MODEL), lambda i, j: (i, 0)),
                  pl.BlockSpec((HALO, D_MODEL), lambda i, j: (jnp.maximum(i * tph - 1, 0), 0)),
                  pl.BlockSpec((HALO, D_MODEL), lambda i, j: (jnp.minimum((i + 1) * tph, nh - 1), 0)),
                  pl.BlockSpec((1, D_MODEL), lambda i, j: (0, 0)),
                  pl.BlockSpec((D_MODEL, 2 * fc), lambda i, j: (0, j)),
                  pl.BlockSpec((3, fc), lambda i, j: (0, j)),
                  pl.BlockSpec((1, fc), lambda i, j: (0, j)),
                  pl.BlockSpec((fc, D_MODEL), lambda i, j: (j, 0)),
                  pl.BlockSpec((1, D_MODEL), lambda i, j: (0, 0))],
        out_specs=pl.BlockSpec((tm, D_MODEL), lambda i, j: (i, 0)),
        out_shape=jax.ShapeDtypeStruct((n, D_MODEL), jnp.float32),
        scratch_shapes=[pltpu.VMEM((tm + 2 * HALO, D_MODEL), jnp.bfloat16),
                        pltpu.VMEM((tm + 2 * HALO, fc), jnp.float32),
                        pltpu.VMEM((tm, D_MODEL), jnp.float32)],
        compiler_params=_cparams(("parallel", "arbitrary")),
        name="conv_glu",
    )(x2, x2, x2, gpre, wup, cw, cb, wdn, gpost)


_GQA_HEAD_ORDER = (0, 2, 1, 3)


def _reorder_heads(w, order):
    lead = w.shape[:-1]
    nh = w.shape[-1] // HEAD_DIM
    w = w.reshape(lead + (nh, HEAD_DIM))
    return jnp.take(w, jnp.array(order), axis=-2).reshape(lead + (nh * HEAD_DIM,))


def _axial_perm(w):
    lead = w.shape[:-1]
    nh = w.shape[-1] // HEAD_DIM
    w = w.reshape(lead + (nh, 2, 2, 16))
    return jnp.swapaxes(w, -3, -2).reshape(lead + (nh * HEAD_DIM,))


def _prep_w_in(w_in):
    bq = _reorder_heads(w_in[..., 768:1024], _GQA_HEAD_ORDER)
    cq = _axial_perm(_reorder_heads(w_in[..., 1280:1536], _GQA_HEAD_ORDER))
    ck = _axial_perm(w_in[..., 1536:1664])
    kr = w_in[..., 2176:2208]
    z = lambda wd: jnp.zeros(w_in.shape[:-1] + (wd,), w_in.dtype)
    return jnp.concatenate(
        [w_in[..., 0:768], bq, w_in[..., 1024:1280], cq, ck, w_in[..., 1664:1792],
         w_in[..., 1792:2176], z(MLA_NOPE), kr, z(LANES - MLA_NOPE - MLA_ROPE)],
        axis=-1).astype(jnp.bfloat16)


def _rope_tables(cos, sin, lead_ones, trail_ones):
    s = cos.shape[0]
    one = lambda wd: jnp.ones((s, wd), jnp.float32)
    zero = lambda wd: jnp.zeros((s, wd), jnp.float32)
    c = jnp.concatenate([one(lead_ones), cos, cos, one(trail_ones)], axis=1)
    sn = jnp.concatenate([zero(lead_ones), -sin, zero(cos.shape[1]), zero(trail_ones)], axis=1)
    sp = jnp.concatenate([zero(lead_ones), zero(cos.shape[1]), sin, zero(trail_ones)], axis=1)
    return jnp.stack([c, sn, sp])


def _angles(pos, dim):
    inv = ROPE_THETA ** (-jnp.arange(0, dim, 2, dtype=jnp.float32) / dim)
    ang = pos.astype(jnp.float32)[:, None] * inv[None, :]
    return jnp.cos(ang), jnp.sin(ang)


def _forward(x, rel_bias, attn_pre_norm, w_in, diff_lambda, diff_subln, win_sink, ax_q_norm, ax_k_norm,
             mla_q_norm, mla_kv_norm, mla_w_uq, mla_w_ukv, w_out, attn_post_norm, ffn_pre_norm,
             ffn_w_up, ffn_conv_w, ffn_conv_b, ffn_w_down, ffn_post_norm, *, tq_a, tq, tm, fc):
    bsz, seq, _ = x.shape
    depth = w_in.shape[0]
    n = bsz * seq
    bf = jnp.bfloat16

    pos = jnp.arange(seq, dtype=jnp.int32)
    rows = seq // GRID_W
    row_ids = jnp.repeat(jnp.arange(rows, dtype=jnp.int32), GRID_W)
    col_ids = jnp.tile(jnp.arange(GRID_W, dtype=jnp.int32), rows)
    rcos, rsin = _angles(row_ids, HEAD_DIM // 2)
    ccos, csin = _angles(col_ids, HEAD_DIM // 2)
    ctab = _rope_tables(jnp.concatenate([rcos, ccos], 1), jnp.concatenate([rsin, csin], 1), 0, 0)
    ctab = jnp.tile(ctab, (1, 1, 4))
    mcos, msin = _angles(pos, MLA_ROPE)
    dtab = _rope_tables(mcos, msin, MLA_NOPE, LANES - MLA_NOPE - MLA_ROPE)

    gw = 2 * seq - tq_a
    qi = jnp.arange(tq_a, dtype=jnp.int32)[:, None]
    rel_a = jnp.arange(gw, dtype=jnp.int32)[None, :] - (seq - tq_a) - qi
    gbias = _bias_tiles(rel_bias, _bucket(rel_a), 0, 4, LOG2E)
    rel_b = jnp.arange(3 * BLOCK, dtype=jnp.int32)[None, :] - BLOCK - jnp.arange(BLOCK, dtype=jnp.int32)[:, None]
    idx_b = jnp.where(jnp.abs(rel_b) <= WINDOW, _bucket(rel_b), -1)
    wbias = _bias_tiles(rel_bias, idx_b, 4, 4, 1.0)

    w_in_p = _prep_w_in(w_in)
    cq_g = jnp.tile(_axial_perm(ax_q_norm), (1, 4))[:, None, :]
    ck_g = jnp.tile(_axial_perm(ax_k_norm), (1, 2))[:, None, :]
    wuq = jnp.pad(mla_w_uq.reshape(depth, MLA_Q_RANK, 4, MLA_NOPE + MLA_ROPE),
                  ((0, 0), (0, 0), (0, 0), (0, LANES - MLA_NOPE - MLA_ROPE))
                  ).reshape(depth, MLA_Q_RANK, 4 * LANES).astype(bf)
    wukv = mla_w_ukv.reshape(depth, MLA_KV_RANK, 4, 2, HEAD_DIM)
    wukvk = jnp.pad(wukv[:, :, :, 0], ((0, 0), (0, 0), (0, 0), (0, HEAD_DIM))
                    ).reshape(depth, MLA_KV_RANK, 4 * LANES).astype(bf)
    wukvv = wukv[:, :, :, 1].reshape(depth, MLA_KV_RANK, GROUP_WIDTH).astype(bf)
    wo = w_out.reshape(depth, 4, 4, HEAD_DIM, D_MODEL)
    wo = jnp.stack([wo[:, 0], jnp.take(wo[:, 1], jnp.array(_GQA_HEAD_ORDER), axis=1),
                    jnp.take(wo[:, 2], jnp.array(_GQA_HEAD_ORDER), axis=1), wo[:, 3]], axis=1)
    wo = wo.reshape(depth, 4, GROUP_WIDTH, D_MODEL).astype(bf)
    nf = D_FF // fc
    wup = ffn_w_up.reshape(depth, D_MODEL, 2, nf, fc)
    wup = jnp.swapaxes(wup, 2, 3).reshape(depth, D_MODEL, 2 * D_FF).astype(bf)
    wdn = ffn_w_down.astype(bf)
    sub = jnp.tile(diff_subln, (1, 2))[:, None, :]

    x2 = x.reshape(n, D_MODEL)
    r3 = lambda a: a.reshape(bsz, seq, a.shape[-1])
    for l in range(depth):
        lam_init = 0.8 - 0.6 * math.exp(-0.3 * l)
        (aq, ak, av, bq, bk, bv, cq, ck, cv, dq, dk, dv) = _pre(
            x2, attn_pre_norm[l][None], w_in_p[l], cq_g[l], ck_g[l], ctab,
            mla_q_norm[l][None], mla_kv_norm[l][None], wuq[l], wukvk[l], wukvv[l], dtab, seq, tm)
        oa = _diff_attention(r3(aq), r3(ak), r3(av), gbias, diff_lambda[l][None], sub[l],
                             lam_init, tq_a)
        ob = _win_attention(r3(bq), r3(bk), r3(bv), win_sink[l], wbias)
        oc = _gqa_attention(r3(cq), r3(ck), r3(cv), tq)
        od = _mla_attention(r3(dq), r3(dk), r3(dv), tq)
        x2 = _outproj(x2, oa.reshape(n, -1), ob.reshape(n, -1), oc.reshape(n, -1),
                      od.reshape(n, -1), wo[l], attn_post_norm[l][None], tm)
        x2 = _ffn(x2, ffn_pre_norm[l][None], wup[l], ffn_conv_w[l], ffn_conv_b[l][None], wdn[l],
                  ffn_post_norm[l][None], seq, tm, fc)
    return x2.reshape(bsz, seq, D_MODEL)


def kernel(x, rel_bias, attn_pre_norm, w_in, diff_lambda, diff_subln, win_sink, ax_q_norm, ax_k_norm,
           mla_q_norm, mla_kv_norm, mla_w_uq, mla_w_ukv, w_out, attn_post_norm, ffn_pre_norm,
           ffn_w_up, ffn_conv_w, ffn_conv_b, ffn_w_down, ffn_post_norm):
    return _forward(x, rel_bias, attn_pre_norm, w_in, diff_lambda, diff_subln, win_sink, ax_q_norm,
                    ax_k_norm, mla_q_norm, mla_kv_norm, mla_w_uq, mla_w_ukv, w_out, attn_post_norm,
                    ffn_pre_norm, ffn_w_up, ffn_conv_w, ffn_conv_b, ffn_w_down, ffn_post_norm,
                    tq_a=256, tq=256, tm=512, fc=256)
```

```python
import functools
import math

import jax
import jax.numpy as jnp
import numpy as np
from jax import lax
from jax.experimental import pallas as pl
from jax.experimental.pallas import tpu as pltpu

D_MODEL = 1024
HEAD_DIM = 64
GROUP_WIDTH = 256
BLOCK = 128
DIFF_HALF = 32
WINDOW = 128
GRID_W = 64
ROPE_THETA = 10000.0
MLA_Q_RANK = 256
MLA_KV_RANK = 128
MLA_NOPE = 64
MLA_ROPE = 32
REL_BUCKETS = 32
REL_MAX_DIST = 128
D_FF = 2816
EPS = 1e-6
LANES = 128
LOG2E = 1.4426950408889634
NEG_BIG = -1e30
VMEM_LIMIT = 56 * 1024 * 1024

PROJ_W = 768 + 512 + 512 + 256 + 128 + 128


def _cparams(sem):
    return pltpu.CompilerParams(dimension_semantics=sem, vmem_limit_bytes=VMEM_LIMIT)


def _rms(x, g):
    return x * lax.rsqrt(jnp.mean(x * x, axis=-1, keepdims=True) + EPS) * g


def _group_rms(x, ngroups):
    grp = lax.broadcasted_iota(jnp.int32, x.shape, 1) // HEAD_DIM
    x2 = x * x
    inv = jnp.zeros_like(x)
    for h in range(ngroups):
        m = grp == h
        ss = jnp.sum(jnp.where(m, x2, 0.0), axis=-1, keepdims=True)
        inv = jnp.where(m, lax.rsqrt(ss * (1.0 / HEAD_DIM) + EPS), inv)
    return x * inv


def _rope(x, c, sn, sp, half):
    w = x.shape[-1]
    return x * c + pltpu.roll(x, w - half, 1) * sn + pltpu.roll(x, half, 1) * sp


def _bias_kernel(tab_ref, idx_ref, o_ref, *, head0, scale):
    h = pl.program_id(0)
    idx = idx_ref[...]
    acc = jnp.zeros(idx.shape, jnp.float32)
    for b in range(REL_BUCKETS):
        acc = jnp.where(idx == b, tab_ref[b, head0 + h], acc)
    o_ref[0] = jnp.where(idx < 0, NEG_BIG, acc * scale)


def _bias_tiles(rel_bias, idx, head0, nheads, scale):
    r, c = idx.shape
    return pl.pallas_call(
        functools.partial(_bias_kernel, head0=head0, scale=scale),
        grid=(nheads,),
        in_specs=[pl.BlockSpec(memory_space=pltpu.SMEM),
                  pl.BlockSpec((r, c), lambda h: (0, 0))],
        out_specs=pl.BlockSpec((1, r, c), lambda h: (h, 0, 0)),
        out_shape=jax.ShapeDtypeStruct((nheads, r, c), jnp.float32),
        compiler_params=_cparams(("arbitrary",)),
        name="rel_bias_tiles",
    )(rel_bias, idx)


def _bucket(rel):
    half = REL_BUCKETS // 2
    max_exact = half // 2
    n = jnp.abs(rel)
    nf = jnp.maximum(n, 1).astype(jnp.float32)
    large = max_exact + (jnp.log(nf / max_exact) / math.log(REL_MAX_DIST / max_exact)
                         * (half - max_exact)).astype(jnp.int32)
    large = jnp.minimum(large, half - 1)
    return jnp.where(rel > 0, half, 0) + jnp.where(n < max_exact, n, large)


def _pre_kernel(x_ref, g_ref, w_ref, cq_g_ref, ck_g_ref, ctab_ref, dq_g_ref, dkv_g_ref,
                wuq_ref, wukvk_ref, wukvv_ref, dtab_ref,
                aq_ref, ak_ref, av_ref, bq_ref, bk_ref, bv_ref, cq_ref, ck_ref, cv_ref,
                dq_ref, dk_ref, dv_ref):
    bf = jnp.bfloat16
    h = _rms(x_ref[...], g_ref[...]).astype(bf)
    proj = lambda lo, hi: jnp.dot(h, w_ref[:, lo:hi], preferred_element_type=jnp.float32)
    pc = proj(1280, 1792)
    cc, csn, csp = ctab_ref[0], ctab_ref[1], ctab_ref[2]
    q = _group_rms(pc[:, 0:256], 4) * cq_g_ref[...]
    cq_ref[...] = (_rope(q, cc, csn, csp, 32) * 0.125).astype(bf)
    k = _group_rms(pc[:, 256:384], 2) * ck_g_ref[...]
    ck_ref[...] = _rope(k, cc[:, :128], csn[:, :128], csp[:, :128], 32).astype(bf)
    cv_ref[...] = pc[:, 384:512].astype(bf)
    pd = proj(1792, 2304)
    dc, dsn, dsp = dtab_ref[0], dtab_ref[1], dtab_ref[2]
    cq = _rms(pd[:, 0:256], dq_g_ref[...]).astype(bf)
    q = jnp.dot(cq, wuq_ref[...], preferred_element_type=jnp.float32)
    rep = lambda t: jnp.concatenate([t] * 4, axis=1)
    dq_ref[...] = _rope(q, rep(dc), rep(dsn), rep(dsp), 16).astype(bf)
    ckv = _rms(pd[:, 256:384], dkv_g_ref[...]).astype(bf)
    kr = _rope(pd[:, 384:512], dc, dsn, dsp, 16)
    kn = jnp.dot(ckv, wukvk_ref[...], preferred_element_type=jnp.float32)
    dk_ref[...] = (kn + rep(kr)).astype(bf)
    dv_ref[...] = jnp.dot(ckv, wukvv_ref[...], preferred_element_type=jnp.float32).astype(bf)
    pa = proj(0, 768)
    aq_ref[...] = pa[:, 0:256].astype(bf)
    ak_ref[...] = pa[:, 256:512].astype(bf)
    av_ref[...] = pa[:, 512:768].astype(bf)
    pb = proj(768, 1280)
    bq_ref[...] = (pb[:, 0:256] * 0.125).astype(bf)
    bk_ref[...] = pb[:, 256:384].astype(bf)
    bv_ref[...] = pb[:, 384:512].astype(bf)


def _pre(x2, g, w, cq_g, ck_g, ctab, dq_g, dkv_g, wuq, wukvk, wukvv, dtab, seq, tm):
    n = x2.shape[0]
    nt = seq // tm
    row = lambda wd: pl.BlockSpec((tm, wd), lambda i: (i, 0))
    full = lambda a: pl.BlockSpec(a.shape, lambda i: (0,) * a.ndim)
    widths = (256, 256, 256, 256, 128, 128, 256, 128, 128, 512, 512, 256)
    return pl.pallas_call(
        _pre_kernel,
        grid=(n // tm,),
        in_specs=[row(D_MODEL), full(g), full(w), full(cq_g), full(ck_g),
                  pl.BlockSpec((3, tm, 256), lambda i: (0, i % nt, 0)),
                  full(dq_g), full(dkv_g), full(wuq), full(wukvk), full(wukvv),
                  pl.BlockSpec((3, tm, 128), lambda i: (0, i % nt, 0))],
        out_specs=[row(wd) for wd in widths],
        out_shape=[jax.ShapeDtypeStruct((n, wd), jnp.bfloat16) for wd in widths],
        compiler_params=_cparams(("parallel",)),
        name="pre_attention",
    )(x2, g, w, cq_g, ck_g, ctab, dq_g, dkv_g, wuq, wukvk, wukvv, dtab)


def _softmax_parts(s, extra=None):
    m = jnp.max(s, axis=-1, keepdims=True)
    if extra is not None:
        m = jnp.maximum(m, extra)
    p = jnp.exp(s - m)
    l = jnp.sum(p, axis=-1, keepdims=True)
    if extra is not None:
        l = l + jnp.exp(extra - m)
    return p.astype(jnp.bfloat16), 1.0 / l


_NT = (((1,), (1,)), ((), ()))


def _diff_kernel(q_ref, k_ref, v_ref, g_ref, dl_ref, sub_ref, o_ref, *, lam_init, seq, tq):
    i = pl.program_id(2)
    nq = seq // tq
    q = q_ref[0]
    k = k_ref[0]
    v = v_ref[0]
    lane = lax.broadcasted_iota(jnp.int32, (tq, LANES), 1)
    dl = dl_ref[0]
    lam = (jnp.exp(jnp.sum(dl[0:1] * dl[1:2], axis=-1, keepdims=True))
           - jnp.exp(jnp.sum(dl[2:3] * dl[3:4], axis=-1, keepdims=True)) + lam_init)
    start = pl.multiple_of((nq - 1 - i) * tq, LANES)
    os = []
    for hc in range(4):
        lo = hc * DIFF_HALF
        qm = jnp.where((lane >= lo) & (lane < lo + DIFF_HALF), q, jnp.zeros_like(q))
        s = lax.dot_general(qm, k, _NT, preferred_element_type=jnp.float32)
        s = s * (DIFF_HALF ** -0.5 * LOG2E) + g_ref[hc // 2, :, pl.ds(start, seq)]
        p = jnp.exp2(s - jnp.max(s, axis=-1, keepdims=True))
        r = 1.0 / jnp.sum(p, axis=-1, keepdims=True)
        os.append(jnp.dot(p.astype(jnp.bfloat16), v, preferred_element_type=jnp.float32) * r)
    out = jnp.zeros((tq, LANES), jnp.float32)
    for hh in range(2):
        oh = os[2 * hh] - lam * os[2 * hh + 1]
        mh = (lane >= hh * HEAD_DIM) & (lane < (hh + 1) * HEAD_DIM)
        ss = jnp.sum(jnp.where(mh, oh * oh, 0.0), axis=-1, keepdims=True) * (1.0 / HEAD_DIM)
        out = jnp.where(mh, oh * lax.rsqrt(ss + EPS), out)
    o_ref[0] = (out * sub_ref[...] * (1.0 - lam_init)).astype(jnp.bfloat16)


def _diff_attention(q, k, v, gbias, dl, sub, lam_init, tq):
    b, seq, _ = q.shape
    gw = gbias.shape[-1]
    return pl.pallas_call(
        functools.partial(_diff_kernel, lam_init=lam_init, seq=seq, tq=tq),
        grid=(2, b, seq // tq),
        in_specs=[pl.BlockSpec((1, tq, LANES), lambda p, bi, i: (bi, i, p)),
                  pl.BlockSpec((1, seq, LANES), lambda p, bi, i: (bi, 0, p)),
                  pl.BlockSpec((1, seq, LANES), lambda p, bi, i: (bi, 0, p)),
                  pl.BlockSpec((2, tq, gw), lambda p, bi, i: (p, 0, 0)),
                  pl.BlockSpec((1, 4, DIFF_HALF), lambda p, bi, i: (0, 0, 0)),
                  pl.BlockSpec((1, LANES), lambda p, bi, i: (0, 0))],
        out_specs=pl.BlockSpec((1, tq, LANES), lambda p, bi, i: (bi, i, p)),
        out_shape=jax.ShapeDtypeStruct((b, seq, GROUP_WIDTH), jnp.bfloat16),
        compiler_params=_cparams(("parallel", "parallel", "arbitrary")),
        name="diff_attention",
    )(q, k, v, gbias, dl, sub)


def _win_kernel(sink_ref, q_ref, kp_ref, kc_ref, kn_ref, vp_ref, vc_ref, vn_ref, wb_ref, o_ref,
                *, nstep, nb):
    n = pl.program_id(1)
    kext = jnp.concatenate([kp_ref[0], kc_ref[0], kn_ref[0]], axis=0)
    vext = jnp.concatenate([vp_ref[0], vc_ref[0], vn_ref[0]], axis=0)
    lane = lax.broadcasted_iota(jnp.int32, (BLOCK, LANES), 1)
    col = lax.broadcasted_iota(jnp.int32, (BLOCK, 3 * BLOCK), 1)
    for blk in range(nb):
        rows = slice(blk * BLOCK, (blk + 1) * BLOCK)
        kcat = kext[blk * BLOCK:(blk + 3) * BLOCK]
        vcat = vext[blk * BLOCK:(blk + 3) * BLOCK]
        for j in range(2):
            q = q_ref[0, rows, j * LANES:(j + 1) * LANES]
            zero = jnp.zeros_like(q)
            qs = jnp.concatenate([jnp.where(lane < HEAD_DIM, q, zero),
                                  jnp.where(lane >= HEAD_DIM, q, zero)], axis=0)
            s = lax.dot_general(qs, kcat, _NT, preferred_element_type=jnp.float32)
            ps, rs = [], []
            for g in range(2):
                head = j + 2 * g
                sg = s[g * BLOCK:(g + 1) * BLOCK] + wb_ref[head]
                if blk == 0:
                    sg = jnp.where((col < BLOCK) & (n == 0), NEG_BIG, sg)
                if blk == nb - 1:
                    sg = jnp.where((col >= 2 * BLOCK) & (n == nstep - 1), NEG_BIG, sg)
                p, r = _softmax_parts(sg, extra=sink_ref[head])
                ps.append(p)
                rs.append(r)
            o = jnp.dot(jnp.concatenate(ps, axis=0), vcat, preferred_element_type=jnp.float32)
            out = jnp.where(lane < HEAD_DIM, o[:BLOCK] * rs[0], o[BLOCK:] * rs[1])
            o_ref[0, rows, j * LANES:(j + 1) * LANES] = out.astype(jnp.bfloat16)


def _win_attention(q, k, v, sink, wbias, tw):
    b, seq, _ = q.shape
    nblk = seq // BLOCK
    nb = tw // BLOCK
    prev = lambda bi, n: (bi, jnp.maximum(n * nb - 1, 0), 0)
    cur = lambda bi, n: (bi, n, 0)
    nxt = lambda bi, n: (bi, jnp.minimum((n + 1) * nb, nblk - 1), 0)
    halo = lambda f: pl.BlockSpec((1, BLOCK, LANES), f)
    body = pl.BlockSpec((1, tw, LANES), cur)
    return pl.pallas_call(
        functools.partial(_win_kernel, nstep=seq // tw, nb=nb),
        grid=(b, seq // tw),
        in_specs=[pl.BlockSpec(memory_space=pltpu.SMEM),
                  pl.BlockSpec((1, tw, GROUP_WIDTH), cur),
                  halo(prev), body, halo(nxt), halo(prev), body, halo(nxt),
                  pl.BlockSpec((4, BLOCK, 3 * BLOCK), lambda bi, n: (0, 0, 0))],
        out_specs=pl.BlockSpec((1, tw, GROUP_WIDTH), cur),
        out_shape=jax.ShapeDtypeStruct((b, seq, GROUP_WIDTH), jnp.bfloat16),
        compiler_params=_cparams(("parallel", "arbitrary")),
        name="window_attention",
    )(sink, q, k, k, k, v, v, v, wbias)


def _dense_chain(q, k, v, log2_scale):
    s = lax.dot_general(q, k, _NT, preferred_element_type=jnp.float32) * log2_scale
    p = jnp.exp2(s - jnp.max(s, axis=-1, keepdims=True))
    r = 1.0 / jnp.sum(p, axis=-1, keepdims=True)
    return jnp.dot(p.astype(jnp.bfloat16), v, preferred_element_type=jnp.float32) * r


def _gqa_kernel(q_ref, k_ref, v_ref, o_ref, *, tq):
    k = k_ref[0]
    v = v_ref[0]
    lane = lax.broadcasted_iota(jnp.int32, (tq, LANES), 1)
    for j in range(2):
        q = q_ref[0, :, j * LANES:(j + 1) * LANES]
        zero = jnp.zeros_like(q)
        lo = _dense_chain(jnp.where(lane < HEAD_DIM, q, zero), k, v, LOG2E)
        hi = _dense_chain(jnp.where(lane >= HEAD_DIM, q, zero), k, v, LOG2E)
        o_ref[0, :, j * LANES:(j + 1) * LANES] = jnp.where(lane < HEAD_DIM, lo, hi).astype(jnp.bfloat16)


def _gqa_attention(q, k, v, tq):
    b, seq, _ = q.shape
    return pl.pallas_call(
        functools.partial(_gqa_kernel, tq=tq),
        grid=(b, seq // tq),
        in_specs=[pl.BlockSpec((1, tq, GROUP_WIDTH), lambda bi, i: (bi, i, 0)),
                  pl.BlockSpec((1, seq, LANES), lambda bi, i: (bi, 0, 0)),
                  pl.BlockSpec((1, seq, LANES), lambda bi, i: (bi, 0, 0))],
        out_specs=pl.BlockSpec((1, tq, GROUP_WIDTH), lambda bi, i: (bi, i, 0)),
        out_shape=jax.ShapeDtypeStruct((b, seq, GROUP_WIDTH), jnp.bfloat16),
        compiler_params=_cparams(("parallel", "arbitrary")),
        name="axial_gqa_attention",
    )(q, k, v)


def _mla_kernel(q_ref, k_ref, v_ref, o_ref, *, tq):
    lane = lax.broadcasted_iota(jnp.int32, (tq, LANES), 1)
    log2_scale = (MLA_NOPE + MLA_ROPE) ** -0.5 * LOG2E
    for pr in range(2):
        v = v_ref[0, :, pr * LANES:(pr + 1) * LANES]
        outs = []
        for hh in (2 * pr, 2 * pr + 1):
            q = q_ref[0, :, hh * LANES:(hh + 1) * LANES]
            k = k_ref[0, :, hh * LANES:(hh + 1) * LANES]
            outs.append(_dense_chain(q, k, v, log2_scale))
        o_ref[0, :, pr * LANES:(pr + 1) * LANES] = jnp.where(
            lane < HEAD_DIM, outs[0], outs[1]).astype(jnp.bfloat16)


def _mla_attention(q, k, v, tq):
    b, seq, _ = q.shape
    return pl.pallas_call(
        functools.partial(_mla_kernel, tq=tq),
        grid=(b, seq // tq),
        in_specs=[pl.BlockSpec((1, tq, 4 * LANES), lambda bi, i: (bi, i, 0)),
                  pl.BlockSpec((1, seq, 4 * LANES), lambda bi, i: (bi, 0, 0)),
                  pl.BlockSpec((1, seq, GROUP_WIDTH), lambda bi, i: (bi, 0, 0))],
        out_specs=pl.BlockSpec((1, tq, GROUP_WIDTH), lambda bi, i: (bi, i, 0)),
        out_shape=jax.ShapeDtypeStruct((b, seq, GROUP_WIDTH), jnp.bfloat16),
        compiler_params=_cparams(("parallel", "arbitrary")),
        name="latent_attention",
    )(q, k, v)


def _outproj_kernel(x_ref, a_ref, b_ref, c_ref, d_ref, w_ref, g_ref, o_ref):
    y = jnp.dot(a_ref[...], w_ref[0], preferred_element_type=jnp.float32)
    y += jnp.dot(b_ref[...], w_ref[1], preferred_element_type=jnp.float32)
    y += jnp.dot(c_ref[...], w_ref[2], preferred_element_type=jnp.float32)
    y += jnp.dot(d_ref[...], w_ref[3], preferred_element_type=jnp.float32)
    o_ref[...] = x_ref[...] + _rms(y, g_ref[...])


def _outproj(x2, oa, ob, oc, od, w, g, tm):
    n = x2.shape[0]
    row = lambda wd: pl.BlockSpec((tm, wd), lambda i: (i, 0))
    return pl.pallas_call(
        _outproj_kernel,
        grid=(n // tm,),
        in_specs=[row(D_MODEL), row(256), row(256), row(256), row(256),
                  pl.BlockSpec(w.shape, lambda i: (0, 0, 0)),
                  pl.BlockSpec(g.shape, lambda i: (0, 0))],
        out_specs=row(D_MODEL),
        out_shape=jax.ShapeDtypeStruct((n, D_MODEL), jnp.float32),
        compiler_params=_cparams(("parallel",)),
        name="out_projection",
    )(x2, oa, ob, oc, od, w, g)


HALO = 8


def _ffn_kernel(x_ref, xp_ref, xn_ref, gpre_ref, wg_ref, wv_ref, cw_ref, cb_ref, wdn_ref, gpost_ref,
                o_ref, h_ref, *, tm, fc, tiles_per_seq):
    i = pl.program_id(0)
    bf = jnp.bfloat16
    g = gpre_ref[...]
    first = (i % tiles_per_seq) == 0
    last = (i % tiles_per_seq) == tiles_per_seq - 1
    h_ref[0:HALO, :] = jnp.where(first, 0.0, _rms(xp_ref[...], g)).astype(bf)
    h_ref[HALO:HALO + tm, :] = _rms(x_ref[...], g).astype(bf)
    h_ref[HALO + tm:, :] = jnp.where(last, 0.0, _rms(xn_ref[...], g)).astype(bf)
    y = jnp.zeros((tm, D_MODEL), jnp.float32)
    for c in range(D_FF // fc):
        cs = slice(c * fc, (c + 1) * fc)
        gate = jnp.dot(h_ref[...], wg_ref[:, cs], preferred_element_type=jnp.float32)
        val = jnp.dot(h_ref[HALO:HALO + tm, :], wv_ref[:, cs], preferred_element_type=jnp.float32)
        gate = (gate[HALO - 1:HALO - 1 + tm] * cw_ref[0:1, cs] + gate[HALO:HALO + tm] * cw_ref[1:2, cs]
                + gate[HALO + 1:HALO + 1 + tm] * cw_ref[2:3, cs] + cb_ref[:, cs])
        act = 0.5 * gate * (1.0 + lax.erf(gate * (2.0 ** -0.5))) * val
        y = y + jnp.dot(act.astype(bf), wdn_ref[cs, :], preferred_element_type=jnp.float32)
    o_ref[...] = x_ref[...] + _rms(y, gpost_ref[...])


def _ffn(x2, gpre, wg, wv, cw, cb, wdn, gpost, seq, tm, fc):
    n = x2.shape[0]
    tph = tm // HALO
    nh = n // HALO
    resident = lambda a: pl.BlockSpec(a.shape, lambda i: (0,) * a.ndim,
                                      pipeline_mode=pl.Buffered(1))
    return pl.pallas_call(
        functools.partial(_ffn_kernel, tm=tm, fc=fc, tiles_per_seq=seq // tm),
        grid=(n // tm,),
        in_specs=[pl.BlockSpec((tm, D_MODEL), lambda i: (i, 0)),
                  pl.BlockSpec((HALO, D_MODEL), lambda i: (jnp.maximum(i * tph - 1, 0), 0)),
                  pl.BlockSpec((HALO, D_MODEL), lambda i: (jnp.minimum((i + 1) * tph, nh - 1), 0)),
                  resident(gpre), resident(wg), resident(wv), resident(cw), resident(cb),
                  resident(wdn), resident(gpost)],
        out_specs=pl.BlockSpec((tm, D_MODEL), lambda i: (i, 0)),
        out_shape=jax.ShapeDtypeStruct((n, D_MODEL), jnp.float32),
        scratch_shapes=[pltpu.VMEM((tm + 2 * HALO, D_MODEL), jnp.bfloat16)],
        compiler_params=_cparams(("parallel",)),
        name="conv_glu",
    )(x2, x2, x2, gpre, wg, wv, cw, cb, wdn, gpost)


_GQA_HEAD_ORDER = (0, 2, 1, 3)


def _reorder_heads(w, order):
    lead = w.shape[:-1]
    nh = w.shape[-1] // HEAD_DIM
    w = w.reshape(lead + (nh, HEAD_DIM))
    return jnp.stack([w[..., o, :] for o in order], axis=-2).reshape(lead + (nh * HEAD_DIM,))


def _axial_perm(w):
    lead = w.shape[:-1]
    nh = w.shape[-1] // HEAD_DIM
    w = w.reshape(lead + (nh, 2, 2, 16))
    return jnp.swapaxes(w, -3, -2).reshape(lead + (nh * HEAD_DIM,))


def _prep_w_in(w_in):
    w_in = w_in.astype(jnp.bfloat16)
    bq =_reorder_heads(w_in[..., 768:1024], _GQA_HEAD_ORDER)
    cq = _axial_perm(_reorder_heads(w_in[..., 1280:1536], _GQA_HEAD_ORDER))
    ck = _axial_perm(w_in[..., 1536:1664])
    kr = w_in[..., 2176:2208]
    z = lambda wd: jnp.zeros(w_in.shape[:-1] + (wd,), w_in.dtype)
    return jnp.concatenate(
        [w_in[..., 0:768], bq, w_in[..., 1024:1280], cq, ck, w_in[..., 1664:1792],
         w_in[..., 1792:2176], z(MLA_NOPE), kr, z(LANES - MLA_NOPE - MLA_ROPE)], axis=-1)


def _rope_tables(cos, sin, lead_ones, trail_ones):
    s = cos.shape[0]
    one = lambda wd: jnp.ones((s, wd), jnp.float32)
    zero = lambda wd: jnp.zeros((s, wd), jnp.float32)
    c = jnp.concatenate([one(lead_ones), cos, cos, one(trail_ones)], axis=1)
    sn = jnp.concatenate([zero(lead_ones), -sin, zero(cos.shape[1]), zero(trail_ones)], axis=1)
    sp = jnp.concatenate([zero(lead_ones), zero(cos.shape[1]), sin, zero(trail_ones)], axis=1)
    return jnp.stack([c, sn, sp])


def _angles(pos, dim):
    inv = ROPE_THETA ** (-jnp.arange(0, dim, 2, dtype=jnp.float32) / dim)
    ang = pos.astype(jnp.float32)[:, None] * inv[None, :]
    return jnp.cos(ang), jnp.sin(ang)


def _forward(x, rel_bias, attn_pre_norm, w_in, diff_lambda, diff_subln, win_sink, ax_q_norm, ax_k_norm,
             mla_q_norm, mla_kv_norm, mla_w_uq, mla_w_ukv, w_out, attn_post_norm, ffn_pre_norm,
             ffn_w_up, ffn_conv_w, ffn_conv_b, ffn_w_down, ffn_post_norm, *, tq_a, tq, tw, tm, tf, fc):
    bsz, seq, _ = x.shape
    depth = w_in.shape[0]
    n = bsz * seq
    bf = jnp.bfloat16

    pos = jnp.arange(seq, dtype=jnp.int32)
    rows = seq // GRID_W
    row_ids = jnp.repeat(jnp.arange(rows, dtype=jnp.int32), GRID_W)
    col_ids = jnp.tile(jnp.arange(GRID_W, dtype=jnp.int32), rows)
    rcos, rsin = _angles(row_ids, HEAD_DIM // 2)
    ccos, csin = _angles(col_ids, HEAD_DIM // 2)
    ctab = _rope_tables(jnp.concatenate([rcos, ccos], 1), jnp.concatenate([rsin, csin], 1), 0, 0)
    ctab = jnp.tile(ctab, (1, 1, 4))
    mcos, msin = _angles(pos, MLA_ROPE)
    dtab = _rope_tables(mcos, msin, MLA_NOPE, LANES - MLA_NOPE - MLA_ROPE)

    gw = 2 * seq - tq_a
    qi = jnp.arange(tq_a, dtype=jnp.int32)[:, None]
    rel_a = jnp.arange(gw, dtype=jnp.int32)[None, :] - (seq - tq_a) - qi
    gbias = _bias_tiles(rel_bias, _bucket(rel_a), 0, 4, LOG2E)
    rel_b = jnp.arange(3 * BLOCK, dtype=jnp.int32)[None, :] - BLOCK - jnp.arange(BLOCK, dtype=jnp.int32)[:, None]
    idx_b = jnp.where(jnp.abs(rel_b) <= WINDOW, _bucket(rel_b), -1)
    wbias = _bias_tiles(rel_bias, idx_b, 4, 4, 1.0)

    w_in_p = _prep_w_in(w_in)
    cq_g = jnp.tile(_axial_perm(ax_q_norm), (1, 4))[:, None, :]
    ck_g = jnp.tile(_axial_perm(ax_k_norm), (1, 2))[:, None, :]
    wuq = jnp.pad(mla_w_uq.reshape(depth, MLA_Q_RANK, 4, MLA_NOPE + MLA_ROPE),
                  ((0, 0), (0, 0), (0, 0), (0, LANES - MLA_NOPE - MLA_ROPE))
                  ).reshape(depth, MLA_Q_RANK, 4 * LANES).astype(bf)
    wukv = mla_w_ukv.reshape(depth, MLA_KV_RANK, 4, 2, HEAD_DIM)
    wukvk = jnp.pad(wukv[:, :, :, 0], ((0, 0), (0, 0), (0, 0), (0, HEAD_DIM))
                    ).reshape(depth, MLA_KV_RANK, 4 * LANES).astype(bf)
    wukvv = wukv[:, :, :, 1].reshape(depth, MLA_KV_RANK, GROUP_WIDTH).astype(bf)
    wo = w_out.astype(bf).reshape(depth, 4, 4, HEAD_DIM, D_MODEL)
    slots = [(0, h) for h in range(4)] + [(1, h) for h in _GQA_HEAD_ORDER] \
        + [(2, h) for h in _GQA_HEAD_ORDER] + [(3, h) for h in range(4)]
    wo = jnp.stack([wo[:, m, h] for m, h in slots], axis=1).reshape(depth, 4, GROUP_WIDTH, D_MODEL)
    wg = ffn_w_up[:, :, :D_FF].astype(bf)
    wv = ffn_w_up[:, :, D_FF:].astype(bf)
    wdn = ffn_w_down.astype(bf)
    sub = jnp.tile(diff_subln, (1, 2))[:, None, :]

    x2 = x.reshape(n, D_MODEL)
    r3 = lambda a: a.reshape(bsz, seq, a.shape[-1])
    for l in range(depth):
        lam_init = 0.8 - 0.6 * math.exp(-0.3 * l)
        (aq, ak, av, bq, bk, bv, cq, ck, cv, dq, dk, dv) = _pre(
            x2, attn_pre_norm[l][None], w_in_p[l], cq_g[l], ck_g[l], ctab,
            mla_q_norm[l][None], mla_kv_norm[l][None], wuq[l], wukvk[l], wukvv[l], dtab, seq, tm)
        oa = _diff_attention(r3(aq), r3(ak), r3(av), gbias, diff_lambda[l][None], sub[l],
                             lam_init, tq_a)
        ob = _win_attention(r3(bq), r3(bk), r3(bv), win_sink[l], wbias, tw)
        oc = _gqa_attention(r3(cq), r3(ck), r3(cv), tq)
        od = _mla_attention(r3(dq), r3(dk), r3(dv), tq)
        x2 = _outproj(x2, oa.reshape(n, -1), ob.reshape(n, -1), oc.reshape(n, -1),
                      od.reshape(n, -1), wo[l], attn_post_norm[l][None], tm)
        x2 = _ffn(x2, ffn_pre_norm[l][None], wg[l], wv[l], ffn_conv_w[l], ffn_conv_b[l][None],
                  wdn[l], ffn_post_norm[l][None], seq, tf, fc)
    return x2.reshape(bsz, seq, D_MODEL)


def kernel(x, rel_bias, attn_pre_norm, w_in, diff_lambda, diff_subln, win_sink, ax_q_norm, ax_k_norm,
           mla_q_norm, mla_kv_norm, mla_w_uq, mla_w_ukv, w_out, attn_post_norm, ffn_pre_norm,
           ffn_w_up, ffn_conv_w, ffn_conv_b, ffn_w_down, ffn_post_norm):
    return _forward(x, rel_bias, attn_pre_norm, w_in, diff_lambda, diff_subln, win_sink, ax_q_norm,
                    ax_k_norm, mla_q_norm, mla_kv_norm, mla_w_uq, mla_w_ukv, w_out, attn_post_norm,
                    ffn_pre_norm, ffn_w_up, ffn_conv_w, ffn_conv_b, ffn_w_down, ffn_post_norm,
                    tq_a=256, tq=256, tw=512, tm=512, tf=512, fc=256)
```

```python
import functools
import math

import jax
import jax.numpy as jnp
from jax import lax
from jax.experimental import pallas as pl
from jax.experimental.pallas import tpu as pltpu

D_MODEL = 1024
HEAD_DIM = 64
GROUP_WIDTH = 256
BLOCK = 128
DIFF_HALF = 32
WINDOW = 128
GRID_W = 64
ROPE_THETA = 10000.0
MLA_Q_RANK = 256
MLA_KV_RANK = 128
MLA_NOPE = 64
MLA_ROPE = 32
REL_BUCKETS = 32
REL_MAX_DIST = 128
D_FF = 2816
EPS = 1e-6
LANES = 128
LOG2E = 1.4426950408889634
NEG_BIG = -1e30
VMEM_LIMIT = 56 * 1024 * 1024
NEAR = 2 * BLOCK


def _cparams(sem):
    return pltpu.CompilerParams(dimension_semantics=sem, vmem_limit_bytes=VMEM_LIMIT)


def _rms(x, g):
    return x * lax.rsqrt(jnp.mean(x * x, axis=-1, keepdims=True) + EPS) * g


def _group_rms(x, ngroups):
    grp = lax.broadcasted_iota(jnp.int32, x.shape, 1) // HEAD_DIM
    x2 = x * x
    inv = jnp.zeros_like(x)
    for h in range(ngroups):
        m = grp == h
        ss = jnp.sum(jnp.where(m, x2, 0.0), axis=-1, keepdims=True)
        inv = jnp.where(m, lax.rsqrt(ss * (1.0 / HEAD_DIM) + EPS), inv)
    return x * inv


def _rope(x, c, sn, sp, half):
    w = x.shape[-1]
    return x * c + pltpu.roll(x, w - half, 1) * sn + pltpu.roll(x, half, 1) * sp


def _bias_kernel(tab_ref, idx_ref, o_ref, *, head0, scale):
    h = pl.program_id(0)
    idx = idx_ref[...]
    acc = jnp.zeros(idx.shape, jnp.float32)
    for b in range(REL_BUCKETS):
        acc = jnp.where(idx == b, tab_ref[b, head0 + h], acc)
    o_ref[0] = jnp.where(idx < 0, NEG_BIG, acc * scale)


def _bias_tiles(rel_bias, idx, head0, nheads, scale):
    r, c = idx.shape
    return pl.pallas_call(
        functools.partial(_bias_kernel, head0=head0, scale=scale),
        grid=(nheads,),
        in_specs=[pl.BlockSpec(memory_space=pltpu.SMEM),
                  pl.BlockSpec((r, c), lambda h: (0, 0))],
        out_specs=pl.BlockSpec((1, r, c), lambda h: (h, 0, 0)),
        out_shape=jax.ShapeDtypeStruct((nheads, r, c), jnp.float32),
        compiler_params=_cparams(("arbitrary",)),
        name="rel_bias_tiles",
    )(rel_bias, idx)


def _bucket(rel):
    half = REL_BUCKETS // 2
    max_exact = half // 2
    n = jnp.abs(rel)
    nf = jnp.maximum(n, 1).astype(jnp.float32)
    large = max_exact + (jnp.log(nf / max_exact) / math.log(REL_MAX_DIST / max_exact)
                         * (half - max_exact)).astype(jnp.int32)
    large = jnp.minimum(large, half - 1)
    return jnp.where(rel > 0, half, 0) + jnp.where(n < max_exact, n, large)


def _pre_kernel(x_ref, g_ref, w_ref, cq_g_ref, ck_g_ref, ctab_ref, dq_g_ref, dkv_g_ref,
                wuq_ref, wukvk_ref, wukvv_ref, dtab_ref,
                aq_ref, ak_ref, avt_ref, bq_ref, bk_ref, bv_ref, cq_ref, ck_ref, cvt_ref,
                dq_ref, dk_ref, dvt_ref):
    bf = jnp.bfloat16
    h = _rms(x_ref[...], g_ref[...]).astype(bf)
    proj = lambda lo, hi: jnp.dot(h, w_ref[:, lo:hi], preferred_element_type=jnp.float32)
    pc = proj(1280, 1792)
    cc, csn, csp = ctab_ref[0], ctab_ref[1], ctab_ref[2]
    q = _group_rms(pc[:, 0:256], 4) * cq_g_ref[...]
    cq_ref[...] = (_rope(q, cc, csn, csp, 32) * 0.125).astype(bf)
    k = _group_rms(pc[:, 256:384], 2) * ck_g_ref[...]
    ck_ref[...] = _rope(k, cc[:, :128], csn[:, :128], csp[:, :128], 32).astype(bf)
    cvt_ref[...] = pc[:, 384:512].T.astype(bf)
    pd = proj(1792, 2304)
    dc, dsn, dsp = dtab_ref[0], dtab_ref[1], dtab_ref[2]
    cq = _rms(pd[:, 0:256], dq_g_ref[...]).astype(bf)
    q = jnp.dot(cq, wuq_ref[...], preferred_element_type=jnp.float32)
    rep = lambda t: jnp.concatenate([t] * 4, axis=1)
    dq_ref[...] = _rope(q, rep(dc), rep(dsn), rep(dsp), 16).astype(bf)
    ckv = _rms(pd[:, 256:384], dkv_g_ref[...]).astype(bf)
    kr = _rope(pd[:, 384:512], dc, dsn, dsp, 16)
    kn = jnp.dot(ckv, wukvk_ref[...], preferred_element_type=jnp.float32)
    dk_ref[...] = (kn + rep(kr)).astype(bf)
    dvt_ref[...] = jnp.dot(ckv, wukvv_ref[...], preferred_element_type=jnp.float32).T.astype(bf)
    pa = proj(0, 768)
    aq_ref[...] = pa[:, 0:256].astype(bf)
    ak_ref[...] = pa[:, 256:512].astype(bf)
    avt_ref[...] = pa[:, 512:768].T.astype(bf)
    pb = proj(768, 1280)
    bq_ref[...] = (pb[:, 0:256] * 0.125).astype(bf)
    bk_ref[...] = pb[:, 256:384].astype(bf)
    bv_ref[...] = pb[:, 384:512].astype(bf)


def _pre(x2, g, w, cq_g, ck_g, ctab, dq_g, dkv_g, wuq, wukvk, wukvv, dtab, seq, tm):
    n = x2.shape[0]
    nt = seq // tm
    row = lambda wd: pl.BlockSpec((tm, wd), lambda i: (i, 0))
    col = lambda wd: pl.BlockSpec((wd, tm), lambda i: (0, i))
    full = lambda a: pl.BlockSpec(a.shape, lambda i: (0,) * a.ndim)
    bf = jnp.bfloat16
    rows = lambda wd: (row(wd), jax.ShapeDtypeStruct((n, wd), bf))
    cols = lambda wd: (col(wd), jax.ShapeDtypeStruct((wd, n), bf))
    outs = [rows(256), rows(256), cols(256), rows(256), rows(128), rows(128),
            rows(256), rows(128), cols(128), rows(512), rows(512), cols(256)]
    return pl.pallas_call(
        _pre_kernel,
        grid=(n // tm,),
        in_specs=[row(D_MODEL), full(g), full(w), full(cq_g), full(ck_g),
                  pl.BlockSpec((3, tm, 256), lambda i: (0, i % nt, 0)),
                  full(dq_g), full(dkv_g), full(wuq), full(wukvk), full(wukvv),
                  pl.BlockSpec((3, tm, 128), lambda i: (0, i % nt, 0))],
        out_specs=[o[0] for o in outs],
        out_shape=[o[1] for o in outs],
        compiler_params=_cparams(("parallel",)),
        name="pre_attention",
    )(x2, g, w, cq_g, ck_g, ctab, dq_g, dkv_g, wuq, wukvk, wukvv, dtab)


_NT = (((1,), (1,)), ((), ()))


def _scores(chain):
    q, k, _, log2_scale, bias_t = chain
    st = lax.dot_general(k, q, _NT, preferred_element_type=jnp.float32) * log2_scale
    return st if bias_t is None else st + bias_t()


def _softmax_t(st):
    pt = jnp.exp2(st - jnp.max(st, axis=0, keepdims=True))
    return pt.astype(jnp.bfloat16), 1.0 / jnp.sum(pt, axis=0, keepdims=True)


def _attend_t(chains):
    n = len(chains)
    st, pr, out = [None] * n, [None] * n, [None] * n
    for t in range(n + 2):
        if t < n:
            st[t] = _scores(chains[t])
        if 0 <= t - 1 < n:
            pr[t - 1] = _softmax_t(st[t - 1])
            st[t - 1] = None
        if 0 <= t - 2 < n:
            pt, r = pr[t - 2]
            out[t - 2] = jnp.dot(chains[t - 2][2], pt, preferred_element_type=jnp.float32) * r
            pr[t - 2] = None
    return out


def _diff_kernel(q_ref, k_ref, vt_ref, gt_ref, dl_ref, sub_ref, o_ref, *, lam_init, seq, tq):
    i = pl.program_id(1)
    lane = lax.broadcasted_iota(jnp.int32, (tq, LANES), 1)
    upper_rows = lax.broadcasted_iota(jnp.int32, (LANES, tq), 0) >= HEAD_DIM
    dl = dl_ref[0]
    lam = (jnp.exp(jnp.sum(dl[0:1] * dl[1:2], axis=-1, keepdims=True))
           - jnp.exp(jnp.sum(dl[2:3] * dl[3:4], axis=-1, keepdims=True)) + lam_init)
    offs = [pl.multiple_of(jnp.clip(jb * BLOCK - i * tq, -NEAR, tq + BLOCK) + NEAR, BLOCK)
            for jb in range(seq // BLOCK)]
    chains = []
    for pr in range(2):
        pl_ = slice(pr * LANES, (pr + 1) * LANES)
        q, k, vt = q_ref[0, :, pl_], k_ref[0, :, pl_], vt_ref[pl_, :]
        for hh in range(2):
            bias_t = lambda h=2 * pr + hh: jnp.concatenate(
                [gt_ref[h, pl.ds(off, BLOCK), :] for off in offs], axis=0)
            for c in range(2):
                lo = hh * HEAD_DIM + c * DIFF_HALF
                qm = jnp.where((lane >= lo) & (lane < lo + DIFF_HALF), q, jnp.zeros_like(q))
                chains.append((qm, k, vt, DIFF_HALF ** -0.5 * LOG2E, bias_t))
    outs = _attend_t(chains)
    for pr in range(2):
        heads = []
        for hh in range(2):
            oh = outs[4 * pr + 2 * hh] - lam * outs[4 * pr + 2 * hh + 1]
            mh = upper_rows if hh else jnp.logical_not(upper_rows)
            ss = jnp.sum(jnp.where(mh, oh * oh, 0.0), axis=0, keepdims=True) * (1.0 / HEAD_DIM)
            heads.append(oh * lax.rsqrt(ss + EPS))
        out = jnp.where(upper_rows, heads[1], heads[0]).T * sub_ref[...] * (1.0 - lam_init)
        o_ref[0, :, pr * LANES:(pr + 1) * LANES] = out.astype(jnp.bfloat16)


def _diff_attention(q, k, vt, gt, dl, sub, lam_init, tq):
    b, seq, _ = q.shape
    return pl.pallas_call(
        functools.partial(_diff_kernel, lam_init=lam_init, seq=seq, tq=tq),
        grid=(b, seq // tq),
        in_specs=[pl.BlockSpec((1, tq, GROUP_WIDTH), lambda bi, i: (bi, i, 0)),
                  pl.BlockSpec((1, seq, GROUP_WIDTH), lambda bi, i: (bi, 0, 0)),
                  pl.BlockSpec((GROUP_WIDTH, seq), lambda bi, i: (0, bi)),
                  pl.BlockSpec(gt.shape, lambda bi, i: (0, 0, 0)),
                  pl.BlockSpec((1, 4, DIFF_HALF), lambda bi, i: (0, 0, 0)),
                  pl.BlockSpec((1, LANES), lambda bi, i: (0, 0))],
        out_specs=pl.BlockSpec((1, tq, GROUP_WIDTH), lambda bi, i: (bi, i, 0)),
        out_shape=jax.ShapeDtypeStruct((b, seq, GROUP_WIDTH), jnp.bfloat16),
        compiler_params=_cparams(("parallel", "arbitrary")),
        name="diff_attention",
    )(q, k, vt, gt, dl, sub)


def _gqa_kernel(q_ref, k_ref, vt_ref, o_ref, *, tq, cq):
    k, vt = k_ref[0], vt_ref[...]
    lower = lax.broadcasted_iota(jnp.int32, (cq, LANES), 1) < HEAD_DIM
    upper_rows = lax.broadcasted_iota(jnp.int32, (LANES, cq), 0) >= HEAD_DIM
    chains = []
    for r0 in range(0, tq, cq):
        for j in range(2):
            q = q_ref[0, r0:r0 + cq, j * LANES:(j + 1) * LANES]
            zero = jnp.zeros_like(q)
            chains.append((jnp.where(lower, q, zero), k, vt, LOG2E, None))
            chains.append((jnp.where(lower, zero, q), k, vt, LOG2E, None))
    outs = _attend_t(chains)
    for n, r0 in enumerate(range(0, tq, cq)):
        for j in range(2):
            lo, hi = outs[4 * n + 2 * j], outs[4 * n + 2 * j + 1]
            o_ref[0, r0:r0 + cq, j * LANES:(j + 1) * LANES] = jnp.where(
                upper_rows, hi, lo).T.astype(jnp.bfloat16)


def _gqa_attention(q, k, vt, tq, cq):
    b, seq, _ = q.shape
    return pl.pallas_call(
        functools.partial(_gqa_kernel, tq=tq, cq=cq),
        grid=(b, seq // tq),
        in_specs=[pl.BlockSpec((1, tq, GROUP_WIDTH), lambda bi, i: (bi, i, 0)),
                  pl.BlockSpec((1, seq, LANES), lambda bi, i: (bi, 0, 0)),
                  pl.BlockSpec((LANES, seq), lambda bi, i: (0, bi))],
        out_specs=pl.BlockSpec((1, tq, GROUP_WIDTH), lambda bi, i: (bi, i, 0)),
        out_shape=jax.ShapeDtypeStruct((b, seq, GROUP_WIDTH), jnp.bfloat16),
        compiler_params=_cparams(("parallel", "arbitrary")),
        name="axial_gqa_attention",
    )(q, k, vt)


def _mla_kernel(q_ref, k_ref, vt_ref, o_ref, *, tq, cq):
    upper_rows = lax.broadcasted_iota(jnp.int32, (LANES, cq), 0) >= HEAD_DIM
    log2_scale = (MLA_NOPE + MLA_ROPE) ** -0.5 * LOG2E
    chains = []
    for r0 in range(0, tq, cq):
        for hh in range(4):
            hl = slice(hh * LANES, (hh + 1) * LANES)
            vl = slice((hh // 2) * LANES, (hh // 2 + 1) * LANES)
            chains.append((q_ref[0, r0:r0 + cq, hl], k_ref[0, :, hl], vt_ref[vl, :], log2_scale, None))
    outs = _attend_t(chains)
    for n, r0 in enumerate(range(0, tq, cq)):
        for pr in range(2):
            lo, hi = outs[4 * n + 2 * pr], outs[4 * n + 2 * pr + 1]
            o_ref[0, r0:r0 + cq, pr * LANES:(pr + 1) * LANES] = jnp.where(
                upper_rows, hi, lo).T.astype(jnp.bfloat16)


def _mla_attention(q, k, vt, tq, cq):
    b, seq, _ = q.shape
    return pl.pallas_call(
        functools.partial(_mla_kernel, tq=tq, cq=cq),
        grid=(b, seq // tq),
        in_specs=[pl.BlockSpec((1, tq, 4 * LANES), lambda bi, i: (bi, i, 0)),
                  pl.BlockSpec((1, seq, 4 * LANES), lambda bi, i: (bi, 0, 0)),
                  pl.BlockSpec((GROUP_WIDTH, seq), lambda bi, i: (0, bi))],
        out_specs=pl.BlockSpec((1, tq, GROUP_WIDTH), lambda bi, i: (bi, i, 0)),
        out_shape=jax.ShapeDtypeStruct((b, seq, GROUP_WIDTH), jnp.bfloat16),
        compiler_params=_cparams(("parallel", "arbitrary")),
        name="latent_attention",
    )(q, k, vt)


def _softmax_parts(s, extra):
    m = jnp.maximum(jnp.max(s, axis=-1, keepdims=True), extra)
    p = jnp.exp(s - m)
    l = jnp.sum(p, axis=-1, keepdims=True) + jnp.exp(extra - m)
    return p.astype(jnp.bfloat16), 1.0 / l


def _win_kernel(sink_ref, q_ref, kp_ref, kc_ref, kn_ref, vp_ref, vc_ref, vn_ref, wb_ref, o_ref,
                *, nstep, nb):
    n = pl.program_id(1)
    kext = jnp.concatenate([kp_ref[0], kc_ref[0], kn_ref[0]], axis=0)
    vext = jnp.concatenate([vp_ref[0], vc_ref[0], vn_ref[0]], axis=0)
    lane = lax.broadcasted_iota(jnp.int32, (BLOCK, LANES), 1)
    col = lax.broadcasted_iota(jnp.int32, (BLOCK, 3 * BLOCK), 1)
    for blk in range(nb):
        rows = slice(blk * BLOCK, (blk + 1) * BLOCK)
        kcat = kext[blk * BLOCK:(blk + 3) * BLOCK]
        vcat = vext[blk * BLOCK:(blk + 3) * BLOCK]
        for j in range(2):
            q = q_ref[0, rows, j * LANES:(j + 1) * LANES]
            zero = jnp.zeros_like(q)
            qs = jnp.concatenate([jnp.where(lane < HEAD_DIM, q, zero),
                                  jnp.where(lane >= HEAD_DIM, q, zero)], axis=0)
            s = lax.dot_general(qs, kcat, _NT, preferred_element_type=jnp.float32)
            ps, rs = [], []
            for g in range(2):
                head = j + 2 * g
                sg = s[g * BLOCK:(g + 1) * BLOCK] + wb_ref[head]
                if blk == 0:
                    sg = jnp.where((col < BLOCK) & (n == 0), NEG_BIG, sg)
                if blk == nb - 1:
                    sg = jnp.where((col >= 2 * BLOCK) & (n == nstep - 1), NEG_BIG, sg)
                p, r = _softmax_parts(sg, sink_ref[head])
                ps.append(p)
                rs.append(r)
            o = jnp.dot(jnp.concatenate(ps, axis=0), vcat, preferred_element_type=jnp.float32)
            out = jnp.where(lane < HEAD_DIM, o[:BLOCK] * rs[0], o[BLOCK:] * rs[1])
            o_ref[0, rows, j * LANES:(j + 1) * LANES] = out.astype(jnp.bfloat16)


def _win_attention(q, k, v, sink, wbias, tw):
    b, seq, _ = q.shape
    nblk = seq // BLOCK
    nb = tw // BLOCK
    prev = lambda bi, n: (bi, jnp.maximum(n * nb - 1, 0), 0)
    cur = lambda bi, n: (bi, n, 0)
    nxt = lambda bi, n: (bi, jnp.minimum((n + 1) * nb, nblk - 1), 0)
    halo = lambda f: pl.BlockSpec((1, BLOCK, LANES), f)
    body = pl.BlockSpec((1, tw, LANES), cur)
    return pl.pallas_call(
        functools.partial(_win_kernel, nstep=seq // tw, nb=nb),
        grid=(b, seq // tw),
        in_specs=[pl.BlockSpec(memory_space=pltpu.SMEM),
                  pl.BlockSpec((1, tw, GROUP_WIDTH), cur),
                  halo(prev), body, halo(nxt), halo(prev), body, halo(nxt),
                  pl.BlockSpec((4, BLOCK, 3 * BLOCK), lambda bi, n: (0, 0, 0))],
        out_specs=pl.BlockSpec((1, tw, GROUP_WIDTH), cur),
        out_shape=jax.ShapeDtypeStruct((b, seq, GROUP_WIDTH), jnp.bfloat16),
        compiler_params=_cparams(("parallel", "arbitrary")),
        name="window_attention",
    )(sink, q, k, k, k, v, v, v, wbias)


def _outproj_kernel(x_ref, a_ref, b_ref, c_ref, d_ref, w_ref, g_ref, o_ref):
    y = jnp.dot(a_ref[...], w_ref[0], preferred_element_type=jnp.float32)
    y += jnp.dot(b_ref[...], w_ref[1], preferred_element_type=jnp.float32)
    y += jnp.dot(c_ref[...], w_ref[2], preferred_element_type=jnp.float32)
    y += jnp.dot(d_ref[...], w_ref[3], preferred_element_type=jnp.float32)
    o_ref[...] = x_ref[...] + _rms(y, g_ref[...])


def _outproj(x2, oa, ob, oc, od, w, g, tm):
    n = x2.shape[0]
    row = lambda wd: pl.BlockSpec((tm, wd), lambda i: (i, 0))
    return pl.pallas_call(
        _outproj_kernel,
        grid=(n // tm,),
        in_specs=[row(D_MODEL), row(256), row(256), row(256), row(256),
                  pl.BlockSpec(w.shape, lambda i: (0, 0, 0)),
                  pl.BlockSpec(g.shape, lambda i: (0, 0))],
        out_specs=row(D_MODEL),
        out_shape=jax.ShapeDtypeStruct((n, D_MODEL), jnp.float32),
        compiler_params=_cparams(("parallel",)),
        name="out_projection",
    )(x2, oa, ob, oc, od, w, g)


HALO = 8


def _ffn_kernel(x_ref, xp_ref, xn_ref, gpre_ref, wg_ref, wv_ref, cw_ref, cb_ref, wdn_ref, gpost_ref,
                o_ref, h_ref, *, tm, fc, tiles_per_seq):
    i = pl.program_id(0)
    bf = jnp.bfloat16
    g = gpre_ref[...]
    first = (i % tiles_per_seq) == 0
    last = (i % tiles_per_seq) == tiles_per_seq - 1
    h_ref[0:HALO, :] = jnp.where(first, 0.0, _rms(xp_ref[...], g)).astype(bf)
    h_ref[HALO:HALO + tm, :] = _rms(x_ref[...], g).astype(bf)
    h_ref[HALO + tm:, :] = jnp.where(last, 0.0, _rms(xn_ref[...], g)).astype(bf)
    y = jnp.zeros((tm, D_MODEL), jnp.float32)
    for c in range(D_FF // fc):
        cs = slice(c * fc, (c + 1) * fc)
        gate = jnp.dot(h_ref[...], wg_ref[:, cs], preferred_element_type=jnp.float32)
        val = jnp.dot(h_ref[HALO:HALO + tm, :], wv_ref[:, cs], preferred_element_type=jnp.float32)
        gate = (gate[HALO - 1:HALO - 1 + tm] * cw_ref[0:1, cs] + gate[HALO:HALO + tm] * cw_ref[1:2, cs]
                + gate[HALO + 1:HALO + 1 + tm] * cw_ref[2:3, cs] + cb_ref[:, cs])
        act = 0.5 * gate * (1.0 + lax.erf(gate * (2.0 ** -0.5))) * val
        y = y + jnp.dot(act.astype(bf), wdn_ref[cs, :], preferred_element_type=jnp.float32)
    o_ref[...] = x_ref[...] + _rms(y, gpost_ref[...])


def _ffn(x2, gpre, wg, wv, cw, cb, wdn, gpost, seq, tm, fc):
    n = x2.shape[0]
    tph = tm // HALO
    nh = n // HALO
    resident = lambda a: pl.BlockSpec(a.shape, lambda i: (0,) * a.ndim,
                                      pipeline_mode=pl.Buffered(1))
    return pl.pallas_call(
        functools.partial(_ffn_kernel, tm=tm, fc=fc, tiles_per_seq=seq // tm),
        grid=(n // tm,),
        in_specs=[pl.BlockSpec((tm, D_MODEL), lambda i: (i, 0)),
                  pl.BlockSpec((HALO, D_MODEL), lambda i: (jnp.maximum(i * tph - 1, 0), 0)),
                  pl.BlockSpec((HALO, D_MODEL), lambda i: (jnp.minimum((i + 1) * tph, nh - 1), 0)),
                  resident(gpre), resident(wg), resident(wv), resident(cw), resident(cb),
                  resident(wdn), resident(gpost)],
        out_specs=pl.BlockSpec((tm, D_MODEL), lambda i: (i, 0)),
        out_shape=jax.ShapeDtypeStruct((n, D_MODEL), jnp.float32),
        scratch_shapes=[pltpu.VMEM((tm + 2 * HALO, D_MODEL), jnp.bfloat16)],
        compiler_params=_cparams(("parallel",)),
        name="conv_glu",
    )(x2, x2, x2, gpre, wg, wv, cw, cb, wdn, gpost)


_GQA_HEAD_ORDER = (0, 2, 1, 3)


def _reorder_heads(w, order):
    lead = w.shape[:-1]
    nh = w.shape[-1] // HEAD_DIM
    w = w.reshape(lead + (nh, HEAD_DIM))
    return jnp.stack([w[..., o, :] for o in order], axis=-2).reshape(lead + (nh * HEAD_DIM,))


def _axial_perm(w):
    lead = w.shape[:-1]
    nh = w.shape[-1] // HEAD_DIM
    w = w.reshape(lead + (nh, 2, 2, 16))
    return jnp.swapaxes(w, -3, -2).reshape(lead + (nh * HEAD_DIM,))


def _prep_w_in(w_in):
    w_in = w_in.astype(jnp.bfloat16)
    bq = _reorder_heads(w_in[..., 768:1024], _GQA_HEAD_ORDER)
    cq = _axial_perm(_reorder_heads(w_in[..., 1280:1536], _GQA_HEAD_ORDER))
    ck = _axial_perm(w_in[..., 1536:1664])
    kr = w_in[..., 2176:2208]
    z = lambda wd: jnp.zeros(w_in.shape[:-1] + (wd,), w_in.dtype)
    return jnp.concatenate(
        [w_in[..., 0:768], bq, w_in[..., 1024:1280], cq, ck, w_in[..., 1664:1792],
         w_in[..., 1792:2176], z(MLA_NOPE), kr, z(LANES - MLA_NOPE - MLA_ROPE)], axis=-1)


def _rope_tables(cos, sin, lead_ones, trail_ones):
    s = cos.shape[0]
    one = lambda wd: jnp.ones((s, wd), jnp.float32)
    zero = lambda wd: jnp.zeros((s, wd), jnp.float32)
    c = jnp.concatenate([one(lead_ones), cos, cos, one(trail_ones)], axis=1)
    sn = jnp.concatenate([zero(lead_ones), -sin, zero(cos.shape[1]), zero(trail_ones)], axis=1)
    sp = jnp.concatenate([zero(lead_ones), zero(cos.shape[1]), sin, zero(trail_ones)], axis=1)
    return jnp.stack([c, sn, sp])


def _angles(pos, dim):
    inv = ROPE_THETA ** (-jnp.arange(0, dim, 2, dtype=jnp.float32) / dim)
    ang = pos.astype(jnp.float32)[:, None] * inv[None, :]
    return jnp.cos(ang), jnp.sin(ang)


def _forward(x, rel_bias, attn_pre_norm, w_in, diff_lambda, diff_subln, win_sink, ax_q_norm, ax_k_norm,
             mla_q_norm, mla_kv_norm, mla_w_uq, mla_w_ukv, w_out, attn_post_norm, ffn_pre_norm,
             ffn_w_up, ffn_conv_w, ffn_conv_b, ffn_w_down, ffn_post_norm, *, tq_a, tq, cq, tw, tm, tf, fc):
    bsz, seq, _ = x.shape
    depth = w_in.shape[0]
    n = bsz * seq
    bf = jnp.bfloat16

    pos = jnp.arange(seq, dtype=jnp.int32)
    rows = seq // GRID_W
    row_ids = jnp.repeat(jnp.arange(rows, dtype=jnp.int32), GRID_W)
    col_ids = jnp.tile(jnp.arange(GRID_W, dtype=jnp.int32), rows)
    rcos, rsin = _angles(row_ids, HEAD_DIM // 2)
    ccos, csin = _angles(col_ids, HEAD_DIM // 2)
    ctab = _rope_tables(jnp.concatenate([rcos, ccos], 1), jnp.concatenate([rsin, csin], 1), 0, 0)
    ctab = jnp.tile(ctab, (1, 1, 4))
    mcos, msin = _angles(pos, MLA_ROPE)
    dtab = _rope_tables(mcos, msin, MLA_NOPE, LANES - MLA_NOPE - MLA_ROPE)

    rel_a = (jnp.arange(tq_a + 2 * NEAR, dtype=jnp.int32)[:, None] - NEAR
             - jnp.arange(tq_a, dtype=jnp.int32)[None, :])
    gt = _bias_tiles(rel_bias, _bucket(rel_a), 0, 4, LOG2E)
    rel_b = jnp.arange(3 * BLOCK, dtype=jnp.int32)[None, :] - BLOCK - jnp.arange(BLOCK, dtype=jnp.int32)[:, None]
    idx_b = jnp.where(jnp.abs(rel_b) <= WINDOW, _bucket(rel_b), -1)
    wbias = _bias_tiles(rel_bias, idx_b, 4, 4, 1.0)

    w_in_p = _prep_w_in(w_in)
    cq_g = jnp.tile(_axial_perm(ax_q_norm), (1, 4))[:, None, :]
    ck_g = jnp.tile(_axial_perm(ax_k_norm), (1, 2))[:, None, :]
    wuq = jnp.pad(mla_w_uq.reshape(depth, MLA_Q_RANK, 4, MLA_NOPE + MLA_ROPE),
                  ((0, 0), (0, 0), (0, 0), (0, LANES - MLA_NOPE - MLA_ROPE))
                  ).reshape(depth, MLA_Q_RANK, 4 * LANES).astype(bf)
    wukv = mla_w_ukv.reshape(depth, MLA_KV_RANK, 4, 2, HEAD_DIM)
    wukvk = jnp.pad(wukv[:, :, :, 0], ((0, 0), (0, 0), (0, 0), (0, HEAD_DIM))
                    ).reshape(depth, MLA_KV_RANK, 4 * LANES).astype(bf)
    wukvv = wukv[:, :, :, 1].reshape(depth, MLA_KV_RANK, GROUP_WIDTH).astype(bf)
    wo = w_out.astype(bf).reshape(depth, 4, 4, HEAD_DIM, D_MODEL)
    slots = [(0, h) for h in range(4)] + [(1, h) for h in _GQA_HEAD_ORDER] \
        + [(2, h) for h in _GQA_HEAD_ORDER] + [(3, h) for h in range(4)]
    wo = jnp.stack([wo[:, m, h] for m, h in slots], axis=1).reshape(depth, 4, GROUP_WIDTH, D_MODEL)
    wg = ffn_w_up[:, :, :D_FF].astype(bf)
    wv = ffn_w_up[:, :, D_FF:].astype(bf)
    wdn = ffn_w_down.astype(bf)
    sub = jnp.tile(diff_subln, (1, 2))[:, None, :]

    x2 = x.reshape(n, D_MODEL)
    r3 = lambda a: a.reshape(bsz, seq, a.shape[-1])
    for l in range(depth):
        lam_init = 0.8 - 0.6 * math.exp(-0.3 * l)
        (aq, ak, avt, bq, bk, bv, cq_, ck, cvt, dq, dk, dvt) = _pre(
            x2, attn_pre_norm[l][None], w_in_p[l], cq_g[l], ck_g[l], ctab,
            mla_q_norm[l][None], mla_kv_norm[l][None], wuq[l], wukvk[l], wukvv[l], dtab, seq, tm)
        oa = _diff_attention(r3(aq), r3(ak), avt, gt, diff_lambda[l][None], sub[l], lam_init, tq_a)
        ob = _win_attention(r3(bq), r3(bk), r3(bv), win_sink[l], wbias, tw)
        oc = _gqa_attention(r3(cq_), r3(ck), cvt, tq, cq)
        od = _mla_attention(r3(dq), r3(dk), dvt, tq, cq)
        x2 = _outproj(x2, oa.reshape(n, -1), ob.reshape(n, -1), oc.reshape(n, -1),
                      od.reshape(n, -1), wo[l], attn_post_norm[l][None], tm)
        x2 = _ffn(x2, ffn_pre_norm[l][None], wg[l], wv[l], ffn_conv_w[l], ffn_conv_b[l][None],
                  wdn[l], ffn_post_norm[l][None], seq, tf, fc)
    return x2.reshape(bsz, seq, D_MODEL)


def kernel(x, rel_bias, attn_pre_norm, w_in, diff_lambda, diff_subln, win_sink, ax_q_norm, ax_k_norm,
           mla_q_norm, mla_kv_norm, mla_w_uq, mla_w_ukv, w_out, attn_post_norm, ffn_pre_norm,
           ffn_w_up, ffn_conv_w, ffn_conv_b, ffn_w_down, ffn_post_norm):
    return _forward(x, rel_bias, attn_pre_norm, w_in, diff_lambda, diff_subln, win_sink, ax_q_norm,
                    ax_k_norm, mla_q_norm, mla_kv_norm, mla_w_uq, mla_w_ukv, w_out, attn_post_norm,
                    ffn_pre_norm, ffn_w_up, ffn_conv_w, ffn_conv_b, ffn_w_down, ffn_post_norm,
                    tq_a=256, tq=512, cq=256, tw=512, tm=512, tf=512, fc=256)
```

```python
import functools
import math

import jax
import jax.numpy as jnp
from jax import lax
from jax.experimental import pallas as pl
from jax.experimental.pallas import tpu as pltpu

D_MODEL = 1024
HEAD_DIM = 64
GROUP_WIDTH = 256
BLOCK = 128
DIFF_HALF = 32
WINDOW = 128
GRID_W = 64
ROPE_THETA = 10000.0
MLA_Q_RANK = 256
MLA_KV_RANK = 128
MLA_NOPE = 64
MLA_ROPE = 32
REL_BUCKETS = 32
REL_MAX_DIST = 128
D_FF = 2816
EPS = 1e-6
LANES = 128
LOG2E = 1.4426950408889634
NEG_BIG = -1e30
VMEM_LIMIT = 56 * 1024 * 1024
NEAR = 2 * BLOCK


def _cparams(sem):
    return pltpu.CompilerParams(dimension_semantics=sem, vmem_limit_bytes=VMEM_LIMIT)


def _rms(x, g):
    return x * lax.rsqrt(jnp.mean(x * x, axis=-1, keepdims=True) + EPS) * g


def _group_rms(x, ngroups):
    grp = lax.broadcasted_iota(jnp.int32, x.shape, 1) // HEAD_DIM
    x2 = x * x
    inv = jnp.zeros_like(x)
    for h in range(ngroups):
        m = grp == h
        ss = jnp.sum(jnp.where(m, x2, 0.0), axis=-1, keepdims=True)
        inv = jnp.where(m, lax.rsqrt(ss * (1.0 / HEAD_DIM) + EPS), inv)
    return x * inv


def _rope(x, c, sn, sp, half):
    w = x.shape[-1]
    return x * c + pltpu.roll(x, w - half, 1) * sn + pltpu.roll(x, half, 1) * sp


def _bias_kernel(tab_ref, idx_ref, o_ref, *, head0, scale):
    h = pl.program_id(0)
    idx = idx_ref[...]
    acc = jnp.zeros(idx.shape, jnp.float32)
    for b in range(REL_BUCKETS):
        acc = jnp.where(idx == b, tab_ref[b, head0 + h], acc)
    o_ref[0] = jnp.where(idx < 0, NEG_BIG, acc * scale)


def _bias_tiles(rel_bias, idx, head0, nheads, scale):
    r, c = idx.shape
    return pl.pallas_call(
        functools.partial(_bias_kernel, head0=head0, scale=scale),
        grid=(nheads,),
        in_specs=[pl.BlockSpec(memory_space=pltpu.SMEM),
                  pl.BlockSpec((r, c), lambda h: (0, 0))],
        out_specs=pl.BlockSpec((1, r, c), lambda h: (h, 0, 0)),
        out_shape=jax.ShapeDtypeStruct((nheads, r, c), jnp.float32),
        compiler_params=_cparams(("arbitrary",)),
        name="rel_bias_tiles",
    )(rel_bias, idx)


def _bucket(rel):
    half = REL_BUCKETS // 2
    max_exact = half // 2
    n = jnp.abs(rel)
    nf = jnp.maximum(n, 1).astype(jnp.float32)
    large = max_exact + (jnp.log(nf / max_exact) / math.log(REL_MAX_DIST / max_exact)
                         * (half - max_exact)).astype(jnp.int32)
    large = jnp.minimum(large, half - 1)
    return jnp.where(rel > 0, half, 0) + jnp.where(n < max_exact, n, large)


def _pre_kernel(x_ref, g_ref, w_ref, cq_g_ref, ck_g_ref, ctab_ref, dq_g_ref, dkv_g_ref,
                wuq_ref, wukvk_ref, wukvv_ref, dtab_ref,
                aq_ref, ak_ref, avt_ref, bq_ref, bk_ref, bv_ref, cq_ref, ck_ref, cvt_ref,
                dq_ref, dk_ref, dvt_ref):
    bf = jnp.bfloat16
    h = _rms(x_ref[...], g_ref[...]).astype(bf)
    proj = lambda lo, hi: jnp.dot(h, w_ref[:, lo:hi], preferred_element_type=jnp.float32)
    pc = proj(1280, 1792)
    cc, csn, csp = ctab_ref[0], ctab_ref[1], ctab_ref[2]
    q = _group_rms(pc[:, 0:256], 4) * cq_g_ref[...]
    cq_ref[...] = (_rope(q, cc, csn, csp, 32) * (HEAD_DIM ** -0.5 * LOG2E)).astype(bf)
    k = _group_rms(pc[:, 256:384], 2) * ck_g_ref[...]
    ck_ref[...] = _rope(k, cc[:, :128], csn[:, :128], csp[:, :128], 32).astype(bf)
    cvt_ref[...] = pc[:, 384:512].T.astype(bf)
    pd = proj(1792, 2304)
    dc, dsn, dsp = dtab_ref[0], dtab_ref[1], dtab_ref[2]
    cq = _rms(pd[:, 0:256], dq_g_ref[...]).astype(bf)
    q = jnp.dot(cq, wuq_ref[...], preferred_element_type=jnp.float32)
    rep = lambda t: jnp.concatenate([t] * 4, axis=1)
    q = _rope(q, rep(dc), rep(dsn), rep(dsp), 16)
    dq_ref[...] = (q * ((MLA_NOPE + MLA_ROPE) ** -0.5 * LOG2E)).astype(bf)
    ckv = _rms(pd[:, 256:384], dkv_g_ref[...]).astype(bf)
    kr = _rope(pd[:, 384:512], dc, dsn, dsp, 16)
    kn = jnp.dot(ckv, wukvk_ref[...], preferred_element_type=jnp.float32)
    dk_ref[...] = (kn + rep(kr)).astype(bf)
    dvt_ref[...] = jnp.dot(ckv, wukvv_ref[...], preferred_element_type=jnp.float32).T.astype(bf)
    pa = proj(0, 768)
    aq_ref[...] = (pa[:, 0:256] * (DIFF_HALF ** -0.5 * LOG2E)).astype(bf)
    ak_ref[...] = pa[:, 256:512].astype(bf)
    avt_ref[...] = pa[:, 512:768].T.astype(bf)
    pb = proj(768, 1280)
    bq_ref[...] = (pb[:, 0:256] * 0.125).astype(bf)
    bk_ref[...] = pb[:, 256:384].astype(bf)
    bv_ref[...] = pb[:, 384:512].astype(bf)


def _pre(x2, g, w, cq_g, ck_g, ctab, dq_g, dkv_g, wuq, wukvk, wukvv, dtab, seq, tm):
    n = x2.shape[0]
    nt = seq // tm
    row = lambda wd: pl.BlockSpec((tm, wd), lambda i: (i, 0))
    col = lambda wd: pl.BlockSpec((wd, tm), lambda i: (0, i))
    full = lambda a: pl.BlockSpec(a.shape, lambda i: (0,) * a.ndim)
    bf = jnp.bfloat16
    rows = lambda wd: (row(wd), jax.ShapeDtypeStruct((n, wd), bf))
    cols = lambda wd: (col(wd), jax.ShapeDtypeStruct((wd, n), bf))
    outs = [rows(256), rows(256), cols(256), rows(256), rows(128), rows(128),
            rows(256), rows(128), cols(128), rows(512), rows(512), cols(256)]
    return pl.pallas_call(
        _pre_kernel,
        grid=(n // tm,),
        in_specs=[row(D_MODEL), full(g), full(w), full(cq_g), full(ck_g),
                  pl.BlockSpec((3, tm, 256), lambda i: (0, i % nt, 0)),
                  full(dq_g), full(dkv_g), full(wuq), full(wukvk), full(wukvv),
                  pl.BlockSpec((3, tm, 128), lambda i: (0, i % nt, 0))],
        out_specs=[o[0] for o in outs],
        out_shape=[o[1] for o in outs],
        compiler_params=_cparams(("parallel",)),
        name="pre_attention",
    )(x2, g, w, cq_g, ck_g, ctab, dq_g, dkv_g, wuq, wukvk, wukvv, dtab)


_NT = (((1,), (1,)), ((), ()))


def _scores(chain):
    q, k, _, bias_t = chain
    st = lax.dot_general(k, q, _NT, preferred_element_type=jnp.float32)
    return st if bias_t is None else st + bias_t()


def _softmax_t(st):
    pt = jnp.exp2(st - jnp.max(st, axis=0, keepdims=True))
    return pt.astype(jnp.bfloat16), 1.0 / jnp.sum(pt, axis=0, keepdims=True)


def _attend_t(chains):
    n = len(chains)
    st, pr, out = [None] * n, [None] * n, [None] * n
    for t in range(n + 2):
        if t < n:
            st[t] = _scores(chains[t])
        if 0 <= t - 2 < n:
            pt, r = pr[t - 2]
            out[t - 2] = jnp.dot(chains[t - 2][2], pt, preferred_element_type=jnp.float32) * r
            pr[t - 2] = None
        if 0 <= t - 1 < n:
            pr[t - 1] = _softmax_t(st[t - 1])
            st[t - 1] = None
    return out


def _diff_kernel(q_ref, k_ref, vt_ref, gt_ref, dl_ref, sub_ref, o_ref, *, lam_init, seq, tq):
    i = pl.program_id(1)
    lane = lax.broadcasted_iota(jnp.int32, (tq, LANES), 1)
    upper_rows = lax.broadcasted_iota(jnp.int32, (LANES, tq), 0) >= HEAD_DIM
    dl = dl_ref[0]
    lam = (jnp.exp(jnp.sum(dl[0:1] * dl[1:2], axis=-1, keepdims=True))
           - jnp.exp(jnp.sum(dl[2:3] * dl[3:4], axis=-1, keepdims=True)) + lam_init)
    offs = [pl.multiple_of(jnp.clip(jb * BLOCK - i * tq, -NEAR, tq + BLOCK) + NEAR, BLOCK)
            for jb in range(seq // BLOCK)]
    chains = []
    for pr in range(2):
        pl_ = slice(pr * LANES, (pr + 1) * LANES)
        q, k, vt = q_ref[0, :, pl_], k_ref[0, :, pl_], vt_ref[pl_, :]
        for hh in range(2):
            bias_t = lambda h=2 * pr + hh: jnp.concatenate(
                [gt_ref[h, pl.ds(off, BLOCK), :] for off in offs], axis=0)
            for c in range(2):
                lo = hh * HEAD_DIM + c * DIFF_HALF
                qm = jnp.where((lane >= lo) & (lane < lo + DIFF_HALF), q, jnp.zeros_like(q))
                chains.append((qm, k, vt, bias_t))
    outs = _attend_t(chains)
    for pr in range(2):
        heads = []
        for hh in range(2):
            oh = outs[4 * pr + 2 * hh] - lam * outs[4 * pr + 2 * hh + 1]
            mh = upper_rows if hh else jnp.logical_not(upper_rows)
            ss = jnp.sum(jnp.where(mh, oh * oh, 0.0), axis=0, keepdims=True) * (1.0 / HEAD_DIM)
            heads.append(oh * lax.rsqrt(ss + EPS))
        out = jnp.where(upper_rows, heads[1], heads[0]).T * sub_ref[...] * (1.0 - lam_init)
        o_ref[0, :, pr * LANES:(pr + 1) * LANES] = out.astype(jnp.bfloat16)


def _diff_attention(q, k, vt, gt, dl, sub, lam_init, tq):
    b, seq, _ = q.shape
    return pl.pallas_call(
        functools.partial(_diff_kernel, lam_init=lam_init, seq=seq, tq=tq),
        grid=(b, seq // tq),
        in_specs=[pl.BlockSpec((1, tq, GROUP_WIDTH), lambda bi, i: (bi, i, 0)),
                  pl.BlockSpec((1, seq, GROUP_WIDTH), lambda bi, i: (bi, 0, 0)),
                  pl.BlockSpec((GROUP_WIDTH, seq), lambda bi, i: (0, bi)),
                  pl.BlockSpec(gt.shape, lambda bi, i: (0, 0, 0)),
                  pl.BlockSpec((1, 4, DIFF_HALF), lambda bi, i: (0, 0, 0)),
                  pl.BlockSpec((1, LANES), lambda bi, i: (0, 0))],
        out_specs=pl.BlockSpec((1, tq, GROUP_WIDTH), lambda bi, i: (bi, i, 0)),
        out_shape=jax.ShapeDtypeStruct((b, seq, GROUP_WIDTH), jnp.bfloat16),
        compiler_params=_cparams(("parallel", "arbitrary")),
        name="diff_attention",
    )(q, k, vt, gt, dl, sub)


def _gqa_kernel(q_ref, k_ref, vt_ref, o_ref, *, tq, cq):
    k, vt = k_ref[0], vt_ref[...]
    lower = lax.broadcasted_iota(jnp.int32, (cq, LANES), 1) < HEAD_DIM
    upper_rows = lax.broadcasted_iota(jnp.int32, (LANES, cq), 0) >= HEAD_DIM
    chains = []
    for r0 in range(0, tq, cq):
        for j in range(2):
            q = q_ref[0, r0:r0 + cq, j * LANES:(j + 1) * LANES]
            zero = jnp.zeros_like(q)
            chains.append((jnp.where(lower, q, zero), k, vt, None))
            chains.append((jnp.where(lower, zero, q), k, vt, None))
    outs = _attend_t(chains)
    for n, r0 in enumerate(range(0, tq, cq)):
        for j in range(2):
            lo, hi = outs[4 * n + 2 * j], outs[4 * n + 2 * j + 1]
            o_ref[0, r0:r0 + cq, j * LANES:(j + 1) * LANES] = jnp.where(
                upper_rows, hi, lo).T.astype(jnp.bfloat16)


def _gqa_attention(q, k, vt, tq, cq):
    b, seq, _ = q.shape
    return pl.pallas_call(
        functools.partial(_gqa_kernel, tq=tq, cq=cq),
        grid=(b, seq // tq),
        in_specs=[pl.BlockSpec((1, tq, GROUP_WIDTH), lambda bi, i: (bi, i, 0)),
                  pl.BlockSpec((1, seq, LANES), lambda bi, i: (bi, 0, 0)),
                  pl.BlockSpec((LANES, seq), lambda bi, i: (0, bi))],
        out_specs=pl.BlockSpec((1, tq, GROUP_WIDTH), lambda bi, i: (bi, i, 0)),
        out_shape=jax.ShapeDtypeStruct((b, seq, GROUP_WIDTH), jnp.bfloat16),
        compiler_params=_cparams(("parallel", "arbitrary")),
        name="axial_gqa_attention",
    )(q, k, vt)


def _mla_kernel(q_ref, k_ref, vt_ref, o_ref, *, tq, cq):
    upper_rows = lax.broadcasted_iota(jnp.int32, (LANES, cq), 0) >= HEAD_DIM
    chains = []
    for r0 in range(0, tq, cq):
        for hh in range(4):
            hl = slice(hh * LANES, (hh + 1) * LANES)
            vl = slice((hh // 2) * LANES, (hh // 2 + 1) * LANES)
            chains.append((q_ref[0, r0:r0 + cq, hl], k_ref[0, :, hl], vt_ref[vl, :], None))
    outs = _attend_t(chains)
    for n, r0 in enumerate(range(0, tq, cq)):
        for pr in range(2):
            lo, hi = outs[4 * n + 2 * pr], outs[4 * n + 2 * pr + 1]
            o_ref[0, r0:r0 + cq, pr * LANES:(pr + 1) * LANES] = jnp.where(
                upper_rows, hi, lo).T.astype(jnp.bfloat16)


def _mla_attention(q, k, vt, tq, cq):
    b, seq, _ = q.shape
    return pl.pallas_call(
        functools.partial(_mla_kernel, tq=tq, cq=cq),
        grid=(b, seq // tq),
        in_specs=[pl.BlockSpec((1, tq, 4 * LANES), lambda bi, i: (bi, i, 0)),
                  pl.BlockSpec((1, seq, 4 * LANES), lambda bi, i: (bi, 0, 0)),
                  pl.BlockSpec((GROUP_WIDTH, seq), lambda bi, i: (0, bi))],
        out_specs=pl.BlockSpec((1, tq, GROUP_WIDTH), lambda bi, i: (bi, i, 0)),
        out_shape=jax.ShapeDtypeStruct((b, seq, GROUP_WIDTH), jnp.bfloat16),
        compiler_params=_cparams(("parallel", "arbitrary")),
        name="latent_attention",
    )(q, k, vt)


def _softmax_parts(s, extra):
    m = jnp.maximum(jnp.max(s, axis=-1, keepdims=True), extra)
    p = jnp.exp(s - m)
    l = jnp.sum(p, axis=-1, keepdims=True) + jnp.exp(extra - m)
    return p.astype(jnp.bfloat16), 1.0 / l


def _win_kernel(sink_ref, q_ref, kp_ref, kc_ref, kn_ref, vp_ref, vc_ref, vn_ref, wb_ref, o_ref,
                *, nstep, nb):
    n = pl.program_id(1)
    kext = jnp.concatenate([kp_ref[0], kc_ref[0], kn_ref[0]], axis=0)
    vext = jnp.concatenate([vp_ref[0], vc_ref[0], vn_ref[0]], axis=0)
    lane = lax.broadcasted_iota(jnp.int32, (BLOCK, LANES), 1)
    col = lax.broadcasted_iota(jnp.int32, (BLOCK, 3 * BLOCK), 1)
    for blk in range(nb):
        rows = slice(blk * BLOCK, (blk + 1) * BLOCK)
        kcat = kext[blk * BLOCK:(blk + 3) * BLOCK]
        vcat = vext[blk * BLOCK:(blk + 3) * BLOCK]
        for j in range(2):
            q = q_ref[0, rows, j * LANES:(j + 1) * LANES]
            zero = jnp.zeros_like(q)
            qs = jnp.concatenate([jnp.where(lane < HEAD_DIM, q, zero),
                                  jnp.where(lane >= HEAD_DIM, q, zero)], axis=0)
            s = lax.dot_general(qs, kcat, _NT, preferred_element_type=jnp.float32)
            ps, rs = [], []
            for g in range(2):
                head = j + 2 * g
                sg = s[g * BLOCK:(g + 1) * BLOCK] + wb_ref[head]
                if blk == 0:
                    sg = jnp.where((col < BLOCK) & (n == 0), NEG_BIG, sg)
                if blk == nb - 1:
                    sg = jnp.where((col >= 2 * BLOCK) & (n == nstep - 1), NEG_BIG, sg)
                p, r = _softmax_parts(sg, sink_ref[head])
                ps.append(p)
                rs.append(r)
            o = jnp.dot(jnp.concatenate(ps, axis=0), vcat, preferred_element_type=jnp.float32)
            out = jnp.where(lane < HEAD_DIM, o[:BLOCK] * rs[0], o[BLOCK:] * rs[1])
            o_ref[0, rows, j * LANES:(j + 1) * LANES] = out.astype(jnp.bfloat16)


def _win_attention(q, k, v, sink, wbias, tw):
    b, seq, _ = q.shape
    nblk = seq // BLOCK
    nb = tw // BLOCK
    prev = lambda bi, n: (bi, jnp.maximum(n * nb - 1, 0), 0)
    cur = lambda bi, n: (bi, n, 0)
    nxt = lambda bi, n: (bi, jnp.minimum((n + 1) * nb, nblk - 1), 0)
    halo = lambda f: pl.BlockSpec((1, BLOCK, LANES), f)
    body = pl.BlockSpec((1, tw, LANES), cur)
    return pl.pallas_call(
        functools.partial(_win_kernel, nstep=seq // tw, nb=nb),
        grid=(b, seq // tw),
        in_specs=[pl.BlockSpec(memory_space=pltpu.SMEM),
                  pl.BlockSpec((1, tw, GROUP_WIDTH), cur),
                  halo(prev), body, halo(nxt), halo(prev), body, halo(nxt),
                  pl.BlockSpec((4, BLOCK, 3 * BLOCK), lambda bi, n: (0, 0, 0))],
        out_specs=pl.BlockSpec((1, tw, GROUP_WIDTH), cur),
        out_shape=jax.ShapeDtypeStruct((b, seq, GROUP_WIDTH), jnp.bfloat16),
        compiler_params=_cparams(("parallel", "arbitrary")),
        name="window_attention",
    )(sink, q, k, k, k, v, v, v, wbias)


def _outproj_kernel(x_ref, a_ref, b_ref, c_ref, d_ref, w_ref, g_ref, o_ref):
    y = jnp.dot(a_ref[...], w_ref[0], preferred_element_type=jnp.float32)
    y += jnp.dot(b_ref[...], w_ref[1], preferred_element_type=jnp.float32)
    y += jnp.dot(c_ref[...], w_ref[2], preferred_element_type=jnp.float32)
    y += jnp.dot(d_ref[...], w_ref[3], preferred_element_type=jnp.float32)
    o_ref[...] = x_ref[...] + _rms(y, g_ref[...])


def _outproj(x2, oa, ob, oc, od, w, g, tm):
    n = x2.shape[0]
    row = lambda wd: pl.BlockSpec((tm, wd), lambda i: (i, 0))
    return pl.pallas_call(
        _outproj_kernel,
        grid=(n // tm,),
        in_specs=[row(D_MODEL), row(256), row(256), row(256), row(256),
                  pl.BlockSpec(w.shape, lambda i: (0, 0, 0)),
                  pl.BlockSpec(g.shape, lambda i: (0, 0))],
        out_specs=row(D_MODEL),
        out_shape=jax.ShapeDtypeStruct((n, D_MODEL), jnp.float32),
        compiler_params=_cparams(("parallel",)),
        name="out_projection",
    )(x2, oa, ob, oc, od, w, g)


HALO = 8
DOWN_GROUP_ENDS = (5, 10, 11)


def _ffn_kernel(x_ref, xp_ref, xn_ref, gpre_ref, wg_ref, wv_ref, cw_ref, cb_ref, wdn_ref, gpost_ref,
                o_ref, h_ref, *, tm, fc, tiles_per_seq):
    i = pl.program_id(0)
    bf = jnp.bfloat16
    g = gpre_ref[...]
    first = (i % tiles_per_seq) == 0
    last = (i % tiles_per_seq) == tiles_per_seq - 1
    h_ref[0:HALO, :] = jnp.where(first, 0.0, _rms(xp_ref[...], g)).astype(bf)
    h_ref[HALO:HALO + tm, :] = _rms(x_ref[...], g).astype(bf)
    h_ref[HALO + tm:, :] = jnp.where(last, 0.0, _rms(xn_ref[...], g)).astype(bf)
    nf = D_FF // fc
    chunk = lambda c: slice(c * fc, (c + 1) * fc)

    def up(c):
        gate = jnp.dot(h_ref[...], wg_ref[:, chunk(c)], preferred_element_type=jnp.float32)
        val = jnp.dot(h_ref[HALO:HALO + tm, :], wv_ref[:, chunk(c)], preferred_element_type=jnp.float32)
        return gate, val

    y = None
    acts, group_start = [], 0
    nxt = up(0)
    for c in range(nf):
        cs = chunk(c)
        gate, val = nxt
        if c + 1 < nf:
            nxt = up(c + 1)
        gate = (gate[HALO - 1:HALO - 1 + tm] * cw_ref[0:1, cs] + gate[HALO:HALO + tm] * cw_ref[1:2, cs]
                + gate[HALO + 1:HALO + 1 + tm] * cw_ref[2:3, cs] + cb_ref[:, cs])
        acts.append((0.5 * gate * (1.0 + lax.erf(gate * (2.0 ** -0.5))) * val).astype(bf))
        if c + 1 in DOWN_GROUP_ENDS:
            rows = slice(group_start * fc, (c + 1) * fc)
            part = jnp.dot(jnp.concatenate(acts, axis=1), wdn_ref[rows, :],
                           preferred_element_type=jnp.float32)
            y = part if y is None else y + part
            acts, group_start = [], c + 1
    o_ref[...] = x_ref[...] + _rms(y, gpost_ref[...])


def _ffn(x2, gpre, wg, wv, cw, cb, wdn, gpost, seq, tm, fc):
    n = x2.shape[0]
    tph = tm // HALO
    nh = n // HALO
    resident = lambda a: pl.BlockSpec(a.shape, lambda i: (0,) * a.ndim,
                                      pipeline_mode=pl.Buffered(1))
    return pl.pallas_call(
        functools.partial(_ffn_kernel, tm=tm, fc=fc, tiles_per_seq=seq // tm),
        grid=(n // tm,),
        in_specs=[pl.BlockSpec((tm, D_MODEL), lambda i: (i, 0)),
                  pl.BlockSpec((HALO, D_MODEL), lambda i: (jnp.maximum(i * tph - 1, 0), 0)),
                  pl.BlockSpec((HALO, D_MODEL), lambda i: (jnp.minimum((i + 1) * tph, nh - 1), 0)),
                  resident(gpre), resident(wg), resident(wv), resident(cw), resident(cb),
                  resident(wdn), resident(gpost)],
        out_specs=pl.BlockSpec((tm, D_MODEL), lambda i: (i, 0)),
        out_shape=jax.ShapeDtypeStruct((n, D_MODEL), jnp.float32),
        scratch_shapes=[pltpu.VMEM((tm + 2 * HALO, D_MODEL), jnp.bfloat16)],
        compiler_params=_cparams(("parallel",)),
        name="conv_glu",
    )(x2, x2, x2, gpre, wg, wv, cw, cb, wdn, gpost)


_GQA_HEAD_ORDER = (0, 2, 1, 3)


def _reorder_heads(w, order):
    lead = w.shape[:-1]
    nh = w.shape[-1] // HEAD_DIM
    w = w.reshape(lead + (nh, HEAD_DIM))
    return jnp.stack([w[..., o, :] for o in order], axis=-2).reshape(lead + (nh * HEAD_DIM,))


def _axial_perm(w):
    lead = w.shape[:-1]
    nh = w.shape[-1] // HEAD_DIM
    w = w.reshape(lead + (nh, 2, 2, 16))
    return jnp.swapaxes(w, -3, -2).reshape(lead + (nh * HEAD_DIM,))


def _prep_w_in(w_in):
    w_in = w_in.astype(jnp.bfloat16)
    bq = _reorder_heads(w_in[..., 768:1024], _GQA_HEAD_ORDER)
    cq = _axial_perm(_reorder_heads(w_in[..., 1280:1536], _GQA_HEAD_ORDER))
    ck = _axial_perm(w_in[..., 1536:1664])
    kr = w_in[..., 2176:2208]
    z = lambda wd: jnp.zeros(w_in.shape[:-1] + (wd,), w_in.dtype)
    return jnp.concatenate(
        [w_in[..., 0:768], bq, w_in[..., 1024:1280], cq, ck, w_in[..., 1664:1792],
         w_in[..., 1792:2176], z(MLA_NOPE), kr, z(LANES - MLA_NOPE - MLA_ROPE)], axis=-1)


def _rope_tables(cos, sin, lead_ones, trail_ones):
    s = cos.shape[0]
    one = lambda wd: jnp.ones((s, wd), jnp.float32)
    zero = lambda wd: jnp.zeros((s, wd), jnp.float32)
    c = jnp.concatenate([one(lead_ones), cos, cos, one(trail_ones)], axis=1)
    sn = jnp.concatenate([zero(lead_ones), -sin, zero(cos.shape[1]), zero(trail_ones)], axis=1)
    sp = jnp.concatenate([zero(lead_ones), zero(cos.shape[1]), sin, zero(trail_ones)], axis=1)
    return jnp.stack([c, sn, sp])


def _angles(pos, dim):
    inv = ROPE_THETA ** (-jnp.arange(0, dim, 2, dtype=jnp.float32) / dim)
    ang = pos.astype(jnp.float32)[:, None] * inv[None, :]
    return jnp.cos(ang), jnp.sin(ang)


def _forward(x, rel_bias, attn_pre_norm, w_in, diff_lambda, diff_subln, win_sink, ax_q_norm, ax_k_norm,
             mla_q_norm, mla_kv_norm, mla_w_uq, mla_w_ukv, w_out, attn_post_norm, ffn_pre_norm,
             ffn_w_up, ffn_conv_w, ffn_conv_b, ffn_w_down, ffn_post_norm, *, tq_a, tq, cq, tw, tm, tf, fc):
    bsz, seq, _ = x.shape
    depth = w_in.shape[0]
    n = bsz * seq
    bf = jnp.bfloat16

    pos = jnp.arange(seq, dtype=jnp.int32)
    rows = seq // GRID_W
    row_ids = jnp.repeat(jnp.arange(rows, dtype=jnp.int32), GRID_W)
    col_ids = jnp.tile(jnp.arange(GRID_W, dtype=jnp.int32), rows)
    rcos, rsin = _angles(row_ids, HEAD_DIM // 2)
    ccos, csin = _angles(col_ids, HEAD_DIM // 2)
    ctab = _rope_tables(jnp.concatenate([rcos, ccos], 1), jnp.concatenate([rsin, csin], 1), 0, 0)
    ctab = jnp.tile(ctab, (1, 1, 4))
    mcos, msin = _angles(pos, MLA_ROPE)
    dtab = _rope_tables(mcos, msin, MLA_NOPE, LANES - MLA_NOPE - MLA_ROPE)

    rel_a = (jnp.arange(tq_a + 2 * NEAR, dtype=jnp.int32)[:, None] - NEAR
             - jnp.arange(tq_a, dtype=jnp.int32)[None, :])
    gt = _bias_tiles(rel_bias, _bucket(rel_a), 0, 4, LOG2E)
    rel_b = jnp.arange(3 * BLOCK, dtype=jnp.int32)[None, :] - BLOCK - jnp.arange(BLOCK, dtype=jnp.int32)[:, None]
    idx_b = jnp.where(jnp.abs(rel_b) <= WINDOW, _bucket(rel_b), -1)
    wbias = _bias_tiles(rel_bias, idx_b, 4, 4, 1.0)

    w_in_p = _prep_w_in(w_in)
    cq_g = jnp.tile(_axial_perm(ax_q_norm), (1, 4))[:, None, :]
    ck_g = jnp.tile(_axial_perm(ax_k_norm), (1, 2))[:, None, :]
    wuq = jnp.pad(mla_w_uq.reshape(depth, MLA_Q_RANK, 4, MLA_NOPE + MLA_ROPE),
                  ((0, 0), (0, 0), (0, 0), (0, LANES - MLA_NOPE - MLA_ROPE))
                  ).reshape(depth, MLA_Q_RANK, 4 * LANES).astype(bf)
    wukv = mla_w_ukv.reshape(depth, MLA_KV_RANK, 4, 2, HEAD_DIM)
    wukvk = jnp.pad(wukv[:, :, :, 0], ((0, 0), (0, 0), (0, 0), (0, HEAD_DIM))
                    ).reshape(depth, MLA_KV_RANK, 4 * LANES).astype(bf)
    wukvv = wukv[:, :, :, 1].reshape(depth, MLA_KV_RANK, GROUP_WIDTH).astype(bf)
    wo = w_out.astype(bf).reshape(depth, 4, 4, HEAD_DIM, D_MODEL)
    slots = [(0, h) for h in range(4)] + [(1, h) for h in _GQA_HEAD_ORDER] \
        + [(2, h) for h in _GQA_HEAD_ORDER] + [(3, h) for h in range(4)]
    wo = jnp.stack([wo[:, m, h] for m, h in slots], axis=1).reshape(depth, 4, GROUP_WIDTH, D_MODEL)
    wg = ffn_w_up[:, :, :D_FF].astype(bf)
    wv = ffn_w_up[:, :, D_FF:].astype(bf)
    wdn = ffn_w_down.astype(bf)
    sub = jnp.tile(diff_subln, (1, 2))[:, None, :]

    x2 = x.reshape(n, D_MODEL)
    r3 = lambda a: a.reshape(bsz, seq, a.shape[-1])
    for l in range(depth):
        lam_init = 0.8 - 0.6 * math.exp(-0.3 * l)
        (aq, ak, avt, bq, bk, bv, cq_, ck, cvt, dq, dk, dvt) = _pre(
            x2, attn_pre_norm[l][None], w_in_p[l], cq_g[l], ck_g[l], ctab,
            mla_q_norm[l][None], mla_kv_norm[l][None], wuq[l], wukvk[l], wukvv[l], dtab, seq, tm)
        oa = _diff_attention(r3(aq), r3(ak), avt, gt, diff_lambda[l][None], sub[l], lam_init, tq_a)
        ob = _win_attention(r3(bq), r3(bk), r3(bv), win_sink[l], wbias, tw)
        oc = _gqa_attention(r3(cq_), r3(ck), cvt, tq, cq)
        od = _mla_attention(r3(dq), r3(dk), dvt, tq, cq)
        x2 = _outproj(x2, oa.reshape(n, -1), ob.reshape(n, -1), oc.reshape(n, -1),
                      od.reshape(n, -1), wo[l], attn_post_norm[l][None], tm)
        x2 = _ffn(x2, ffn_pre_norm[l][None], wg[l], wv[l], ffn_conv_w[l], ffn_conv_b[l][None],
                  wdn[l], ffn_post_norm[l][None], seq, tf, fc)
    return x2.reshape(bsz, seq, D_MODEL)


def kernel(x, rel_bias, attn_pre_norm, w_in, diff_lambda, diff_subln, win_sink, ax_q_norm, ax_k_norm,
           mla_q_norm, mla_kv_norm, mla_w_uq, mla_w_ukv, w_out, attn_post_norm, ffn_pre_norm,
           ffn_w_up, ffn_conv_w, ffn_conv_b, ffn_w_down, ffn_post_norm):
    return _forward(x, rel_bias, attn_pre_norm, w_in, diff_lambda, diff_subln, win_sink, ax_q_norm,
                    ax_k_norm, mla_q_norm, mla_kv_norm, mla_w_uq, mla_w_ukv, w_out, attn_post_norm,
                    ffn_pre_norm, ffn_w_up, ffn_conv_w, ffn_conv_b, ffn_w_down, ffn_post_norm,
                    tq_a=256, tq=512, cq=256, tw=512, tm=512, tf=512, fc=256)
```

```python
import functools
import math

import jax
import jax.numpy as jnp
from jax import lax
from jax.experimental import pallas as pl
from jax.experimental.pallas import tpu as pltpu

D_MODEL = 1024
HEAD_DIM = 64
GROUP_WIDTH = 256
BLOCK = 128
DIFF_HALF = 32
WINDOW = 128
GRID_W = 64
ROPE_THETA = 10000.0
MLA_Q_RANK = 256
MLA_KV_RANK = 128
MLA_NOPE = 64
MLA_ROPE = 32
REL_BUCKETS = 32
REL_MAX_DIST = 128
D_FF = 2816
EPS = 1e-6
LANES = 128
LOG2E = 1.4426950408889634
NEG_BIG = -1e30
VMEM_LIMIT = 56 * 1024 * 1024
NEAR = 2 * BLOCK


def _cparams(sem):
    return pltpu.CompilerParams(dimension_semantics=sem, vmem_limit_bytes=VMEM_LIMIT)


def _rms(x, g):
    return x * lax.rsqrt(jnp.mean(x * x, axis=-1, keepdims=True) + EPS) * g


def _group_rms(x, ngroups):
    grp = lax.broadcasted_iota(jnp.int32, x.shape, 1) // HEAD_DIM
    x2 = x * x
    inv = jnp.zeros_like(x)
    for h in range(ngroups):
        m = grp == h
        ss = jnp.sum(jnp.where(m, x2, 0.0), axis=-1, keepdims=True)
        inv = jnp.where(m, lax.rsqrt(ss * (1.0 / HEAD_DIM) + EPS), inv)
    return x * inv


def _rope(x, c, sn, sp, half):
    w = x.shape[-1]
    return x * c + pltpu.roll(x, w - half, 1) * sn + pltpu.roll(x, half, 1) * sp


def _bias_kernel(tab_ref, idx_ref, o_ref, *, head0, scale):
    h = pl.program_id(0)
    idx = idx_ref[...]
    acc = jnp.zeros(idx.shape, jnp.float32)
    for b in range(REL_BUCKETS):
        acc = jnp.where(idx == b, tab_ref[b, head0 + h], acc)
    o_ref[0] = jnp.where(idx < 0, NEG_BIG, acc * scale)


def _bias_tiles(rel_bias, idx, head0, nheads, scale):
    r, c = idx.shape
    return pl.pallas_call(
        functools.partial(_bias_kernel, head0=head0, scale=scale),
        grid=(nheads,),
        in_specs=[pl.BlockSpec(memory_space=pltpu.SMEM),
                  pl.BlockSpec((r, c), lambda h: (0, 0))],
        out_specs=pl.BlockSpec((1, r, c), lambda h: (h, 0, 0)),
        out_shape=jax.ShapeDtypeStruct((nheads, r, c), jnp.float32),
        compiler_params=_cparams(("arbitrary",)),
        name="rel_bias_tiles",
    )(rel_bias, idx)


def _bucket(rel):
    half = REL_BUCKETS // 2
    max_exact = half // 2
    n = jnp.abs(rel)
    nf = jnp.maximum(n, 1).astype(jnp.float32)
    large = max_exact + (jnp.log(nf / max_exact) / math.log(REL_MAX_DIST / max_exact)
                         * (half - max_exact)).astype(jnp.int32)
    large = jnp.minimum(large, half - 1)
    return jnp.where(rel > 0, half, 0) + jnp.where(n < max_exact, n, large)


def _pre_kernel(x_ref, g_ref, w_ref, cq_g_ref, ck_g_ref, ctab_ref, dq_g_ref, dkv_g_ref,
                wuq_ref, wukvk_ref, wukvv_ref, dtab_ref,
                aq_ref, ak_ref, avt_ref, bq_ref, bk_ref, bv_ref, cq_ref, ck_ref, cvt_ref,
                dq_ref, dk_ref, dvt_ref):
    bf = jnp.bfloat16
    tm = x_ref.shape[0]
    rep = lambda t: jnp.concatenate([t] * 4, axis=1)

    def epi_c(r, p):
        cc, csn, csp = ctab_ref[0, r], ctab_ref[1, r], ctab_ref[2, r]
        q = _group_rms(p[:, 0:256], 4) * cq_g_ref[...]
        cq_ref[r, :] = (_rope(q, cc, csn, csp, 32) * (HEAD_DIM ** -0.5 * LOG2E)).astype(bf)
        k = _group_rms(p[:, 256:384], 2) * ck_g_ref[...]
        ck_ref[r, :] = _rope(k, cc[:, :128], csn[:, :128], csp[:, :128], 32).astype(bf)
        cvt_ref[:, r] = p[:, 384:512].T.astype(bf)

    def epi_d(r, p):
        dc, dsn, dsp = dtab_ref[0, r], dtab_ref[1, r], dtab_ref[2, r]
        cq = _rms(p[:, 0:256], dq_g_ref[...]).astype(bf)
        q = jnp.dot(cq, wuq_ref[...], preferred_element_type=jnp.float32)
        q = _rope(q, rep(dc), rep(dsn), rep(dsp), 16)
        dq_ref[r, :] = (q * ((MLA_NOPE + MLA_ROPE) ** -0.5 * LOG2E)).astype(bf)
        ckv = _rms(p[:, 256:384], dkv_g_ref[...]).astype(bf)
        kr = _rope(p[:, 384:512], dc, dsn, dsp, 16)
        kn = jnp.dot(ckv, wukvk_ref[...], preferred_element_type=jnp.float32)
        dk_ref[r, :] = (kn + rep(kr)).astype(bf)
        dvt_ref[:, r] = jnp.dot(ckv, wukvv_ref[...], preferred_element_type=jnp.float32).T.astype(bf)

    def epi_a(r, p):
        aq_ref[r, :] = (p[:, 0:256] * (DIFF_HALF ** -0.5 * LOG2E)).astype(bf)
        ak_ref[r, :] = p[:, 256:512].astype(bf)
        avt_ref[:, r] = p[:, 512:768].T.astype(bf)

    def epi_b(r, p):
        bq_ref[r, :] = (p[:, 0:256] * 0.125).astype(bf)
        bk_ref[r, :] = p[:, 256:384].astype(bf)
        bv_ref[r, :] = p[:, 384:512].astype(bf)

    mixers = [((1280, 1792), epi_c), ((1792, 2304), epi_d), ((0, 768), epi_a), ((768, 1280), epi_b)]
    halves = [slice(0, tm // 2), slice(tm // 2, tm)]
    pending = None
    for r in halves:
        h = _rms(x_ref[r, :], g_ref[...]).astype(bf)
        for (lo, hi), epilogue in mixers:
            p = jnp.dot(h, w_ref[:, lo:hi], preferred_element_type=jnp.float32)
            if pending is not None:
                pending[0](pending[1], pending[2])
            pending = (epilogue, r, p)
    pending[0](pending[1], pending[2])


def _pre(x2, g, w, cq_g, ck_g, ctab, dq_g, dkv_g, wuq, wukvk, wukvv, dtab, seq, tm):
    n = x2.shape[0]
    nt = seq // tm
    row = lambda wd: pl.BlockSpec((tm, wd), lambda i: (i, 0))
    col = lambda wd: pl.BlockSpec((wd, tm), lambda i: (0, i))
    full = lambda a: pl.BlockSpec(a.shape, lambda i: (0,) * a.ndim)
    bf = jnp.bfloat16
    rows = lambda wd: (row(wd), jax.ShapeDtypeStruct((n, wd), bf))
    cols = lambda wd: (col(wd), jax.ShapeDtypeStruct((wd, n), bf))
    outs = [rows(256), rows(256), cols(256), rows(256), rows(128), rows(128),
            rows(256), rows(128), cols(128), rows(512), rows(512), cols(256)]
    return pl.pallas_call(
        _pre_kernel,
        grid=(n // tm,),
        in_specs=[row(D_MODEL), full(g), full(w), full(cq_g), full(ck_g),
                  pl.BlockSpec((3, tm, 256), lambda i: (0, i % nt, 0)),
                  full(dq_g), full(dkv_g), full(wuq), full(wukvk), full(wukvv),
                  pl.BlockSpec((3, tm, 128), lambda i: (0, i % nt, 0))],
        out_specs=[o[0] for o in outs],
        out_shape=[o[1] for o in outs],
        compiler_params=_cparams(("parallel",)),
        name="pre_attention",
    )(x2, g, w, cq_g, ck_g, ctab, dq_g, dkv_g, wuq, wukvk, wukvv, dtab)


_NT = (((1,), (1,)), ((), ()))


def _scores(chain):
    q, k, _, bias = chain
    st = lax.dot_general(k, q, _NT, preferred_element_type=jnp.float32)
    if bias is None:
        return st
    return st + jnp.concatenate([bias(j) for j in range(st.shape[0] // BLOCK)], axis=0)


def _softmax_t(st):
    fold = lambda a: a.reshape(BLOCK // 8, 8, a.shape[1])
    m = jnp.max(st, axis=0, keepdims=True)
    acc, parts = None, []
    for r0 in range(0, st.shape[0], BLOCK):
        p = jnp.exp2(st[r0:r0 + BLOCK] - m)
        part = jnp.sum(fold(p), axis=0)
        acc = part if acc is None else acc + part
        parts.append(p.astype(jnp.bfloat16))
    return jnp.concatenate(parts, axis=0), 1.0 / jnp.sum(acc, axis=0, keepdims=True)


def _attend_t(chains):
    n = len(chains)
    st, pr, out = [None] * n, [None] * n, [None] * n
    for t in range(n + 2):
        if t < n:
            st[t] = _scores(chains[t])
        if 0 <= t - 2 < n:
            pt, r = pr[t - 2]
            out[t - 2] = jnp.dot(chains[t - 2][2], pt, preferred_element_type=jnp.float32) * r
            pr[t - 2] = None
        if 0 <= t - 1 < n:
            pr[t - 1] = _softmax_t(st[t - 1])
            st[t - 1] = None
    return out


def _diff_kernel(q_ref, k_ref, vt_ref, gt_ref, dl_ref, sub_ref, o_ref, *, lam_init, seq, tq):
    i = pl.program_id(1)
    lane = lax.broadcasted_iota(jnp.int32, (tq, LANES), 1)
    upper_rows = lax.broadcasted_iota(jnp.int32, (LANES, tq), 0) >= HEAD_DIM
    dl = dl_ref[0]
    lam = (jnp.exp(jnp.sum(dl[0:1] * dl[1:2], axis=-1, keepdims=True))
           - jnp.exp(jnp.sum(dl[2:3] * dl[3:4], axis=-1, keepdims=True)) + lam_init)
    offs = [pl.multiple_of(jnp.clip(jb * BLOCK - i * tq, -NEAR, tq + BLOCK) + NEAR, BLOCK)
            for jb in range(seq // BLOCK)]
    chains = []
    for pr in range(2):
        pl_ = slice(pr * LANES, (pr + 1) * LANES)
        q, k, vt = q_ref[0, :, pl_], k_ref[0, :, pl_], vt_ref[pl_, :]
        for hh in range(2):
            bias_t = lambda jb, h=2 * pr + hh: gt_ref[h, pl.ds(offs[jb], BLOCK), :]
            for c in range(2):
                lo = hh * HEAD_DIM + c * DIFF_HALF
                qm = jnp.where((lane >= lo) & (lane < lo + DIFF_HALF), q, jnp.zeros_like(q))
                chains.append((qm, k, vt, bias_t))
    outs = _attend_t(chains)
    for pr in range(2):
        heads = []
        for hh in range(2):
            oh = outs[4 * pr + 2 * hh] - lam * outs[4 * pr + 2 * hh + 1]
            mh = upper_rows if hh else jnp.logical_not(upper_rows)
            ss = jnp.sum(jnp.where(mh, oh * oh, 0.0), axis=0, keepdims=True) * (1.0 / HEAD_DIM)
            heads.append(oh * lax.rsqrt(ss + EPS))
        out = jnp.where(upper_rows, heads[1], heads[0]).T * sub_ref[...] * (1.0 - lam_init)
        o_ref[0, :, pr * LANES:(pr + 1) * LANES] = out.astype(jnp.bfloat16)


def _diff_attention(q, k, vt, gt, dl, sub, lam_init, tq):
    b, seq, _ = q.shape
    return pl.pallas_call(
        functools.partial(_diff_kernel, lam_init=lam_init, seq=seq, tq=tq),
        grid=(b, seq // tq),
        in_specs=[pl.BlockSpec((1, tq, GROUP_WIDTH), lambda bi, i: (bi, i, 0)),
                  pl.BlockSpec((1, seq, GROUP_WIDTH), lambda bi, i: (bi, 0, 0)),
                  pl.BlockSpec((GROUP_WIDTH, seq), lambda bi, i: (0, bi)),
                  pl.BlockSpec(gt.shape, lambda bi, i: (0, 0, 0)),
                  pl.BlockSpec((1, 4, DIFF_HALF), lambda bi, i: (0, 0, 0)),
                  pl.BlockSpec((1, LANES), lambda bi, i: (0, 0))],
        out_specs=pl.BlockSpec((1, tq, GROUP_WIDTH), lambda bi, i: (bi, i, 0)),
        out_shape=jax.ShapeDtypeStruct((b, seq, GROUP_WIDTH), jnp.bfloat16),
        compiler_params=_cparams(("parallel", "arbitrary")),
        name="diff_attention",
    )(q, k, vt, gt, dl, sub)


def _gqa_kernel(q_ref, k_ref, vt_ref, o_ref, *, tq, cq):
    k, vt = k_ref[0], vt_ref[...]
    lower = lax.broadcasted_iota(jnp.int32, (cq, LANES), 1) < HEAD_DIM
    upper_rows = lax.broadcasted_iota(jnp.int32, (LANES, cq), 0) >= HEAD_DIM
    chains = []
    for r0 in range(0, tq, cq):
        for j in range(2):
            q = q_ref[0, r0:r0 + cq, j * LANES:(j + 1) * LANES]
            zero = jnp.zeros_like(q)
            chains.append((jnp.where(lower, q, zero), k, vt, None))
            chains.append((jnp.where(lower, zero, q), k, vt, None))
    outs = _attend_t(chains)
    for n, r0 in enumerate(range(0, tq, cq)):
        for j in range(2):
            lo, hi = outs[4 * n + 2 * j], outs[4 * n + 2 * j + 1]
            o_ref[0, r0:r0 + cq, j * LANES:(j + 1) * LANES] = jnp.where(
                upper_rows, hi, lo).T.astype(jnp.bfloat16)


def _gqa_attention(q, k, vt, tq, cq):
    b, seq, _ = q.shape
    return pl.pallas_call(
        functools.partial(_gqa_kernel, tq=tq, cq=cq),
        grid=(b, seq // tq),
        in_specs=[pl.BlockSpec((1, tq, GROUP_WIDTH), lambda bi, i: (bi, i, 0)),
                  pl.BlockSpec((1, seq, LANES), lambda bi, i: (bi, 0, 0)),
                  pl.BlockSpec((LANES, seq), lambda bi, i: (0, bi))],
        out_specs=pl.BlockSpec((1, tq, GROUP_WIDTH), lambda bi, i: (bi, i, 0)),
        out_shape=jax.ShapeDtypeStruct((b, seq, GROUP_WIDTH), jnp.bfloat16),
        compiler_params=_cparams(("parallel", "arbitrary")),
        name="axial_gqa_attention",
    )(q, k, vt)


def _mla_kernel(q_ref, k_ref, vt_ref, o_ref, *, tq, cq):
    upper_rows = lax.broadcasted_iota(jnp.int32, (LANES, cq), 0) >= HEAD_DIM
    chains = []
    for r0 in range(0, tq, cq):
        for hh in range(4):
            hl = slice(hh * LANES, (hh + 1) * LANES)
            vl = slice((hh // 2) * LANES, (hh // 2 + 1) * LANES)
            chains.append((q_ref[0, r0:r0 + cq, hl], k_ref[0, :, hl], vt_ref[vl, :], None))
    outs = _attend_t(chains)
    for n, r0 in enumerate(range(0, tq, cq)):
        for pr in range(2):
            lo, hi = outs[4 * n + 2 * pr], outs[4 * n + 2 * pr + 1]
            o_ref[0, r0:r0 + cq, pr * LANES:(pr + 1) * LANES] = jnp.where(
                upper_rows, hi, lo).T.astype(jnp.bfloat16)


def _mla_attention(q, k, vt, tq, cq):
    b, seq, _ = q.shape
    return pl.pallas_call(
        functools.partial(_mla_kernel, tq=tq, cq=cq),
        grid=(b, seq // tq),
        in_specs=[pl.BlockSpec((1, tq, 4 * LANES), lambda bi, i: (bi, i, 0)),
                  pl.BlockSpec((1, seq, 4 * LANES), lambda bi, i: (bi, 0, 0)),
                  pl.BlockSpec((GROUP_WIDTH, seq), lambda bi, i: (0, bi))],
        out_specs=pl.BlockSpec((1, tq, GROUP_WIDTH), lambda bi, i: (bi, i, 0)),
        out_shape=jax.ShapeDtypeStruct((b, seq, GROUP_WIDTH), jnp.bfloat16),
        compiler_params=_cparams(("parallel", "arbitrary")),
        name="latent_attention",
    )(q, k, vt)


def _softmax_parts(s, extra):
    m = jnp.maximum(jnp.max(s, axis=-1, keepdims=True), extra)
    p = jnp.exp(s - m)
    l = jnp.sum(p, axis=-1, keepdims=True) + jnp.exp(extra - m)
    return p.astype(jnp.bfloat16), 1.0 / l


def _win_kernel(sink_ref, q_ref, kp_ref, kc_ref, kn_ref, vp_ref, vc_ref, vn_ref, wb_ref, o_ref,
                *, nstep, nb):
    n = pl.program_id(1)
    kext = jnp.concatenate([kp_ref[0], kc_ref[0], kn_ref[0]], axis=0)
    vext = jnp.concatenate([vp_ref[0], vc_ref[0], vn_ref[0]], axis=0)
    lane = lax.broadcasted_iota(jnp.int32, (BLOCK, LANES), 1)
    col = lax.broadcasted_iota(jnp.int32, (BLOCK, 3 * BLOCK), 1)
    units = [(blk, j) for blk in range(nb) for j in range(2)]

    def scores(blk, j):
        q = q_ref[0, blk * BLOCK:(blk + 1) * BLOCK, j * LANES:(j + 1) * LANES]
        zero = jnp.zeros_like(q)
        qs = jnp.concatenate([jnp.where(lane < HEAD_DIM, q, zero),
                              jnp.where(lane >= HEAD_DIM, q, zero)], axis=0)
        return lax.dot_general(qs, kext[blk * BLOCK:(blk + 3) * BLOCK], _NT,
                               preferred_element_type=jnp.float32)

    def softmax(blk, j, s):
        ps, rs = [], []
        for g in range(2):
            head = j + 2 * g
            sg = s[g * BLOCK:(g + 1) * BLOCK] + wb_ref[head]
            if blk == 0:
                sg = jnp.where((col < BLOCK) & (n == 0), NEG_BIG, sg)
            if blk == nb - 1:
                sg = jnp.where((col >= 2 * BLOCK) & (n == nstep - 1), NEG_BIG, sg)
            p, r = _softmax_parts(sg, sink_ref[head])
            ps.append(p)
            rs.append(r)
        return jnp.concatenate(ps, axis=0), rs

    def pv(blk, j, p, rs):
        o = jnp.dot(p, vext[blk * BLOCK:(blk + 3) * BLOCK], preferred_element_type=jnp.float32)
        out = jnp.where(lane < HEAD_DIM, o[:BLOCK] * rs[0], o[BLOCK:] * rs[1])
        o_ref[0, blk * BLOCK:(blk + 1) * BLOCK, j * LANES:(j + 1) * LANES] = out.astype(jnp.bfloat16)

    s, pr = {}, {}
    for t in range(len(units) + 2):
        if t < len(units):
            s[t] = scores(*units[t])
        if 0 <= t - 2 < len(units):
            pv(*units[t - 2], *pr.pop(t - 2))
        if 0 <= t - 1 < len(units):
            pr[t - 1] = softmax(*units[t - 1], s.pop(t - 1))


def _win_attention(q, k, v, sink, wbias, tw):
    b, seq, _ = q.shape
    nblk = seq // BLOCK
    nb = tw // BLOCK
    prev = lambda bi, n: (bi, jnp.maximum(n * nb - 1, 0), 0)
    cur = lambda bi, n: (bi, n, 0)
    nxt = lambda bi, n: (bi, jnp.minimum((n + 1) * nb, nblk - 1), 0)
    halo = lambda f: pl.BlockSpec((1, BLOCK, LANES), f)
    body = pl.BlockSpec((1, tw, LANES), cur)
    return pl.pallas_call(
        functools.partial(_win_kernel, nstep=seq // tw, nb=nb),
        grid=(b, seq // tw),
        in_specs=[pl.BlockSpec(memory_space=pltpu.SMEM),
                  pl.BlockSpec((1, tw, GROUP_WIDTH), cur),
                  halo(prev), body, halo(nxt), halo(prev), body, halo(nxt),
                  pl.BlockSpec((4, BLOCK, 3 * BLOCK), lambda bi, n: (0, 0, 0))],
        out_specs=pl.BlockSpec((1, tw, GROUP_WIDTH), cur),
        out_shape=jax.ShapeDtypeStruct((b, seq, GROUP_WIDTH), jnp.bfloat16),
        compiler_params=_cparams(("parallel", "arbitrary")),
        name="window_attention",
    )(sink, q, k, k, k, v, v, v, wbias)


def _outproj_kernel(x_ref, a_ref, b_ref, c_ref, d_ref, w_ref, g_ref, o_ref):
    tm = x_ref.shape[0]
    halves = [slice(0, tm // 2), slice(tm // 2, tm)]

    def project(r):
        mixed = jnp.concatenate([a_ref[r, :], b_ref[r, :], c_ref[r, :], d_ref[r, :]], axis=1)
        return jnp.dot(mixed, w_ref[...], preferred_element_type=jnp.float32)

    ys = [project(r) for r in halves]
    for r, y in zip(halves, ys):
        o_ref[r, :] = x_ref[r, :] + _rms(y, g_ref[...])


def _outproj(x2, oa, ob, oc, od, w, g, tm):
    n = x2.shape[0]
    row = lambda wd: pl.BlockSpec((tm, wd), lambda i: (i, 0))
    return pl.pallas_call(
        _outproj_kernel,
        grid=(n // tm,),
        in_specs=[row(D_MODEL), row(256), row(256), row(256), row(256),
                  pl.BlockSpec(w.shape, lambda i: (0, 0)),
                  pl.BlockSpec(g.shape, lambda i: (0, 0))],
        out_specs=row(D_MODEL),
        out_shape=jax.ShapeDtypeStruct((n, D_MODEL), jnp.float32),
        compiler_params=_cparams(("parallel",)),
        name="out_projection",
    )(x2, oa, ob, oc, od, w, g)


HALO = 8
DOWN_GROUP_ENDS = (5, 10, 11)


def _ffn_kernel(x_ref, xp_ref, xn_ref, gpre_ref, wg_ref, wv_ref, cw_ref, cb_ref, wdn_ref, gpost_ref,
                o_ref, h_ref, *, tm, fc, tiles_per_seq):
    i = pl.program_id(0)
    bf = jnp.bfloat16
    g = gpre_ref[...]
    first = (i % tiles_per_seq) == 0
    last = (i % tiles_per_seq) == tiles_per_seq - 1
    h_ref[0:HALO, :] = jnp.where(first, 0.0, _rms(xp_ref[...], g)).astype(bf)
    h_ref[HALO:HALO + tm, :] = _rms(x_ref[...], g).astype(bf)
    h_ref[HALO + tm:, :] = jnp.where(last, 0.0, _rms(xn_ref[...], g)).astype(bf)
    nf = D_FF // fc
    chunk = lambda c: slice(c * fc, (c + 1) * fc)

    def up(c):
        gate = jnp.dot(h_ref[...], wg_ref[:, chunk(c)], preferred_element_type=jnp.float32)
        val = jnp.dot(h_ref[HALO:HALO + tm, :], wv_ref[:, chunk(c)], preferred_element_type=jnp.float32)
        return gate, val

    y = None
    acts, group_start = [], 0
    nxt = up(0)
    for c in range(nf):
        cs = chunk(c)
        gate, val = nxt
        if c + 1 < nf:
            nxt = up(c + 1)
        gate = (gate[HALO - 1:HALO - 1 + tm] * cw_ref[0:1, cs] + gate[HALO:HALO + tm] * cw_ref[1:2, cs]
                + gate[HALO + 1:HALO + 1 + tm] * cw_ref[2:3, cs] + cb_ref[:, cs])
        acts.append((0.5 * gate * (1.0 + lax.erf(gate * (2.0 ** -0.5))) * val).astype(bf))
        if c + 1 in DOWN_GROUP_ENDS:
            rows = slice(group_start * fc, (c + 1) * fc)
            part = jnp.dot(jnp.concatenate(acts, axis=1), wdn_ref[rows, :],
                           preferred_element_type=jnp.float32)
            y = part if y is None else y + part
            acts, group_start = [], c + 1
    o_ref[...] = x_ref[...] + _rms(y, gpost_ref[...])


def _ffn(x2, gpre, wg, wv, cw, cb, wdn, gpost, seq, tm, fc):
    n = x2.shape[0]
    tph = tm // HALO
    nh = n // HALO
    resident = lambda a: pl.BlockSpec(a.shape, lambda i: (0,) * a.ndim,
                                      pipeline_mode=pl.Buffered(1))
    return pl.pallas_call(
        functools.partial(_ffn_kernel, tm=tm, fc=fc, tiles_per_seq=seq // tm),
        grid=(n // tm,),
        in_specs=[pl.BlockSpec((tm, D_MODEL), lambda i: (i, 0)),
                  pl.BlockSpec((HALO, D_MODEL), lambda i: (jnp.maximum(i * tph - 1, 0), 0)),
                  pl.BlockSpec((HALO, D_MODEL), lambda i: (jnp.minimum((i + 1) * tph, nh - 1), 0)),
                  resident(gpre), resident(wg), resident(wv), resident(cw), resident(cb),
                  resident(wdn), resident(gpost)],
        out_specs=pl.BlockSpec((tm, D_MODEL), lambda i: (i, 0)),
        out_shape=jax.ShapeDtypeStruct((n, D_MODEL), jnp.float32),
        scratch_shapes=[pltpu.VMEM((tm + 2 * HALO, D_MODEL), jnp.bfloat16)],
        compiler_params=_cparams(("parallel",)),
        name="conv_glu",
    )(x2, x2, x2, gpre, wg, wv, cw, cb, wdn, gpost)


_GQA_HEAD_ORDER = (0, 2, 1, 3)


def _reorder_heads(w, order):
    lead = w.shape[:-1]
    nh = w.shape[-1] // HEAD_DIM
    w = w.reshape(lead + (nh, HEAD_DIM))
    return jnp.stack([w[..., o, :] for o in order], axis=-2).reshape(lead + (nh * HEAD_DIM,))


def _axial_perm(w):
    lead = w.shape[:-1]
    nh = w.shape[-1] // HEAD_DIM
    w = w.reshape(lead + (nh, 2, 2, 16))
    return jnp.swapaxes(w, -3, -2).reshape(lead + (nh * HEAD_DIM,))


def _prep_w_in(w_in):
    w_in = w_in.astype(jnp.bfloat16)
    bq = _reorder_heads(w_in[..., 768:1024], _GQA_HEAD_ORDER)
    cq = _axial_perm(_reorder_heads(w_in[..., 1280:1536], _GQA_HEAD_ORDER))
    ck = _axial_perm(w_in[..., 1536:1664])
    kr = w_in[..., 2176:2208]
    z = lambda wd: jnp.zeros(w_in.shape[:-1] + (wd,), w_in.dtype)
    return jnp.concatenate(
        [w_in[..., 0:768], bq, w_in[..., 1024:1280], cq, ck, w_in[..., 1664:1792],
         w_in[..., 1792:2176], z(MLA_NOPE), kr, z(LANES - MLA_NOPE - MLA_ROPE)], axis=-1)


def _rope_tables(cos, sin, lead_ones, trail_ones):
    s = cos.shape[0]
    one = lambda wd: jnp.ones((s, wd), jnp.float32)
    zero = lambda wd: jnp.zeros((s, wd), jnp.float32)
    c = jnp.concatenate([one(lead_ones), cos, cos, one(trail_ones)], axis=1)
    sn = jnp.concatenate([zero(lead_ones), -sin, zero(cos.shape[1]), zero(trail_ones)], axis=1)
    sp = jnp.concatenate([zero(lead_ones), zero(cos.shape[1]), sin, zero(trail_ones)], axis=1)
    return jnp.stack([c, sn, sp])


def _angles(pos, dim):
    inv = ROPE_THETA ** (-jnp.arange(0, dim, 2, dtype=jnp.float32) / dim)
    ang = pos.astype(jnp.float32)[:, None] * inv[None, :]
    return jnp.cos(ang), jnp.sin(ang)


def _forward(x, rel_bias, attn_pre_norm, w_in, diff_lambda, diff_subln, win_sink, ax_q_norm, ax_k_norm,
             mla_q_norm, mla_kv_norm, mla_w_uq, mla_w_ukv, w_out, attn_post_norm, ffn_pre_norm,
             ffn_w_up, ffn_conv_w, ffn_conv_b, ffn_w_down, ffn_post_norm, *, tq_a, tq, cq, tw, tm, tf, fc):
    bsz, seq, _ = x.shape
    depth = w_in.shape[0]
    n = bsz * seq
    bf = jnp.bfloat16

    pos = jnp.arange(seq, dtype=jnp.int32)
    rows = seq // GRID_W
    row_ids = jnp.repeat(jnp.arange(rows, dtype=jnp.int32), GRID_W)
    col_ids = jnp.tile(jnp.arange(GRID_W, dtype=jnp.int32), rows)
    rcos, rsin = _angles(row_ids, HEAD_DIM // 2)
    ccos, csin = _angles(col_ids, HEAD_DIM // 2)
    ctab = _rope_tables(jnp.concatenate([rcos, ccos], 1), jnp.concatenate([rsin, csin], 1), 0, 0)
    ctab = jnp.tile(ctab, (1, 1, 4))
    mcos, msin = _angles(pos, MLA_ROPE)
    dtab = _rope_tables(mcos, msin, MLA_NOPE, LANES - MLA_NOPE - MLA_ROPE)

    rel_a = (jnp.arange(tq_a + 2 * NEAR, dtype=jnp.int32)[:, None] - NEAR
             - jnp.arange(tq_a, dtype=jnp.int32)[None, :])
    gt = _bias_tiles(rel_bias, _bucket(rel_a), 0, 4, LOG2E)
    rel_b = jnp.arange(3 * BLOCK, dtype=jnp.int32)[None, :] - BLOCK - jnp.arange(BLOCK, dtype=jnp.int32)[:, None]
    idx_b = jnp.where(jnp.abs(rel_b) <= WINDOW, _bucket(rel_b), -1)
    wbias = _bias_tiles(rel_bias, idx_b, 4, 4, 1.0)

    w_in_p = _prep_w_in(w_in)
    cq_g = jnp.tile(_axial_perm(ax_q_norm), (1, 4))[:, None, :]
    ck_g = jnp.tile(_axial_perm(ax_k_norm), (1, 2))[:, None, :]
    wuq = jnp.pad(mla_w_uq.reshape(depth, MLA_Q_RANK, 4, MLA_NOPE + MLA_ROPE),
                  ((0, 0), (0, 0), (0, 0), (0, LANES - MLA_NOPE - MLA_ROPE))
                  ).reshape(depth, MLA_Q_RANK, 4 * LANES).astype(bf)
    wukv = mla_w_ukv.reshape(depth, MLA_KV_RANK, 4, 2, HEAD_DIM)
    wukvk = jnp.pad(wukv[:, :, :, 0], ((0, 0), (0, 0), (0, 0), (0, HEAD_DIM))
                    ).reshape(depth, MLA_KV_RANK, 4 * LANES).astype(bf)
    wukvv = wukv[:, :, :, 1].reshape(depth, MLA_KV_RANK, GROUP_WIDTH).astype(bf)
    wo = w_out.astype(bf).reshape(depth, 4, 4, HEAD_DIM, D_MODEL)
    slots = [(0, h) for h in range(4)] + [(1, h) for h in _GQA_HEAD_ORDER] \
        + [(2, h) for h in _GQA_HEAD_ORDER] + [(3, h) for h in range(4)]
    wo = jnp.stack([wo[:, m, h] for m, h in slots], axis=1).reshape(depth, D_MODEL, D_MODEL)
    wg = ffn_w_up[:, :, :D_FF].astype(bf)
    wv = ffn_w_up[:, :, D_FF:].astype(bf)
    wdn = ffn_w_down.astype(bf)
    sub = jnp.tile(diff_subln, (1, 2))[:, None, :]

    x2 = x.reshape(n, D_MODEL)
    r3 = lambda a: a.reshape(bsz, seq, a.shape[-1])
    for l in range(depth):
        lam_init = 0.8 - 0.6 * math.exp(-0.3 * l)
        (aq, ak, avt, bq, bk, bv, cq_, ck, cvt, dq, dk, dvt) = _pre(
            x2, attn_pre_norm[l][None], w_in_p[l], cq_g[l], ck_g[l], ctab,
            mla_q_norm[l][None], mla_kv_norm[l][None], wuq[l], wukvk[l], wukvv[l], dtab, seq, tm)
        oa = _diff_attention(r3(aq), r3(ak), avt, gt, diff_lambda[l][None], sub[l], lam_init, tq_a)
        ob = _win_attention(r3(bq), r3(bk), r3(bv), win_sink[l], wbias, tw)
        oc = _gqa_attention(r3(cq_), r3(ck), cvt, tq, cq)
        od = _mla_attention(r3(dq), r3(dk), dvt, tq, cq)
        x2 = _outproj(x2, oa.reshape(n, -1), ob.reshape(n, -1), oc.reshape(n, -1),
                      od.reshape(n, -1), wo[l], attn_post_norm[l][None], tm)
        x2 = _ffn(x2, ffn_pre_norm[l][None], wg[l], wv[l], ffn_conv_w[l], ffn_conv_b[l][None],
                  wdn[l], ffn_post_norm[l][None], seq, tf, fc)
    return x2.reshape(bsz, seq, D_MODEL)


def kernel(x, rel_bias, attn_pre_norm, w_in, diff_lambda, diff_subln, win_sink, ax_q_norm, ax_k_norm,
           mla_q_norm, mla_kv_norm, mla_w_uq, mla_w_ukv, w_out, attn_post_norm, ffn_pre_norm,
           ffn_w_up, ffn_conv_w, ffn_conv_b, ffn_w_down, ffn_post_norm):
    return _forward(x, rel_bias, attn_pre_norm, w_in, diff_lambda, diff_subln, win_sink, ax_q_norm,
                    ax_k_norm, mla_q_norm, mla_kv_norm, mla_w_uq, mla_w_ukv, w_out, attn_post_norm,
                    ffn_pre_norm, ffn_w_up, ffn_conv_w, ffn_conv_b, ffn_w_down, ffn_post_norm,
                    tq_a=256, tq=512, cq=256, tw=512, tm=512, tf=512, fc=256)
```

```python
import functools
import math

import jax
import jax.numpy as jnp
from jax import lax
from jax.experimental import pallas as pl
from jax.experimental.pallas import tpu as pltpu

D_MODEL = 1024
HEAD_DIM = 64
GROUP_WIDTH = 256
BLOCK = 128
DIFF_HALF = 32
WINDOW = 128
GRID_W = 64
ROPE_THETA = 10000.0
MLA_Q_RANK = 256
MLA_KV_RANK = 128
MLA_NOPE = 64
MLA_ROPE = 32
REL_BUCKETS = 32
REL_MAX_DIST = 128
D_FF = 2816
EPS = 1e-6
LANES = 128
LOG2E = 1.4426950408889634
NEG_BIG = -1e30
VMEM_LIMIT = 56 * 1024 * 1024
NEAR = 2 * BLOCK


def _cparams(sem):
    return pltpu.CompilerParams(dimension_semantics=sem, vmem_limit_bytes=VMEM_LIMIT)


def _rms(x, g):
    return x * lax.rsqrt(jnp.mean(x * x, axis=-1, keepdims=True) + EPS) * g


def _group_rms(x, ngroups):
    grp = lax.broadcasted_iota(jnp.int32, x.shape, 1) // HEAD_DIM
    x2 = x * x
    inv = jnp.zeros_like(x)
    for h in range(ngroups):
        m = grp == h
        ss = jnp.sum(jnp.where(m, x2, 0.0), axis=-1, keepdims=True)
        inv = jnp.where(m, lax.rsqrt(ss * (1.0 / HEAD_DIM) + EPS), inv)
    return x * inv


def _rope(x, c, sn, sp, half):
    w = x.shape[-1]
    return x * c + pltpu.roll(x, w - half, 1) * sn + pltpu.roll(x, half, 1) * sp


def _bias_kernel(tab_ref, idx_ref, o_ref, *, head0, scale):
    h = pl.program_id(0)
    idx = idx_ref[...]
    acc = jnp.zeros(idx.shape, jnp.float32)
    for b in range(REL_BUCKETS):
        acc = jnp.where(idx == b, tab_ref[b, head0 + h], acc)
    o_ref[0] = jnp.where(idx < 0, NEG_BIG, acc * scale)


def _bias_tiles(rel_bias, idx, head0, nheads, scale):
    r, c = idx.shape
    return pl.pallas_call(
        functools.partial(_bias_kernel, head0=head0, scale=scale),
        grid=(nheads,),
        in_specs=[pl.BlockSpec(memory_space=pltpu.SMEM),
                  pl.BlockSpec((r, c), lambda h: (0, 0))],
        out_specs=pl.BlockSpec((1, r, c), lambda h: (h, 0, 0)),
        out_shape=jax.ShapeDtypeStruct((nheads, r, c), jnp.float32),
        compiler_params=_cparams(("arbitrary",)),
        name="rel_bias_tiles",
    )(rel_bias, idx)


def _bucket(rel):
    half = REL_BUCKETS // 2
    max_exact = half // 2
    n = jnp.abs(rel)
    nf = jnp.maximum(n, 1).astype(jnp.float32)
    large = max_exact + (jnp.log(nf / max_exact) / math.log(REL_MAX_DIST / max_exact)
                         * (half - max_exact)).astype(jnp.int32)
    large = jnp.minimum(large, half - 1)
    return jnp.where(rel > 0, half, 0) + jnp.where(n < max_exact, n, large)


def _pre_kernel(x_ref, g_ref, w_ref, cq_g_ref, ck_g_ref, ctab_ref, dq_g_ref, dkv_g_ref,
                wuq_ref, wukvk_ref, wukvv_ref, dtab_ref,
                aq_ref, ak_ref, avt_ref, bq_ref, bk_ref, bv_ref, cq_ref, ck_ref, cvt_ref,
                dq_ref, dk_ref, dvt_ref):
    bf = jnp.bfloat16
    tm = x_ref.shape[0]
    rep = lambda t: jnp.concatenate([t] * 4, axis=1)

    def epi_c(r, p):
        cc, csn, csp = ctab_ref[0, r], ctab_ref[1, r], ctab_ref[2, r]
        q = _group_rms(p[:, 0:256], 4) * cq_g_ref[...]
        cq_ref[r, :] = (_rope(q, cc, csn, csp, 32) * (HEAD_DIM ** -0.5 * LOG2E)).astype(bf)
        k = _group_rms(p[:, 256:384], 2) * ck_g_ref[...]
        ck_ref[r, :] = _rope(k, cc[:, :128], csn[:, :128], csp[:, :128], 32).astype(bf)
        cvt_ref[:, r] = p[:, 384:512].T.astype(bf)

    def epi_d(r, p):
        dc, dsn, dsp = dtab_ref[0, r], dtab_ref[1, r], dtab_ref[2, r]
        cq = _rms(p[:, 0:256], dq_g_ref[...]).astype(bf)
        q = jnp.dot(cq, wuq_ref[...], preferred_element_type=jnp.float32)
        q = _rope(q, rep(dc), rep(dsn), rep(dsp), 16)
        dq_ref[r, :] = (q * ((MLA_NOPE + MLA_ROPE) ** -0.5 * LOG2E)).astype(bf)
        ckv = _rms(p[:, 256:384], dkv_g_ref[...]).astype(bf)
        kr = _rope(p[:, 384:512], dc, dsn, dsp, 16)
        kn = jnp.dot(ckv, wukvk_ref[...], preferred_element_type=jnp.float32)
        dk_ref[r, :] = (kn + rep(kr)).astype(bf)
        dvt_ref[:, r] = jnp.dot(ckv, wukvv_ref[...], preferred_element_type=jnp.float32).T.astype(bf)

    def epi_a(r, p):
        aq_ref[r, :] = (p[:, 0:256] * (DIFF_HALF ** -0.5 * LOG2E)).astype(bf)
        ak_ref[r, :] = p[:, 256:512].astype(bf)
        avt_ref[:, r] = p[:, 512:768].T.astype(bf)

    def epi_b(r, p):
        bq_ref[r, :] = (p[:, 0:256] * 0.125).astype(bf)
        bk_ref[r, :] = p[:, 256:384].astype(bf)
        bv_ref[r, :] = p[:, 384:512].astype(bf)

    mixers = [((1280, 1792), epi_c), ((1792, 2304), epi_d), ((0, 768), epi_a), ((768, 1280), epi_b)]
    halves = [slice(0, tm // 2), slice(tm // 2, tm)]
    pending = None
    for r in halves:
        h = _rms(x_ref[r, :], g_ref[...]).astype(bf)
        for (lo, hi), epilogue in mixers:
            p = jnp.dot(h, w_ref[:, lo:hi], preferred_element_type=jnp.float32)
            if pending is not None:
                pending[0](pending[1], pending[2])
            pending = (epilogue, r, p)
    pending[0](pending[1], pending[2])


def _pre(x2, g, w, cq_g, ck_g, ctab, dq_g, dkv_g, wuq, wukvk, wukvv, dtab, seq, tm):
    n = x2.shape[0]
    nt = seq // tm
    row = lambda wd: pl.BlockSpec((tm, wd), lambda i: (i, 0))
    col = lambda wd: pl.BlockSpec((wd, tm), lambda i: (0, i))
    full = lambda a: pl.BlockSpec(a.shape, lambda i: (0,) * a.ndim)
    bf = jnp.bfloat16
    rows = lambda wd: (row(wd), jax.ShapeDtypeStruct((n, wd), bf))
    cols = lambda wd: (col(wd), jax.ShapeDtypeStruct((wd, n), bf))
    outs = [rows(256), rows(256), cols(256), rows(256), rows(128), rows(128),
            rows(256), rows(128), cols(128), rows(512), rows(512), cols(256)]
    return pl.pallas_call(
        _pre_kernel,
        grid=(n // tm,),
        in_specs=[row(D_MODEL), full(g), full(w), full(cq_g), full(ck_g),
                  pl.BlockSpec((3, tm, 256), lambda i: (0, i % nt, 0)),
                  full(dq_g), full(dkv_g), full(wuq), full(wukvk), full(wukvv),
                  pl.BlockSpec((3, tm, 128), lambda i: (0, i % nt, 0))],
        out_specs=[o[0] for o in outs],
        out_shape=[o[1] for o in outs],
        compiler_params=_cparams(("parallel",)),
        name="pre_attention",
    )(x2, g, w, cq_g, ck_g, ctab, dq_g, dkv_g, wuq, wukvk, wukvv, dtab)


_NT = (((1,), (1,)), ((), ()))


FLASH_KEYS = 512
FLASH_LAG = 4


def _attend_flash(chains):
    n = len(chains)
    nblk = chains[0][1].shape[0] // FLASH_KEYS
    fold = lambda a: a.reshape(a.shape[0] // 8, 8, a.shape[1])
    state = [None] * n

    def scores(b, c):
        q, k, _, bias = chains[c]
        rows = slice(b * FLASH_KEYS, (b + 1) * FLASH_KEYS)
        s = lax.dot_general(k[rows], q, _NT, preferred_element_type=jnp.float32)
        if bias is not None:
            per = FLASH_KEYS // BLOCK
            s = s + jnp.concatenate([bias(b * per + j) for j in range(per)], axis=0)
        return s

    def update(b, c, s):
        vt = chains[c][2][:, b * FLASH_KEYS:(b + 1) * FLASH_KEYS]
        mb = jnp.max(jnp.max(fold(s), axis=0), axis=0, keepdims=True)
        if state[c] is None:
            m = mb
            p = jnp.exp2(s - m)
            l = jnp.sum(jnp.sum(fold(p), axis=0), axis=0, keepdims=True)
            acc = jnp.dot(vt, p.astype(jnp.bfloat16), preferred_element_type=jnp.float32)
        else:
            m0, l0, acc0 = state[c]
            m = jnp.maximum(m0, mb)
            alpha = jnp.exp2(m0 - m)
            p = jnp.exp2(s - m)
            l = alpha * l0 + jnp.sum(jnp.sum(fold(p), axis=0), axis=0, keepdims=True)
            acc = alpha * acc0 + jnp.dot(vt, p.astype(jnp.bfloat16), preferred_element_type=jnp.float32)
        state[c] = (m, l, acc)

    units = [(b, c) for b in range(nblk) for c in range(n)]
    pending = []
    for u in units:
        pending.append((*u, scores(*u)))
        if len(pending) > FLASH_LAG:
            update(*pending.pop(0))
    for item in pending:
        update(*item)
    return [acc * (1.0 / l) for _, l, acc in state]


def _diff_kernel(q_ref, k_ref, vt_ref, gt_ref, dl_ref, sub_ref, o_ref, *, lam_init, seq, tq):
    i = pl.program_id(1)
    lane = lax.broadcasted_iota(jnp.int32, (tq, LANES), 1)
    upper_rows = lax.broadcasted_iota(jnp.int32, (LANES, tq), 0) >= HEAD_DIM
    dl = dl_ref[0]
    lam = (jnp.exp(jnp.sum(dl[0:1] * dl[1:2], axis=-1, keepdims=True))
           - jnp.exp(jnp.sum(dl[2:3] * dl[3:4], axis=-1, keepdims=True)) + lam_init)
    offs = [pl.multiple_of(jnp.clip(jb * BLOCK - i * tq, -NEAR, tq + BLOCK) + NEAR, BLOCK)
            for jb in range(seq // BLOCK)]
    chains = []
    for pr in range(2):
        pl_ = slice(pr * LANES, (pr + 1) * LANES)
        q, k, vt = q_ref[0, :, pl_], k_ref[0, :, pl_], vt_ref[pl_, :]
        for hh in range(2):
            bias_t = lambda jb, h=2 * pr + hh: gt_ref[h, pl.ds(offs[jb], BLOCK), :]
            for c in range(2):
                lo = hh * HEAD_DIM + c * DIFF_HALF
                qm = jnp.where((lane >= lo) & (lane < lo + DIFF_HALF), q, jnp.zeros_like(q))
                chains.append((qm, k, vt, bias_t))
    outs = _attend_flash(chains)
    for pr in range(2):
        heads = []
        for hh in range(2):
            oh = outs[4 * pr + 2 * hh] - lam * outs[4 * pr + 2 * hh + 1]
            mh = upper_rows if hh else jnp.logical_not(upper_rows)
            ss = jnp.sum(jnp.where(mh, oh * oh, 0.0), axis=0, keepdims=True) * (1.0 / HEAD_DIM)
            heads.append(oh * lax.rsqrt(ss + EPS))
        out = jnp.where(upper_rows, heads[1], heads[0]).T * sub_ref[...] * (1.0 - lam_init)
        o_ref[0, :, pr * LANES:(pr + 1) * LANES] = out.astype(jnp.bfloat16)


def _diff_attention(q, k, vt, gt, dl, sub, lam_init, tq):
    b, seq, _ = q.shape
    return pl.pallas_call(
        functools.partial(_diff_kernel, lam_init=lam_init, seq=seq, tq=tq),
        grid=(b, seq // tq),
        in_specs=[pl.BlockSpec((1, tq, GROUP_WIDTH), lambda bi, i: (bi, i, 0)),
                  pl.BlockSpec((1, seq, GROUP_WIDTH), lambda bi, i: (bi, 0, 0)),
                  pl.BlockSpec((GROUP_WIDTH, seq), lambda bi, i: (0, bi)),
                  pl.BlockSpec(gt.shape, lambda bi, i: (0, 0, 0)),
                  pl.BlockSpec((1, 4, DIFF_HALF), lambda bi, i: (0, 0, 0)),
                  pl.BlockSpec((1, LANES), lambda bi, i: (0, 0))],
        out_specs=pl.BlockSpec((1, tq, GROUP_WIDTH), lambda bi, i: (bi, i, 0)),
        out_shape=jax.ShapeDtypeStruct((b, seq, GROUP_WIDTH), jnp.bfloat16),
        compiler_params=_cparams(("parallel", "arbitrary")),
        name="diff_attention",
    )(q, k, vt, gt, dl, sub)


def _gqa_kernel(q_ref, k_ref, vt_ref, o_ref, *, tq, cq):
    k, vt = k_ref[0], vt_ref[...]
    lower = lax.broadcasted_iota(jnp.int32, (cq, LANES), 1) < HEAD_DIM
    upper_rows = lax.broadcasted_iota(jnp.int32, (LANES, cq), 0) >= HEAD_DIM
    chains = []
    for r0 in range(0, tq, cq):
        for j in range(2):
            q = q_ref[0, r0:r0 + cq, j * LANES:(j + 1) * LANES]
            zero = jnp.zeros_like(q)
            chains.append((jnp.where(lower, q, zero), k, vt, None))
            chains.append((jnp.where(lower, zero, q), k, vt, None))
    outs = _attend_flash(chains)
    for n, r0 in enumerate(range(0, tq, cq)):
        for j in range(2):
            lo, hi = outs[4 * n + 2 * j], outs[4 * n + 2 * j + 1]
            o_ref[0, r0:r0 + cq, j * LANES:(j + 1) * LANES] = jnp.where(
                upper_rows, hi, lo).T.astype(jnp.bfloat16)


def _gqa_attention(q, k, vt, tq, cq):
    b, seq, _ = q.shape
    return pl.pallas_call(
        functools.partial(_gqa_kernel, tq=tq, cq=cq),
        grid=(b, seq // tq),
        in_specs=[pl.BlockSpec((1, tq, GROUP_WIDTH), lambda bi, i: (bi, i, 0)),
                  pl.BlockSpec((1, seq, LANES), lambda bi, i: (bi, 0, 0)),
                  pl.BlockSpec((LANES, seq), lambda bi, i: (0, bi))],
        out_specs=pl.BlockSpec((1, tq, GROUP_WIDTH), lambda bi, i: (bi, i, 0)),
        out_shape=jax.ShapeDtypeStruct((b, seq, GROUP_WIDTH), jnp.bfloat16),
        compiler_params=_cparams(("parallel", "arbitrary")),
        name="axial_gqa_attention",
    )(q, k, vt)


def _mla_kernel(q_ref, k_ref, vt_ref, o_ref, *, tq, cq):
    upper_rows = lax.broadcasted_iota(jnp.int32, (LANES, cq), 0) >= HEAD_DIM
    chains = []
    for r0 in range(0, tq, cq):
        for hh in range(4):
            hl = slice(hh * LANES, (hh + 1) * LANES)
            vl = slice((hh // 2) * LANES, (hh // 2 + 1) * LANES)
            chains.append((q_ref[0, r0:r0 + cq, hl], k_ref[0, :, hl], vt_ref[vl, :], None))
    outs = _attend_flash(chains)
    for n, r0 in enumerate(range(0, tq, cq)):
        for pr in range(2):
            lo, hi = outs[4 * n + 2 * pr], outs[4 * n + 2 * pr + 1]
            o_ref[0, r0:r0 + cq, pr * LANES:(pr + 1) * LANES] = jnp.where(
                upper_rows, hi, lo).T.astype(jnp.bfloat16)


def _mla_attention(q, k, vt, tq, cq):
    b, seq, _ = q.shape
    return pl.pallas_call(
        functools.partial(_mla_kernel, tq=tq, cq=cq),
        grid=(b, seq // tq),
        in_specs=[pl.BlockSpec((1, tq, 4 * LANES), lambda bi, i: (bi, i, 0)),
                  pl.BlockSpec((1, seq, 4 * LANES), lambda bi, i: (bi, 0, 0)),
                  pl.BlockSpec((GROUP_WIDTH, seq), lambda bi, i: (0, bi))],
        out_specs=pl.BlockSpec((1, tq, GROUP_WIDTH), lambda bi, i: (bi, i, 0)),
        out_shape=jax.ShapeDtypeStruct((b, seq, GROUP_WIDTH), jnp.bfloat16),
        compiler_params=_cparams(("parallel", "arbitrary")),
        name="latent_attention",
    )(q, k, vt)


def _softmax_parts(s, extra):
    m = jnp.maximum(jnp.max(s, axis=-1, keepdims=True), extra)
    p = jnp.exp(s - m)
    l = jnp.sum(p, axis=-1, keepdims=True) + jnp.exp(extra - m)
    return p.astype(jnp.bfloat16), 1.0 / l


def _win_kernel(sink_ref, q_ref, kp_ref, kc_ref, kn_ref, vp_ref, vc_ref, vn_ref, wb_ref, o_ref,
                *, nstep, nb):
    n = pl.program_id(1)
    kext = jnp.concatenate([kp_ref[0], kc_ref[0], kn_ref[0]], axis=0)
    vext = jnp.concatenate([vp_ref[0], vc_ref[0], vn_ref[0]], axis=0)
    lane = lax.broadcasted_iota(jnp.int32, (BLOCK, LANES), 1)
    col = lax.broadcasted_iota(jnp.int32, (BLOCK, 3 * BLOCK), 1)
    units = [(blk, j) for blk in range(nb) for j in range(2)]

    def scores(blk, j):
        q = q_ref[0, blk * BLOCK:(blk + 1) * BLOCK, j * LANES:(j + 1) * LANES]
        zero = jnp.zeros_like(q)
        qs = jnp.concatenate([jnp.where(lane < HEAD_DIM, q, zero),
                              jnp.where(lane >= HEAD_DIM, q, zero)], axis=0)
        return lax.dot_general(qs, kext[blk * BLOCK:(blk + 3) * BLOCK], _NT,
                               preferred_element_type=jnp.float32)

    def softmax(blk, j, s):
        ps, rs = [], []
        for g in range(2):
            head = j + 2 * g
            sg = s[g * BLOCK:(g + 1) * BLOCK] + wb_ref[head]
            if blk == 0:
                sg = jnp.where((col < BLOCK) & (n == 0), NEG_BIG, sg)
            if blk == nb - 1:
                sg = jnp.where((col >= 2 * BLOCK) & (n == nstep - 1), NEG_BIG, sg)
            p, r = _softmax_parts(sg, sink_ref[head])
            ps.append(p)
            rs.append(r)
        return jnp.concatenate(ps, axis=0), rs

    def pv(blk, j, p, rs):
        o = jnp.dot(p, vext[blk * BLOCK:(blk + 3) * BLOCK], preferred_element_type=jnp.float32)
        out = jnp.where(lane < HEAD_DIM, o[:BLOCK] * rs[0], o[BLOCK:] * rs[1])
        o_ref[0, blk * BLOCK:(blk + 1) * BLOCK, j * LANES:(j + 1) * LANES] = out.astype(jnp.bfloat16)

    s, pr = {}, {}
    for t in range(len(units) + 2):
        if t < len(units):
            s[t] = scores(*units[t])
        if 0 <= t - 2 < len(units):
            pv(*units[t - 2], *pr.pop(t - 2))
        if 0 <= t - 1 < len(units):
            pr[t - 1] = softmax(*units[t - 1], s.pop(t - 1))


def _win_attention(q, k, v, sink, wbias, tw):
    b, seq, _ = q.shape
    nblk = seq // BLOCK
    nb = tw // BLOCK
    prev = lambda bi, n: (bi, jnp.maximum(n * nb - 1, 0), 0)
    cur = lambda bi, n: (bi, n, 0)
    nxt = lambda bi, n: (bi, jnp.minimum((n + 1) * nb, nblk - 1), 0)
    halo = lambda f: pl.BlockSpec((1, BLOCK, LANES), f)
    body = pl.BlockSpec((1, tw, LANES), cur)
    return pl.pallas_call(
        functools.partial(_win_kernel, nstep=seq // tw, nb=nb),
        grid=(b, seq // tw),
        in_specs=[pl.BlockSpec(memory_space=pltpu.SMEM),
                  pl.BlockSpec((1, tw, GROUP_WIDTH), cur),
                  halo(prev), body, halo(nxt), halo(prev), body, halo(nxt),
                  pl.BlockSpec((4, BLOCK, 3 * BLOCK), lambda bi, n: (0, 0, 0))],
        out_specs=pl.BlockSpec((1, tw, GROUP_WIDTH), cur),
        out_shape=jax.ShapeDtypeStruct((b, seq, GROUP_WIDTH), jnp.bfloat16),
        compiler_params=_cparams(("parallel", "arbitrary")),
        name="window_attention",
    )(sink, q, k, k, k, v, v, v, wbias)


def _outproj_kernel(x_ref, a_ref, b_ref, c_ref, d_ref, w_ref, g_ref, o_ref):
    tm = x_ref.shape[0]
    halves = [slice(0, tm // 2), slice(tm // 2, tm)]

    def project(r):
        mixed = jnp.concatenate([a_ref[r, :], b_ref[r, :], c_ref[r, :], d_ref[r, :]], axis=1)
        return jnp.dot(mixed, w_ref[...], preferred_element_type=jnp.float32)

    ys = [project(r) for r in halves]
    for r, y in zip(halves, ys):
        o_ref[r, :] = x_ref[r, :] + _rms(y, g_ref[...])


def _outproj(x2, oa, ob, oc, od, w, g, tm):
    n = x2.shape[0]
    row = lambda wd: pl.BlockSpec((tm, wd), lambda i: (i, 0))
    return pl.pallas_call(
        _outproj_kernel,
        grid=(n // tm,),
        in_specs=[row(D_MODEL), row(256), row(256), row(256), row(256),
                  pl.BlockSpec(w.shape, lambda i: (0, 0)),
                  pl.BlockSpec(g.shape, lambda i: (0, 0))],
        out_specs=row(D_MODEL),
        out_shape=jax.ShapeDtypeStruct((n, D_MODEL), jnp.float32),
        compiler_params=_cparams(("parallel",)),
        name="out_projection",
    )(x2, oa, ob, oc, od, w, g)


HALO = 8
DOWN_GROUP_ENDS = (5, 10, 11)


def _ffn_kernel(x_ref, xp_ref, xn_ref, gpre_ref, wg_ref, wv_ref, cw_ref, cb_ref, wdn_ref, gpost_ref,
                o_ref, h_ref, *, tm, fc, tiles_per_seq):
    i = pl.program_id(0)
    bf = jnp.bfloat16
    g = gpre_ref[...]
    first = (i % tiles_per_seq) == 0
    last = (i % tiles_per_seq) == tiles_per_seq - 1
    h_ref[0:HALO, :] = jnp.where(first, 0.0, _rms(xp_ref[...], g)).astype(bf)
    h_ref[HALO:HALO + tm, :] = _rms(x_ref[...], g).astype(bf)
    h_ref[HALO + tm:, :] = jnp.where(last, 0.0, _rms(xn_ref[...], g)).astype(bf)
    nf = D_FF // fc
    chunk = lambda c: slice(c * fc, (c + 1) * fc)

    def up(c):
        gate = jnp.dot(h_ref[...], wg_ref[:, chunk(c)], preferred_element_type=jnp.float32)
        val = jnp.dot(h_ref[HALO:HALO + tm, :], wv_ref[:, chunk(c)], preferred_element_type=jnp.float32)
        return gate, val

    y = None
    acts, group_start = [], 0
    nxt = up(0)
    for c in range(nf):
        cs = chunk(c)
        gate, val = nxt
        if c + 1 < nf:
            nxt = up(c + 1)
        gate = (gate[HALO - 1:HALO - 1 + tm] * cw_ref[0:1, cs] + gate[HALO:HALO + tm] * cw_ref[1:2, cs]
                + gate[HALO + 1:HALO + 1 + tm] * cw_ref[2:3, cs] + cb_ref[:, cs])
        acts.append((0.5 * gate * (1.0 + lax.erf(gate * (2.0 ** -0.5))) * val).astype(bf))
        if c + 1 in DOWN_GROUP_ENDS:
            rows = slice(group_start * fc, (c + 1) * fc)
            part = jnp.dot(jnp.concatenate(acts, axis=1), wdn_ref[rows, :],
                           preferred_element_type=jnp.float32)
            y = part if y is None else y + part
            acts, group_start = [], c + 1
    o_ref[...] = x_ref[...] + _rms(y, gpost_ref[...])


def _ffn(x2, gpre, wg, wv, cw, cb, wdn, gpost, seq, tm, fc):
    n = x2.shape[0]
    tph = tm // HALO
    nh = n // HALO
    resident = lambda a: pl.BlockSpec(a.shape, lambda i: (0,) * a.ndim,
                                      pipeline_mode=pl.Buffered(1))
    return pl.pallas_call(
        functools.partial(_ffn_kernel, tm=tm, fc=fc, tiles_per_seq=seq // tm),
        grid=(n // tm,),
        in_specs=[pl.BlockSpec((tm, D_MODEL), lambda i: (i, 0)),
                  pl.BlockSpec((HALO, D_MODEL), lambda i: (jnp.maximum(i * tph - 1, 0), 0)),
                  pl.BlockSpec((HALO, D_MODEL), lambda i: (jnp.minimum((i + 1) * tph, nh - 1), 0)),
                  resident(gpre), resident(wg), resident(wv), resident(cw), resident(cb),
                  resident(wdn), resident(gpost)],
        out_specs=pl.BlockSpec((tm, D_MODEL), lambda i: (i, 0)),
        out_shape=jax.ShapeDtypeStruct((n, D_MODEL), jnp.float32),
        scratch_shapes=[pltpu.VMEM((tm + 2 * HALO, D_MODEL), jnp.bfloat16)],
        compiler_params=_cparams(("parallel",)),
        name="conv_glu",
    )(x2, x2, x2, gpre, wg, wv, cw, cb, wdn, gpost)


_GQA_HEAD_ORDER = (0, 2, 1, 3)


def _reorder_heads(w, order):
    lead = w.shape[:-1]
    nh = w.shape[-1] // HEAD_DIM
    w = w.reshape(lead + (nh, HEAD_DIM))
    return jnp.stack([w[..., o, :] for o in order], axis=-2).reshape(lead + (nh * HEAD_DIM,))


def _axial_perm(w):
    lead = w.shape[:-1]
    nh = w.shape[-1] // HEAD_DIM
    w = w.reshape(lead + (nh, 2, 2, 16))
    return jnp.swapaxes(w, -3, -2).reshape(lead + (nh * HEAD_DIM,))


def _prep_w_in(w_in):
    w_in = w_in.astype(jnp.bfloat16)
    bq = _reorder_heads(w_in[..., 768:1024], _GQA_HEAD_ORDER)
    cq = _axial_perm(_reorder_heads(w_in[..., 1280:1536], _GQA_HEAD_ORDER))
    ck = _axial_perm(w_in[..., 1536:1664])
    kr = w_in[..., 2176:2208]
    z = lambda wd: jnp.zeros(w_in.shape[:-1] + (wd,), w_in.dtype)
    return jnp.concatenate(
        [w_in[..., 0:768], bq, w_in[..., 1024:1280], cq, ck, w_in[..., 1664:1792],
         w_in[..., 1792:2176], z(MLA_NOPE), kr, z(LANES - MLA_NOPE - MLA_ROPE)], axis=-1)


def _rope_tables(cos, sin, lead_ones, trail_ones):
    s = cos.shape[0]
    one = lambda wd: jnp.ones((s, wd), jnp.float32)
    zero = lambda wd: jnp.zeros((s, wd), jnp.float32)
    c = jnp.concatenate([one(lead_ones), cos, cos, one(trail_ones)], axis=1)
    sn = jnp.concatenate([zero(lead_ones), -sin, zero(cos.shape[1]), zero(trail_ones)], axis=1)
    sp = jnp.concatenate([zero(lead_ones), zero(cos.shape[1]), sin, zero(trail_ones)], axis=1)
    return jnp.stack([c, sn, sp])


def _angles(pos, dim):
    inv = ROPE_THETA ** (-jnp.arange(0, dim, 2, dtype=jnp.float32) / dim)
    ang = pos.astype(jnp.float32)[:, None] * inv[None, :]
    return jnp.cos(ang), jnp.sin(ang)


def _forward(x, rel_bias, attn_pre_norm, w_in, diff_lambda, diff_subln, win_sink, ax_q_norm, ax_k_norm,
             mla_q_norm, mla_kv_norm, mla_w_uq, mla_w_ukv, w_out, attn_post_norm, ffn_pre_norm,
             ffn_w_up, ffn_conv_w, ffn_conv_b, ffn_w_down, ffn_post_norm, *, tq_a, tq, cq, tw, tm, tf, fc):
    bsz, seq, _ = x.shape
    depth = w_in.shape[0]
    n = bsz * seq
    bf = jnp.bfloat16

    pos = jnp.arange(seq, dtype=jnp.int32)
    rows = seq // GRID_W
    row_ids = jnp.repeat(jnp.arange(rows, dtype=jnp.int32), GRID_W)
    col_ids = jnp.tile(jnp.arange(GRID_W, dtype=jnp.int32), rows)
    rcos, rsin = _angles(row_ids, HEAD_DIM // 2)
    ccos, csin = _angles(col_ids, HEAD_DIM // 2)
    ctab = _rope_tables(jnp.concatenate([rcos, ccos], 1), jnp.concatenate([rsin, csin], 1), 0, 0)
    ctab = jnp.tile(ctab, (1, 1, 4))
    mcos, msin = _angles(pos, MLA_ROPE)
    dtab = _rope_tables(mcos, msin, MLA_NOPE, LANES - MLA_NOPE - MLA_ROPE)

    rel_a = (jnp.arange(tq_a + 2 * NEAR, dtype=jnp.int32)[:, None] - NEAR
             - jnp.arange(tq_a, dtype=jnp.int32)[None, :])
    gt = _bias_tiles(rel_bias, _bucket(rel_a), 0, 4, LOG2E)
    rel_b = jnp.arange(3 * BLOCK, dtype=jnp.int32)[None, :] - BLOCK - jnp.arange(BLOCK, dtype=jnp.int32)[:, None]
    idx_b = jnp.where(jnp.abs(rel_b) <= WINDOW, _bucket(rel_b), -1)
    wbias = _bias_tiles(rel_bias, idx_b, 4, 4, 1.0)

    w_in_p = _prep_w_in(w_in)
    cq_g = jnp.tile(_axial_perm(ax_q_norm), (1, 4))[:, None, :]
    ck_g = jnp.tile(_axial_perm(ax_k_norm), (1, 2))[:, None, :]
    wuq = jnp.pad(mla_w_uq.reshape(depth, MLA_Q_RANK, 4, MLA_NOPE + MLA_ROPE),
                  ((0, 0), (0, 0), (0, 0), (0, LANES - MLA_NOPE - MLA_ROPE))
                  ).reshape(depth, MLA_Q_RANK, 4 * LANES).astype(bf)
    wukv = mla_w_ukv.reshape(depth, MLA_KV_RANK, 4, 2, HEAD_DIM)
    wukvk = jnp.pad(wukv[:, :, :, 0], ((0, 0), (0, 0), (0, 0), (0, HEAD_DIM))
                    ).reshape(depth, MLA_KV_RANK, 4 * LANES).astype(bf)
    wukvv = wukv[:, :, :, 1].reshape(depth, MLA_KV_RANK, GROUP_WIDTH).astype(bf)
    wo = w_out.astype(bf).reshape(depth, 4, 4, HEAD_DIM, D_MODEL)
    slots = [(0, h) for h in range(4)] + [(1, h) for h in _GQA_HEAD_ORDER] \
        + [(2, h) for h in _GQA_HEAD_ORDER] + [(3, h) for h in range(4)]
    wo = jnp.stack([wo[:, m, h] for m, h in slots], axis=1).reshape(depth, D_MODEL, D_MODEL)
    wg = ffn_w_up[:, :, :D_FF].astype(bf)
    wv = ffn_w_up[:, :, D_FF:].astype(bf)
    wdn = ffn_w_down.astype(bf)
    sub = jnp.tile(diff_subln, (1, 2))[:, None, :]

    x2 = x.reshape(n, D_MODEL)
    r3 = lambda a: a.reshape(bsz, seq, a.shape[-1])
    for l in range(depth):
        lam_init = 0.8 - 0.6 * math.exp(-0.3 * l)
        (aq, ak, avt, bq, bk, bv, cq_, ck, cvt, dq, dk, dvt) = _pre(
            x2, attn_pre_norm[l][None], w_in_p[l], cq_g[l], ck_g[l], ctab,
            mla_q_norm[l][None], mla_kv_norm[l][None], wuq[l], wukvk[l], wukvv[l], dtab, seq, tm)
        oa = _diff_attention(r3(aq), r3(ak), avt, gt, diff_lambda[l][None], sub[l], lam_init, tq_a)
        ob = _win_attention(r3(bq), r3(bk), r3(bv), win_sink[l], wbias, tw)
        oc = _gqa_attention(r3(cq_), r3(ck), cvt, tq, cq)
        od = _mla_attention(r3(dq), r3(dk), dvt, tq, cq)
        x2 = _outproj(x2, oa.reshape(n, -1), ob.reshape(n, -1), oc.reshape(n, -1),
                      od.reshape(n, -1), wo[l], attn_post_norm[l][None], tm)
        x2 = _ffn(x2, ffn_pre_norm[l][None], wg[l], wv[l], ffn_conv_w[l], ffn_conv_b[l][None],
                  wdn[l], ffn_post_norm[l][None], seq, tf, fc)
    return x2.reshape(bsz, seq, D_MODEL)


def kernel(x, rel_bias, attn_pre_norm, w_in, diff_lambda, diff_subln, win_sink, ax_q_norm, ax_k_norm,
           mla_q_norm, mla_kv_norm, mla_w_uq, mla_w_ukv, w_out, attn_post_norm, ffn_pre_norm,
           ffn_w_up, ffn_conv_w, ffn_conv_b, ffn_w_down, ffn_post_norm):
    return _forward(x, rel_bias, attn_pre_norm, w_in, diff_lambda, diff_subln, win_sink, ax_q_norm,
                    ax_k_norm, mla_q_norm, mla_kv_norm, mla_w_uq, mla_w_ukv, w_out, attn_post_norm,
                    ffn_pre_norm, ffn_w_up, ffn_conv_w, ffn_conv_b, ffn_w_down, ffn_post_norm,
                    tq_a=256, tq=512, cq=256, tw=512, tm=512, tf=512, fc=256)
```

```python
import functools
import math

import jax
import jax.numpy as jnp
from jax import lax
from jax.experimental import pallas as pl
from jax.experimental.pallas import tpu as pltpu

D_MODEL = 1024
HEAD_DIM = 64
GROUP_WIDTH = 256
BLOCK = 128
DIFF_HALF = 32
WINDOW = 128
GRID_W = 64
ROPE_THETA = 10000.0
MLA_Q_RANK = 256
MLA_KV_RANK = 128
MLA_NOPE = 64
MLA_ROPE = 32
REL_BUCKETS = 32
REL_MAX_DIST = 128
D_FF = 2816
EPS = 1e-6
LANES = 128
LOG2E = 1.4426950408889634
NEG_BIG = -1e30
VMEM_LIMIT = 56 * 1024 * 1024
NEAR = 2 * BLOCK


def _cparams(sem):
    return pltpu.CompilerParams(dimension_semantics=sem, vmem_limit_bytes=VMEM_LIMIT)


def _rms(x, g):
    return x * lax.rsqrt(jnp.mean(x * x, axis=-1, keepdims=True) + EPS) * g


def _group_rms(x, ngroups):
    grp = lax.broadcasted_iota(jnp.int32, x.shape, 1) // HEAD_DIM
    x2 = x * x
    inv = jnp.zeros_like(x)
    for h in range(ngroups):
        m = grp == h
        ss = jnp.sum(jnp.where(m, x2, 0.0), axis=-1, keepdims=True)
        inv = jnp.where(m, lax.rsqrt(ss * (1.0 / HEAD_DIM) + EPS), inv)
    return x * inv


def _rope(x, c, sn, sp, half):
    w = x.shape[-1]
    return x * c + pltpu.roll(x, w - half, 1) * sn + pltpu.roll(x, half, 1) * sp


def _bias_kernel(tab_ref, idx_ref, o_ref, *, head0, scale):
    h = pl.program_id(0)
    idx = idx_ref[...]
    acc = jnp.zeros(idx.shape, jnp.float32)
    for b in range(REL_BUCKETS):
        acc = jnp.where(idx == b, tab_ref[b, head0 + h], acc)
    o_ref[0] = jnp.where(idx < 0, NEG_BIG, acc * scale)


def _bias_tiles(rel_bias, idx, head0, nheads, scale):
    r, c = idx.shape
    return pl.pallas_call(
        functools.partial(_bias_kernel, head0=head0, scale=scale),
        grid=(nheads,),
        in_specs=[pl.BlockSpec(memory_space=pltpu.SMEM),
                  pl.BlockSpec((r, c), lambda h: (0, 0))],
        out_specs=pl.BlockSpec((1, r, c), lambda h: (h, 0, 0)),
        out_shape=jax.ShapeDtypeStruct((nheads, r, c), jnp.float32),
        compiler_params=_cparams(("arbitrary",)),
        name="rel_bias_tiles",
    )(rel_bias, idx)


def _bucket(rel):
    half = REL_BUCKETS // 2
    max_exact = half // 2
    n = jnp.abs(rel)
    nf = jnp.maximum(n, 1).astype(jnp.float32)
    large = max_exact + (jnp.log(nf / max_exact) / math.log(REL_MAX_DIST / max_exact)
                         * (half - max_exact)).astype(jnp.int32)
    large = jnp.minimum(large, half - 1)
    return jnp.where(rel > 0, half, 0) + jnp.where(n < max_exact, n, large)


def _pre_kernel(x_ref, g_ref, w_ref, cq_g_ref, ck_g_ref, ctab_ref, dq_g_ref, dkv_g_ref,
                wuq_ref, wukvk_ref, wukvv_ref, dtab_ref,
                aq_ref, ak_ref, avt_ref, bq_ref, bk_ref, bv_ref, cq_ref, ck_ref, cvt_ref,
                dq_ref, dk_ref, dvt_ref):
    bf = jnp.bfloat16
    tm = x_ref.shape[0]
    rep = lambda t: jnp.concatenate([t] * 4, axis=1)

    def epi_c(r, p):
        cc, csn, csp = ctab_ref[0, r], ctab_ref[1, r], ctab_ref[2, r]
        q = _group_rms(p[:, 0:256], 4) * cq_g_ref[...]
        cq_ref[r, :] = (_rope(q, cc, csn, csp, 32) * (HEAD_DIM ** -0.5 * LOG2E)).astype(bf)
        k = _group_rms(p[:, 256:384], 2) * ck_g_ref[...]
        ck_ref[r, :] = _rope(k, cc[:, :128], csn[:, :128], csp[:, :128], 32).astype(bf)
        cvt_ref[:, r] = p[:, 384:512].T.astype(bf)

    def epi_d(r, p):
        dc, dsn, dsp = dtab_ref[0, r], dtab_ref[1, r], dtab_ref[2, r]
        cq = _rms(p[:, 0:256], dq_g_ref[...]).astype(bf)
        q = jnp.dot(cq, wuq_ref[...], preferred_element_type=jnp.float32)
        q = _rope(q, rep(dc), rep(dsn), rep(dsp), 16)
        dq_ref[r, :] = (q * ((MLA_NOPE + MLA_ROPE) ** -0.5 * LOG2E)).astype(bf)
        ckv = _rms(p[:, 256:384], dkv_g_ref[...]).astype(bf)
        kr = _rope(p[:, 384:512], dc, dsn, dsp, 16)
        kn = jnp.dot(ckv, wukvk_ref[...], preferred_element_type=jnp.float32)
        dk_ref[r, :] = (kn + rep(kr)).astype(bf)
        dvt_ref[:, r] = jnp.dot(ckv, wukvv_ref[...], preferred_element_type=jnp.float32).T.astype(bf)

    def epi_a(r, p):
        aq_ref[r, :] = (p[:, 0:256] * (DIFF_HALF ** -0.5 * LOG2E)).astype(bf)
        ak_ref[r, :] = p[:, 256:512].astype(bf)
        avt_ref[:, r] = p[:, 512:768].T.astype(bf)

    def epi_b(r, p):
        bq_ref[r, :] = (p[:, 0:256] * 0.125).astype(bf)
        bk_ref[r, :] = p[:, 256:384].astype(bf)
        bv_ref[r, :] = p[:, 384:512].astype(bf)

    mixers = [((1280, 1792), epi_c), ((1792, 2304), epi_d), ((0, 768), epi_a), ((768, 1280), epi_b)]
    halves = [slice(0, tm // 2), slice(tm // 2, tm)]
    pending = None
    for r in halves:
        h = _rms(x_ref[r, :], g_ref[...]).astype(bf)
        for (lo, hi), epilogue in mixers:
            p = jnp.dot(h, w_ref[:, lo:hi], preferred_element_type=jnp.float32)
            if pending is not None:
                pending[0](pending[1], pending[2])
            pending = (epilogue, r, p)
    pending[0](pending[1], pending[2])


def _pre(x2, g, w, cq_g, ck_g, ctab, dq_g, dkv_g, wuq, wukvk, wukvv, dtab, seq, tm):
    n = x2.shape[0]
    nt = seq // tm
    row = lambda wd: pl.BlockSpec((tm, wd), lambda i: (i, 0))
    col = lambda wd: pl.BlockSpec((wd, tm), lambda i: (0, i))
    full = lambda a: pl.BlockSpec(a.shape, lambda i: (0,) * a.ndim)
    bf = jnp.bfloat16
    rows = lambda wd: (row(wd), jax.ShapeDtypeStruct((n, wd), bf))
    cols = lambda wd: (col(wd), jax.ShapeDtypeStruct((wd, n), bf))
    outs = [rows(256), rows(256), cols(256), rows(256), rows(128), rows(128),
            rows(256), rows(128), cols(128), rows(512), rows(512), cols(256)]
    return pl.pallas_call(
        _pre_kernel,
        grid=(n // tm,),
        in_specs=[row(D_MODEL), full(g), full(w), full(cq_g), full(ck_g),
                  pl.BlockSpec((3, tm, 256), lambda i: (0, i % nt, 0)),
                  full(dq_g), full(dkv_g), full(wuq), full(wukvk), full(wukvv),
                  pl.BlockSpec((3, tm, 128), lambda i: (0, i % nt, 0))],
        out_specs=[o[0] for o in outs],
        out_shape=[o[1] for o in outs],
        compiler_params=_cparams(("parallel",)),
        name="pre_attention",
    )(x2, g, w, cq_g, ck_g, ctab, dq_g, dkv_g, wuq, wukvk, wukvv, dtab)


_NT = (((1,), (1,)), ((), ()))


FLASH_KEYS = 512
FLASH_LAG = 4


def _attend_flash(chains):
    n = len(chains)
    nblk = chains[0][1].shape[0] // FLASH_KEYS
    fold = lambda a: a.reshape(a.shape[0] // 8, 8, a.shape[1])
    state = [None] * n

    def scores(b, c):
        q, k, _, bias = chains[c]
        rows = slice(b * FLASH_KEYS, (b + 1) * FLASH_KEYS)
        s = lax.dot_general(k[rows], q, _NT, preferred_element_type=jnp.float32)
        if bias is not None:
            per = FLASH_KEYS // BLOCK
            s = s + jnp.concatenate([bias(b * per + j) for j in range(per)], axis=0)
        return s

    def update(b, c, s):
        vt = chains[c][2][:, b * FLASH_KEYS:(b + 1) * FLASH_KEYS]
        mb = jnp.max(jnp.max(fold(s), axis=0), axis=0, keepdims=True)
        if state[c] is None:
            m = mb
            p = jnp.exp2(s - m)
            l = jnp.sum(jnp.sum(fold(p), axis=0), axis=0, keepdims=True)
            acc = jnp.dot(vt, p.astype(jnp.bfloat16), preferred_element_type=jnp.float32)
        else:
            m0, l0, acc0 = state[c]
            m = jnp.maximum(m0, mb)
            alpha = jnp.exp2(m0 - m)
            p = jnp.exp2(s - m)
            l = alpha * l0 + jnp.sum(jnp.sum(fold(p), axis=0), axis=0, keepdims=True)
            acc = alpha * acc0 + jnp.dot(vt, p.astype(jnp.bfloat16), preferred_element_type=jnp.float32)
        state[c] = (m, l, acc)

    units = [(b, c) for b in range(nblk) for c in range(n)]
    pending = []
    for u in units:
        pending.append((*u, scores(*u)))
        if len(pending) > FLASH_LAG:
            update(*pending.pop(0))
    for item in pending:
        update(*item)
    return [acc * (1.0 / l) for _, l, acc in state]


def _diff_kernel(q_ref, k_ref, vt_ref, gt_ref, dl_ref, sub_ref, o_ref, *, lam_init, seq, tq, cq):
    i = pl.program_id(1)
    lane = lax.broadcasted_iota(jnp.int32, (cq, LANES), 1)
    upper_rows = lax.broadcasted_iota(jnp.int32, (LANES, cq), 0) >= HEAD_DIM
    dl = dl_ref[0]
    lam = (jnp.exp(jnp.sum(dl[0:1] * dl[1:2], axis=-1, keepdims=True))
           - jnp.exp(jnp.sum(dl[2:3] * dl[3:4], axis=-1, keepdims=True)) + lam_init)
    chains = []
    for r0 in range(0, tq, cq):
        offs = [pl.multiple_of(jnp.clip(jb * BLOCK - (i * tq + r0), -NEAR, cq + BLOCK) + NEAR, BLOCK)
                for jb in range(seq // BLOCK)]
        for pr in range(2):
            pl_ = slice(pr * LANES, (pr + 1) * LANES)
            q, k, vt = q_ref[0, r0:r0 + cq, pl_], k_ref[0, :, pl_], vt_ref[pl_, :]
            for hh in range(2):
                bias_t = lambda jb, h=2 * pr + hh, offs=offs: gt_ref[h, pl.ds(offs[jb], BLOCK), :]
                for c in range(2):
                    lo = hh * HEAD_DIM + c * DIFF_HALF
                    qm = jnp.where((lane >= lo) & (lane < lo + DIFF_HALF), q, jnp.zeros_like(q))
                    chains.append((qm, k, vt, bias_t))
    outs = _attend_flash(chains)
    for n, r0 in enumerate(range(0, tq, cq)):
        for pr in range(2):
            heads = []
            for hh in range(2):
                first = 8 * n + 4 * pr + 2 * hh
                oh = outs[first] - lam * outs[first + 1]
                mh = upper_rows if hh else jnp.logical_not(upper_rows)
                ss = jnp.sum(jnp.where(mh, oh * oh, 0.0), axis=0, keepdims=True) * (1.0 / HEAD_DIM)
                heads.append(oh * lax.rsqrt(ss + EPS))
            out = jnp.where(upper_rows, heads[1], heads[0]).T * sub_ref[...] * (1.0 - lam_init)
            o_ref[0, r0:r0 + cq, pr * LANES:(pr + 1) * LANES] = out.astype(jnp.bfloat16)


def _diff_attention(q, k, vt, gt, dl, sub, lam_init, tq, cq):
    b, seq, _ = q.shape
    return pl.pallas_call(
        functools.partial(_diff_kernel, lam_init=lam_init, seq=seq, tq=tq, cq=cq),
        grid=(b, seq // tq),
        in_specs=[pl.BlockSpec((1, tq, GROUP_WIDTH), lambda bi, i: (bi, i, 0)),
                  pl.BlockSpec((1, seq, GROUP_WIDTH), lambda bi, i: (bi, 0, 0)),
                  pl.BlockSpec((GROUP_WIDTH, seq), lambda bi, i: (0, bi)),
                  pl.BlockSpec(gt.shape, lambda bi, i: (0, 0, 0)),
                  pl.BlockSpec((1, 4, DIFF_HALF), lambda bi, i: (0, 0, 0)),
                  pl.BlockSpec((1, LANES), lambda bi, i: (0, 0))],
        out_specs=pl.BlockSpec((1, tq, GROUP_WIDTH), lambda bi, i: (bi, i, 0)),
        out_shape=jax.ShapeDtypeStruct((b, seq, GROUP_WIDTH), jnp.bfloat16),
        compiler_params=_cparams(("parallel", "arbitrary")),
        name="diff_attention",
    )(q, k, vt, gt, dl, sub)


def _gqa_kernel(q_ref, k_ref, vt_ref, o_ref, *, tq, cq):
    k, vt = k_ref[0], vt_ref[...]
    lower = lax.broadcasted_iota(jnp.int32, (cq, LANES), 1) < HEAD_DIM
    upper_rows = lax.broadcasted_iota(jnp.int32, (LANES, cq), 0) >= HEAD_DIM
    chains = []
    for r0 in range(0, tq, cq):
        for j in range(2):
            q = q_ref[0, r0:r0 + cq, j * LANES:(j + 1) * LANES]
            zero = jnp.zeros_like(q)
            chains.append((jnp.where(lower, q, zero), k, vt, None))
            chains.append((jnp.where(lower, zero, q), k, vt, None))
    outs = _attend_flash(chains)
    for n, r0 in enumerate(range(0, tq, cq)):
        for j in range(2):
            lo, hi = outs[4 * n + 2 * j], outs[4 * n + 2 * j + 1]
            o_ref[0, r0:r0 + cq, j * LANES:(j + 1) * LANES] = jnp.where(
                upper_rows, hi, lo).T.astype(jnp.bfloat16)


def _gqa_attention(q, k, vt, tq, cq):
    b, seq, _ = q.shape
    return pl.pallas_call(
        functools.partial(_gqa_kernel, tq=tq, cq=cq),
        grid=(b, seq // tq),
        in_specs=[pl.BlockSpec((1, tq, GROUP_WIDTH), lambda bi, i: (bi, i, 0)),
                  pl.BlockSpec((1, seq, LANES), lambda bi, i: (bi, 0, 0)),
                  pl.BlockSpec((LANES, seq), lambda bi, i: (0, bi))],
        out_specs=pl.BlockSpec((1, tq, GROUP_WIDTH), lambda bi, i: (bi, i, 0)),
        out_shape=jax.ShapeDtypeStruct((b, seq, GROUP_WIDTH), jnp.bfloat16),
        compiler_params=_cparams(("parallel", "arbitrary")),
        name="axial_gqa_attention",
    )(q, k, vt)


def _mla_kernel(q_ref, k_ref, vt_ref, o_ref, *, tq, cq):
    upper_rows = lax.broadcasted_iota(jnp.int32, (LANES, cq), 0) >= HEAD_DIM
    chains = []
    for r0 in range(0, tq, cq):
        for hh in range(4):
            hl = slice(hh * LANES, (hh + 1) * LANES)
            vl = slice((hh // 2) * LANES, (hh // 2 + 1) * LANES)
            chains.append((q_ref[0, r0:r0 + cq, hl], k_ref[0, :, hl], vt_ref[vl, :], None))
    outs = _attend_flash(chains)
    for n, r0 in enumerate(range(0, tq, cq)):
        for pr in range(2):
            lo, hi = outs[4 * n + 2 * pr], outs[4 * n + 2 * pr + 1]
            o_ref[0, r0:r0 + cq, pr * LANES:(pr + 1) * LANES] = jnp.where(
                upper_rows, hi, lo).T.astype(jnp.bfloat16)


def _mla_attention(q, k, vt, tq, cq):
    b, seq, _ = q.shape
    return pl.pallas_call(
        functools.partial(_mla_kernel, tq=tq, cq=cq),
        grid=(b, seq // tq),
        in_specs=[pl.BlockSpec((1, tq, 4 * LANES), lambda bi, i: (bi, i, 0)),
                  pl.BlockSpec((1, seq, 4 * LANES), lambda bi, i: (bi, 0, 0)),
                  pl.BlockSpec((GROUP_WIDTH, seq), lambda bi, i: (0, bi))],
        out_specs=pl.BlockSpec((1, tq, GROUP_WIDTH), lambda bi, i: (bi, i, 0)),
        out_shape=jax.ShapeDtypeStruct((b, seq, GROUP_WIDTH), jnp.bfloat16),
        compiler_params=_cparams(("parallel", "arbitrary")),
        name="latent_attention",
    )(q, k, vt)


def _softmax_parts(s, extra):
    m = jnp.maximum(jnp.max(s, axis=-1, keepdims=True), extra)
    p = jnp.exp(s - m)
    l = jnp.sum(p, axis=-1, keepdims=True) + jnp.exp(extra - m)
    return p.astype(jnp.bfloat16), 1.0 / l


def _win_kernel(sink_ref, q_ref, kp_ref, kc_ref, kn_ref, vp_ref, vc_ref, vn_ref, wb_ref, o_ref,
                *, nstep, nb):
    n = pl.program_id(1)
    kext = jnp.concatenate([kp_ref[0], kc_ref[0], kn_ref[0]], axis=0)
    vext = jnp.concatenate([vp_ref[0], vc_ref[0], vn_ref[0]], axis=0)
    lane = lax.broadcasted_iota(jnp.int32, (BLOCK, LANES), 1)
    col = lax.broadcasted_iota(jnp.int32, (BLOCK, 3 * BLOCK), 1)
    units = [(blk, j) for blk in range(nb) for j in range(2)]

    def scores(blk, j):
        q = q_ref[0, blk * BLOCK:(blk + 1) * BLOCK, j * LANES:(j + 1) * LANES]
        zero = jnp.zeros_like(q)
        qs = jnp.concatenate([jnp.where(lane < HEAD_DIM, q, zero),
                              jnp.where(lane >= HEAD_DIM, q, zero)], axis=0)
        return lax.dot_general(qs, kext[blk * BLOCK:(blk + 3) * BLOCK], _NT,
                               preferred_element_type=jnp.float32)

    def softmax(blk, j, s):
        ps, rs = [], []
        for g in range(2):
            head = j + 2 * g
            sg = s[g * BLOCK:(g + 1) * BLOCK] + wb_ref[head]
            if blk == 0:
                sg = jnp.where((col < BLOCK) & (n == 0), NEG_BIG, sg)
            if blk == nb - 1:
                sg = jnp.where((col >= 2 * BLOCK) & (n == nstep - 1), NEG_BIG, sg)
            p, r = _softmax_parts(sg, sink_ref[head])
            ps.append(p)
            rs.append(r)
        return jnp.concatenate(ps, axis=0), rs

    def pv(blk, j, p, rs):
        o = jnp.dot(p, vext[blk * BLOCK:(blk + 3) * BLOCK], preferred_element_type=jnp.float32)
        out = jnp.where(lane < HEAD_DIM, o[:BLOCK] * rs[0], o[BLOCK:] * rs[1])
        o_ref[0, blk * BLOCK:(blk + 1) * BLOCK, j * LANES:(j + 1) * LANES] = out.astype(jnp.bfloat16)

    s, pr = {}, {}
    for t in range(len(units) + 2):
        if t < len(units):
            s[t] = scores(*units[t])
        if 0 <= t - 2 < len(units):
            pv(*units[t - 2], *pr.pop(t - 2))
        if 0 <= t - 1 < len(units):
            pr[t - 1] = softmax(*units[t - 1], s.pop(t - 1))


def _win_attention(q, k, v, sink, wbias, tw):
    b, seq, _ = q.shape
    nblk = seq // BLOCK
    nb = tw // BLOCK
    prev = lambda bi, n: (bi, jnp.maximum(n * nb - 1, 0), 0)
    cur = lambda bi, n: (bi, n, 0)
    nxt = lambda bi, n: (bi, jnp.minimum((n + 1) * nb, nblk - 1), 0)
    halo = lambda f: pl.BlockSpec((1, BLOCK, LANES), f)
    body = pl.BlockSpec((1, tw, LANES), cur)
    return pl.pallas_call(
        functools.partial(_win_kernel, nstep=seq // tw, nb=nb),
        grid=(b, seq // tw),
        in_specs=[pl.BlockSpec(memory_space=pltpu.SMEM),
                  pl.BlockSpec((1, tw, GROUP_WIDTH), cur),
                  halo(prev), body, halo(nxt), halo(prev), body, halo(nxt),
                  pl.BlockSpec((4, BLOCK, 3 * BLOCK), lambda bi, n: (0, 0, 0))],
        out_specs=pl.BlockSpec((1, tw, GROUP_WIDTH), cur),
        out_shape=jax.ShapeDtypeStruct((b, seq, GROUP_WIDTH), jnp.bfloat16),
        compiler_params=_cparams(("parallel", "arbitrary")),
        name="window_attention",
    )(sink, q, k, k, k, v, v, v, wbias)


def _outproj_kernel(x_ref, a_ref, b_ref, c_ref, d_ref, w_ref, g_ref, o_ref):
    tm = x_ref.shape[0]
    halves = [slice(0, tm // 2), slice(tm // 2, tm)]

    def project(r):
        mixed = jnp.concatenate([a_ref[r, :], b_ref[r, :], c_ref[r, :], d_ref[r, :]], axis=1)
        return jnp.dot(mixed, w_ref[...], preferred_element_type=jnp.float32)

    ys = [project(r) for r in halves]
    for r, y in zip(halves, ys):
        o_ref[r, :] = x_ref[r, :] + _rms(y, g_ref[...])


def _outproj(x2, oa, ob, oc, od, w, g, tm):
    n = x2.shape[0]
    row = lambda wd: pl.BlockSpec((tm, wd), lambda i: (i, 0))
    return pl.pallas_call(
        _outproj_kernel,
        grid=(n // tm,),
        in_specs=[row(D_MODEL), row(256), row(256), row(256), row(256),
                  pl.BlockSpec(w.shape, lambda i: (0, 0)),
                  pl.BlockSpec(g.shape, lambda i: (0, 0))],
        out_specs=row(D_MODEL),
        out_shape=jax.ShapeDtypeStruct((n, D_MODEL), jnp.float32),
        compiler_params=_cparams(("parallel",)),
        name="out_projection",
    )(x2, oa, ob, oc, od, w, g)


HALO = 8
DOWN_GROUP_ENDS = (5, 10, 11)


def _ffn_kernel(x_ref, xp_ref, xn_ref, gpre_ref, wup_ref, cw_ref, cb_ref, wdn_ref, gpost_ref,
                o_ref, h_ref, *, tm, fc, tiles_per_seq):
    i = pl.program_id(0)
    bf = jnp.bfloat16
    g = gpre_ref[...]
    first = (i % tiles_per_seq) == 0
    last = (i % tiles_per_seq) == tiles_per_seq - 1
    h_ref[0:HALO, :] = jnp.where(first, 0.0, _rms(xp_ref[...], g)).astype(bf)
    h_ref[HALO:HALO + tm, :] = _rms(x_ref[...], g).astype(bf)
    h_ref[HALO + tm:, :] = jnp.where(last, 0.0, _rms(xn_ref[...], g)).astype(bf)
    nf = D_FF // fc
    chunk = lambda c: slice(c * fc, (c + 1) * fc)

    def up(c):
        val_cols = slice(D_FF + c * fc, D_FF + (c + 1) * fc)
        gate = jnp.dot(h_ref[...], wup_ref[:, chunk(c)], preferred_element_type=jnp.float32)
        val = jnp.dot(h_ref[HALO:HALO + tm, :], wup_ref[:, val_cols], preferred_element_type=jnp.float32)
        return gate, val

    y = None
    acts, group_start = [], 0
    nxt = up(0)
    for c in range(nf):
        cs = chunk(c)
        gate, val = nxt
        if c + 1 < nf:
            nxt = up(c + 1)
        gate = (gate[HALO - 1:HALO - 1 + tm] * cw_ref[0:1, cs] + gate[HALO:HALO + tm] * cw_ref[1:2, cs]
                + gate[HALO + 1:HALO + 1 + tm] * cw_ref[2:3, cs] + cb_ref[:, cs])
        acts.append((0.5 * gate * (1.0 + lax.erf(gate * (2.0 ** -0.5))) * val).astype(bf))
        if c + 1 in DOWN_GROUP_ENDS:
            rows = slice(group_start * fc, (c + 1) * fc)
            part = jnp.dot(jnp.concatenate(acts, axis=1), wdn_ref[rows, :],
                           preferred_element_type=jnp.float32)
            y = part if y is None else y + part
            acts, group_start = [], c + 1
    o_ref[...] = x_ref[...] + _rms(y, gpost_ref[...])


def _ffn(x2, gpre, wup, cw, cb, wdn, gpost, seq, tm, fc):
    n = x2.shape[0]
    tph = tm // HALO
    nh = n // HALO
    resident = lambda a: pl.BlockSpec(a.shape, lambda i: (0,) * a.ndim,
                                      pipeline_mode=pl.Buffered(1))
    return pl.pallas_call(
        functools.partial(_ffn_kernel, tm=tm, fc=fc, tiles_per_seq=seq // tm),
        grid=(n // tm,),
        in_specs=[pl.BlockSpec((tm, D_MODEL), lambda i: (i, 0)),
                  pl.BlockSpec((HALO, D_MODEL), lambda i: (jnp.maximum(i * tph - 1, 0), 0)),
                  pl.BlockSpec((HALO, D_MODEL), lambda i: (jnp.minimum((i + 1) * tph, nh - 1), 0)),
                  resident(gpre), resident(wup), resident(cw), resident(cb),
                  resident(wdn), resident(gpost)],
        out_specs=pl.BlockSpec((tm, D_MODEL), lambda i: (i, 0)),
        out_shape=jax.ShapeDtypeStruct((n, D_MODEL), jnp.float32),
        scratch_shapes=[pltpu.VMEM((tm + 2 * HALO, D_MODEL), jnp.bfloat16)],
        compiler_params=_cparams(("parallel",)),
        name="conv_glu",
    )(x2, x2, x2, gpre, wup, cw, cb, wdn, gpost)


_GQA_HEAD_ORDER = (0, 2, 1, 3)


def _reorder_heads(w, order):
    lead = w.shape[:-1]
    nh = w.shape[-1] // HEAD_DIM
    w = w.reshape(lead + (nh, HEAD_DIM))
    return jnp.stack([w[..., o, :] for o in order], axis=-2).reshape(lead + (nh * HEAD_DIM,))


def _axial_perm(w):
    lead = w.shape[:-1]
    nh = w.shape[-1] // HEAD_DIM
    w = w.reshape(lead + (nh, 2, 2, 16))
    return jnp.swapaxes(w, -3, -2).reshape(lead + (nh * HEAD_DIM,))


def _prep_w_in(w_in):
    w_in = w_in.astype(jnp.bfloat16)
    bq = _reorder_heads(w_in[..., 768:1024], _GQA_HEAD_ORDER)
    cq = _axial_perm(_reorder_heads(w_in[..., 1280:1536], _GQA_HEAD_ORDER))
    ck = _axial_perm(w_in[..., 1536:1664])
    kr = w_in[..., 2176:2208]
    z = lambda wd: jnp.zeros(w_in.shape[:-1] + (wd,), w_in.dtype)
    return jnp.concatenate(
        [w_in[..., 0:768], bq, w_in[..., 1024:1280], cq, ck, w_in[..., 1664:1792],
         w_in[..., 1792:2176], z(MLA_NOPE), kr, z(LANES - MLA_NOPE - MLA_ROPE)], axis=-1)


def _rope_tables(cos, sin, lead_ones, trail_ones):
    s = cos.shape[0]
    one = lambda wd: jnp.ones((s, wd), jnp.float32)
    zero = lambda wd: jnp.zeros((s, wd), jnp.float32)
    c = jnp.concatenate([one(lead_ones), cos, cos, one(trail_ones)], axis=1)
    sn = jnp.concatenate([zero(lead_ones), -sin, zero(cos.shape[1]), zero(trail_ones)], axis=1)
    sp = jnp.concatenate([zero(lead_ones), zero(cos.shape[1]), sin, zero(trail_ones)], axis=1)
    return jnp.stack([c, sn, sp])


def _angles(pos, dim):
    inv = ROPE_THETA ** (-jnp.arange(0, dim, 2, dtype=jnp.float32) / dim)
    ang = pos.astype(jnp.float32)[:, None] * inv[None, :]
    return jnp.cos(ang), jnp.sin(ang)


def _forward(x, rel_bias, attn_pre_norm, w_in, diff_lambda, diff_subln, win_sink, ax_q_norm, ax_k_norm,
             mla_q_norm, mla_kv_norm, mla_w_uq, mla_w_ukv, w_out, attn_post_norm, ffn_pre_norm,
             ffn_w_up, ffn_conv_w, ffn_conv_b, ffn_w_down, ffn_post_norm, *, tq_a, tq, cq, tw, tm, tf, fc):
    bsz, seq, _ = x.shape
    depth = w_in.shape[0]
    n = bsz * seq
    bf = jnp.bfloat16

    pos = jnp.arange(seq, dtype=jnp.int32)
    rows = seq // GRID_W
    row_ids = jnp.repeat(jnp.arange(rows, dtype=jnp.int32), GRID_W)
    col_ids = jnp.tile(jnp.arange(GRID_W, dtype=jnp.int32), rows)
    rcos, rsin = _angles(row_ids, HEAD_DIM // 2)
    ccos, csin = _angles(col_ids, HEAD_DIM // 2)
    ctab = _rope_tables(jnp.concatenate([rcos, ccos], 1), jnp.concatenate([rsin, csin], 1), 0, 0)
    ctab = jnp.tile(ctab, (1, 1, 4))
    mcos, msin = _angles(pos, MLA_ROPE)
    dtab = _rope_tables(mcos, msin, MLA_NOPE, LANES - MLA_NOPE - MLA_ROPE)

    rel_a = (jnp.arange(cq + 2 * NEAR, dtype=jnp.int32)[:, None] - NEAR
             - jnp.arange(cq, dtype=jnp.int32)[None, :])
    gt = _bias_tiles(rel_bias, _bucket(rel_a), 0, 4, LOG2E)
    rel_b = jnp.arange(3 * BLOCK, dtype=jnp.int32)[None, :] - BLOCK - jnp.arange(BLOCK, dtype=jnp.int32)[:, None]
    idx_b = jnp.where(jnp.abs(rel_b) <= WINDOW, _bucket(rel_b), -1)
    wbias = _bias_tiles(rel_bias, idx_b, 4, 4, 1.0)

    w_in_p = _prep_w_in(w_in)
    cq_g = jnp.tile(_axial_perm(ax_q_norm), (1, 4))[:, None, :]
    ck_g = jnp.tile(_axial_perm(ax_k_norm), (1, 2))[:, None, :]
    wuq = jnp.pad(mla_w_uq.reshape(depth, MLA_Q_RANK, 4, MLA_NOPE + MLA_ROPE),
                  ((0, 0), (0, 0), (0, 0), (0, LANES - MLA_NOPE - MLA_ROPE))
                  ).reshape(depth, MLA_Q_RANK, 4 * LANES).astype(bf)
    wukv = mla_w_ukv.reshape(depth, MLA_KV_RANK, 4, 2, HEAD_DIM)
    wukvk = jnp.pad(wukv[:, :, :, 0], ((0, 0), (0, 0), (0, 0), (0, HEAD_DIM))
                    ).reshape(depth, MLA_KV_RANK, 4 * LANES).astype(bf)
    wukvv = wukv[:, :, :, 1].reshape(depth, MLA_KV_RANK, GROUP_WIDTH).astype(bf)
    wo = w_out.astype(bf).reshape(depth, 4, 4, HEAD_DIM, D_MODEL)
    slots = [(0, h) for h in range(4)] + [(1, h) for h in _GQA_HEAD_ORDER] \
        + [(2, h) for h in _GQA_HEAD_ORDER] + [(3, h) for h in range(4)]
    wo = jnp.stack([wo[:, m, h] for m, h in slots], axis=1).reshape(depth, D_MODEL, D_MODEL)
    wup = ffn_w_up.astype(bf)
    wdn = ffn_w_down.astype(bf)
    sub = jnp.tile(diff_subln, (1, 2))[:, None, :]

    x2 = x.reshape(n, D_MODEL)
    r3 = lambda a: a.reshape(bsz, seq, a.shape[-1])
    for l in range(depth):
        lam_init = 0.8 - 0.6 * math.exp(-0.3 * l)
        (aq, ak, avt, bq, bk, bv, cq_, ck, cvt, dq, dk, dvt) = _pre(
            x2, attn_pre_norm[l][None], w_in_p[l], cq_g[l], ck_g[l], ctab,
            mla_q_norm[l][None], mla_kv_norm[l][None], wuq[l], wukvk[l], wukvv[l], dtab, seq, tm)
        oa = _diff_attention(r3(aq), r3(ak), avt, gt, diff_lambda[l][None], sub[l], lam_init,
                             tq_a, cq)
        ob = _win_attention(r3(bq), r3(bk), r3(bv), win_sink[l], wbias, tw)
        oc = _gqa_attention(r3(cq_), r3(ck), cvt, tq, cq)
        od = _mla_attention(r3(dq), r3(dk), dvt, tq, cq)
        x2 = _outproj(x2, oa.reshape(n, -1), ob.reshape(n, -1), oc.reshape(n, -1),
                      od.reshape(n, -1), wo[l], attn_post_norm[l][None], tm)
        x2 = _ffn(x2, ffn_pre_norm[l][None], wup[l], ffn_conv_w[l], ffn_conv_b[l][None],
                  wdn[l], ffn_post_norm[l][None], seq, tf, fc)
    return x2.reshape(bsz, seq, D_MODEL)


def kernel(x, rel_bias, attn_pre_norm, w_in, diff_lambda, diff_subln, win_sink, ax_q_norm, ax_k_norm,
           mla_q_norm, mla_kv_norm, mla_w_uq, mla_w_ukv, w_out, attn_post_norm, ffn_pre_norm,
           ffn_w_up, ffn_conv_w, ffn_conv_b, ffn_w_down, ffn_post_norm):
    return _forward(x, rel_bias, attn_pre_norm, w_in, diff_lambda, diff_subln, win_sink, ax_q_norm,
                    ax_k_norm, mla_q_norm, mla_kv_norm, mla_w_uq, mla_w_ukv, w_out, attn_post_norm,
                    ffn_pre_norm, ffn_w_up, ffn_conv_w, ffn_conv_b, ffn_w_down, ffn_post_norm,
                    tq_a=512, tq=1024, cq=256, tw=512, tm=512, tf=512, fc=256)
```

```python
import functools
import math

import jax
import jax.numpy as jnp
from jax import lax
from jax.experimental import pallas as pl
from jax.experimental.pallas import tpu as pltpu

D_MODEL = 1024
HEAD_DIM = 64
GROUP_WIDTH = 256
BLOCK = 128
DIFF_HALF = 32
WINDOW = 128
GRID_W = 64
ROPE_THETA = 10000.0
MLA_Q_RANK = 256
MLA_KV_RANK = 128
MLA_NOPE = 64
MLA_ROPE = 32
REL_BUCKETS = 32
REL_MAX_DIST = 128
D_FF = 2816
EPS = 1e-6
LANES = 128
LOG2E = 1.4426950408889634
NEG_BIG = -1e30
VMEM_LIMIT = 56 * 1024 * 1024
NEAR = 2 * BLOCK


def _cparams(sem):
    return pltpu.CompilerParams(dimension_semantics=sem, vmem_limit_bytes=VMEM_LIMIT)


def _rms(x, g):
    return x * lax.rsqrt(jnp.mean(x * x, axis=-1, keepdims=True) + EPS) * g


def _group_rms(x, ngroups):
    grp = lax.broadcasted_iota(jnp.int32, x.shape, 1) // HEAD_DIM
    x2 = x * x
    inv = jnp.zeros_like(x)
    for h in range(ngroups):
        m = grp == h
        ss = jnp.sum(jnp.where(m, x2, 0.0), axis=-1, keepdims=True)
        inv = jnp.where(m, lax.rsqrt(ss * (1.0 / HEAD_DIM) + EPS), inv)
    return x * inv


def _rope(x, c, sn, sp, half):
    w = x.shape[-1]
    return x * c + pltpu.roll(x, w - half, 1) * sn + pltpu.roll(x, half, 1) * sp


def _bias_kernel(tab_ref, idx_ref, o_ref, *, head0, scale):
    h = pl.program_id(0)
    idx = idx_ref[...]
    acc = jnp.zeros(idx.shape, jnp.float32)
    for b in range(REL_BUCKETS):
        acc = jnp.where(idx == b, tab_ref[b, head0 + h], acc)
    o_ref[0] = jnp.where(idx < 0, NEG_BIG, acc * scale)


def _bias_tiles(rel_bias, idx, head0, nheads, scale):
    r, c = idx.shape
    return pl.pallas_call(
        functools.partial(_bias_kernel, head0=head0, scale=scale),
        grid=(nheads,),
        in_specs=[pl.BlockSpec(memory_space=pltpu.SMEM),
                  pl.BlockSpec((r, c), lambda h: (0, 0))],
        out_specs=pl.BlockSpec((1, r, c), lambda h: (h, 0, 0)),
        out_shape=jax.ShapeDtypeStruct((nheads, r, c), jnp.float32),
        compiler_params=_cparams(("arbitrary",)),
        name="rel_bias_tiles",
    )(rel_bias, idx)


def _bucket(rel):
    half = REL_BUCKETS // 2
    max_exact = half // 2
    n = jnp.abs(rel)
    nf = jnp.maximum(n, 1).astype(jnp.float32)
    large = max_exact + (jnp.log(nf / max_exact) / math.log(REL_MAX_DIST / max_exact)
                         * (half - max_exact)).astype(jnp.int32)
    large = jnp.minimum(large, half - 1)
    return jnp.where(rel > 0, half, 0) + jnp.where(n < max_exact, n, large)


def _pre_kernel(x_ref, g_ref, w_ref, cq_g_ref, ck_g_ref, ctab_ref, dq_g_ref, dkv_g_ref,
                wuq_ref, wukvk_ref, wukvv_ref, dtab_ref,
                aq_ref, ak_ref, avt_ref, bq_ref, bk_ref, bv_ref, cq_ref, ck_ref, cvt_ref,
                dq_ref, dk_ref, dvt_ref):
    bf = jnp.bfloat16
    tm = x_ref.shape[0]
    rep = lambda t: jnp.concatenate([t] * 4, axis=1)

    def epi_c(r, p):
        cc, csn, csp = ctab_ref[0, r], ctab_ref[1, r], ctab_ref[2, r]
        q = _group_rms(p[:, 0:256], 4) * cq_g_ref[...]
        cq_ref[r, :] = (_rope(q, cc, csn, csp, 32) * (HEAD_DIM ** -0.5 * LOG2E)).astype(bf)
        k = _group_rms(p[:, 256:384], 2) * ck_g_ref[...]
        ck_ref[r, :] = _rope(k, cc[:, :128], csn[:, :128], csp[:, :128], 32).astype(bf)
        cvt_ref[:, r] = p[:, 384:512].T.astype(bf)

    def epi_d(r, p):
        dc, dsn, dsp = dtab_ref[0, r], dtab_ref[1, r], dtab_ref[2, r]
        cq = _rms(p[:, 0:256], dq_g_ref[...]).astype(bf)
        q = jnp.dot(cq, wuq_ref[...], preferred_element_type=jnp.float32)
        q = _rope(q, rep(dc), rep(dsn), rep(dsp), 16)
        dq_ref[r, :] = (q * ((MLA_NOPE + MLA_ROPE) ** -0.5 * LOG2E)).astype(bf)
        ckv = _rms(p[:, 256:384], dkv_g_ref[...]).astype(bf)
        kr = _rope(p[:, 384:512], dc, dsn, dsp, 16)
        kn = jnp.dot(ckv, wukvk_ref[...], preferred_element_type=jnp.float32)
        dk_ref[r, :] = (kn + rep(kr)).astype(bf)
        dvt_ref[:, r] = jnp.dot(ckv, wukvv_ref[...], preferred_element_type=jnp.float32).T.astype(bf)

    def epi_a(r, p):
        aq_ref[r, :] = (p[:, 0:256] * (DIFF_HALF ** -0.5 * LOG2E)).astype(bf)
        ak_ref[r, :] = p[:, 256:512].astype(bf)
        avt_ref[:, r] = p[:, 512:768].T.astype(bf)

    def epi_b(r, p):
        bq_ref[r, :] = (p[:, 0:256] * 0.125).astype(bf)
        bk_ref[r, :] = p[:, 256:384].astype(bf)
        bv_ref[r, :] = p[:, 384:512].astype(bf)

    mixers = [((1280, 1792), epi_c), ((1792, 2304), epi_d), ((0, 768), epi_a), ((768, 1280), epi_b)]
    halves = [slice(0, tm // 2), slice(tm // 2, tm)]
    pending = None
    for r in halves:
        h = _rms(x_ref[r, :], g_ref[...]).astype(bf)
        for (lo, hi), epilogue in mixers:
            p = jnp.dot(h, w_ref[:, lo:hi], preferred_element_type=jnp.float32)
            if pending is not None:
                pending[0](pending[1], pending[2])
            pending = (epilogue, r, p)
    pending[0](pending[1], pending[2])


def _pre(x2, g, w, cq_g, ck_g, ctab, dq_g, dkv_g, wuq, wukvk, wukvv, dtab, seq, tm):
    n = x2.shape[0]
    nt = seq // tm
    row = lambda wd: pl.BlockSpec((tm, wd), lambda i: (i, 0))
    col = lambda wd: pl.BlockSpec((wd, tm), lambda i: (0, i))
    full = lambda a: pl.BlockSpec(a.shape, lambda i: (0,) * a.ndim)
    bf = jnp.bfloat16
    rows = lambda wd: (row(wd), jax.ShapeDtypeStruct((n, wd), bf))
    cols = lambda wd: (col(wd), jax.ShapeDtypeStruct((wd, n), bf))
    outs = [rows(256), rows(256), cols(256), rows(256), rows(128), rows(128),
            rows(256), rows(128), cols(128), rows(512), rows(512), cols(256)]
    return pl.pallas_call(
        _pre_kernel,
        grid=(n // tm,),
        in_specs=[row(D_MODEL), full(g), full(w), full(cq_g), full(ck_g),
                  pl.BlockSpec((3, tm, 256), lambda i: (0, i % nt, 0)),
                  full(dq_g), full(dkv_g), full(wuq), full(wukvk), full(wukvv),
                  pl.BlockSpec((3, tm, 128), lambda i: (0, i % nt, 0))],
        out_specs=[o[0] for o in outs],
        out_shape=[o[1] for o in outs],
        compiler_params=_cparams(("parallel",)),
        name="pre_attention",
    )(x2, g, w, cq_g, ck_g, ctab, dq_g, dkv_g, wuq, wukvk, wukvv, dtab)


_NT = (((1,), (1,)), ((), ()))


FLASH_KEYS = 512
FLASH_LAG = 4


def _attend_flash(chains):
    n = len(chains)
    nblk = chains[0][1].shape[0] // FLASH_KEYS
    fold = lambda a: a.reshape(a.shape[0] // 8, 8, a.shape[1])
    state = [None] * n

    def scores(b, c):
        q, k, _, bias = chains[c]
        rows = slice(b * FLASH_KEYS, (b + 1) * FLASH_KEYS)
        s = lax.dot_general(k[rows], q, _NT, preferred_element_type=jnp.float32)
        if bias is not None:
            per = FLASH_KEYS // BLOCK
            s = s + jnp.concatenate([bias(b * per + j) for j in range(per)], axis=0)
        return s

    def update(b, c, s):
        vt = chains[c][2][:, b * FLASH_KEYS:(b + 1) * FLASH_KEYS]
        mb = jnp.max(jnp.max(fold(s), axis=0), axis=0, keepdims=True)
        if state[c] is None:
            m = mb
            p = jnp.exp2(s - m)
            l = jnp.sum(jnp.sum(fold(p), axis=0), axis=0, keepdims=True)
            acc = jnp.dot(vt, p.astype(jnp.bfloat16), preferred_element_type=jnp.float32)
        else:
            m0, l0, acc0 = state[c]
            m = jnp.maximum(m0, mb)
            alpha = jnp.exp2(m0 - m)
            p = jnp.exp2(s - m)
            l = alpha * l0 + jnp.sum(jnp.sum(fold(p), axis=0), axis=0, keepdims=True)
            acc = alpha * acc0 + jnp.dot(vt, p.astype(jnp.bfloat16), preferred_element_type=jnp.float32)
        state[c] = (m, l, acc)

    units = [(b, c) for b in range(nblk) for c in range(n)]
    pending = []
    for u in units:
        pending.append((*u, scores(*u)))
        if len(pending) > FLASH_LAG:
            update(*pending.pop(0))
    for item in pending:
        update(*item)
    return [acc * (1.0 / l) for _, l, acc in state]


def _diff_kernel(q_ref, k_ref, vt_ref, gt_ref, dl_ref, sub_ref, o_ref, *, lam_init, seq, tq, cq):
    i = pl.program_id(1)
    lane = lax.broadcasted_iota(jnp.int32, (cq, LANES), 1)
    upper_rows = lax.broadcasted_iota(jnp.int32, (LANES, cq), 0) >= HEAD_DIM
    dl = dl_ref[0]
    lam = (jnp.exp(jnp.sum(dl[0:1] * dl[1:2], axis=-1, keepdims=True))
           - jnp.exp(jnp.sum(dl[2:3] * dl[3:4], axis=-1, keepdims=True)) + lam_init)
    chains = []
    for r0 in range(0, tq, cq):
        offs = [pl.multiple_of(jnp.clip(jb * BLOCK - (i * tq + r0), -NEAR, cq + BLOCK) + NEAR, BLOCK)
                for jb in range(seq // BLOCK)]
        for pr in range(2):
            pl_ = slice(pr * LANES, (pr + 1) * LANES)
            q, k, vt = q_ref[0, r0:r0 + cq, pl_], k_ref[0, :, pl_], vt_ref[pl_, :]
            for hh in range(2):
                bias_t = lambda jb, h=2 * pr + hh, offs=offs: gt_ref[h, pl.ds(offs[jb], BLOCK), :]
                for c in range(2):
                    lo = hh * HEAD_DIM + c * DIFF_HALF
                    qm = jnp.where((lane >= lo) & (lane < lo + DIFF_HALF), q, jnp.zeros_like(q))
                    chains.append((qm, k, vt, bias_t))
    outs = _attend_flash(chains)
    for n, r0 in enumerate(range(0, tq, cq)):
        for pr in range(2):
            heads = []
            for hh in range(2):
                first = 8 * n + 4 * pr + 2 * hh
                oh = outs[first] - lam * outs[first + 1]
                mh = upper_rows if hh else jnp.logical_not(upper_rows)
                ss = jnp.sum(jnp.where(mh, oh * oh, 0.0), axis=0, keepdims=True) * (1.0 / HEAD_DIM)
                heads.append(oh * lax.rsqrt(ss + EPS))
            out = jnp.where(upper_rows, heads[1], heads[0]).T * sub_ref[...] * (1.0 - lam_init)
            o_ref[0, r0:r0 + cq, pr * LANES:(pr + 1) * LANES] = out.astype(jnp.bfloat16)


def _diff_attention(q, k, vt, gt, dl, sub, lam_init, tq, cq):
    b, seq, _ = q.shape
    return pl.pallas_call(
        functools.partial(_diff_kernel, lam_init=lam_init, seq=seq, tq=tq, cq=cq),
        grid=(b, seq // tq),
        in_specs=[pl.BlockSpec((1, tq, GROUP_WIDTH), lambda bi, i: (bi, i, 0)),
                  pl.BlockSpec((1, seq, GROUP_WIDTH), lambda bi, i: (bi, 0, 0)),
                  pl.BlockSpec((GROUP_WIDTH, seq), lambda bi, i: (0, bi)),
                  pl.BlockSpec(gt.shape, lambda bi, i: (0, 0, 0)),
                  pl.BlockSpec((1, 4, DIFF_HALF), lambda bi, i: (0, 0, 0)),
                  pl.BlockSpec((1, LANES), lambda bi, i: (0, 0))],
        out_specs=pl.BlockSpec((1, tq, GROUP_WIDTH), lambda bi, i: (bi, i, 0)),
        out_shape=jax.ShapeDtypeStruct((b, seq, GROUP_WIDTH), jnp.bfloat16),
        compiler_params=_cparams(("parallel", "arbitrary")),
        name="diff_attention",
    )(q, k, vt, gt, dl, sub)


def _gqa_kernel(q_ref, k_ref, vt_ref, o_ref, *, tq, cq):
    k, vt = k_ref[0], vt_ref[...]
    lower = lax.broadcasted_iota(jnp.int32, (cq, LANES), 1) < HEAD_DIM
    upper_rows = lax.broadcasted_iota(jnp.int32, (LANES, cq), 0) >= HEAD_DIM
    chains = []
    for r0 in range(0, tq, cq):
        for j in range(2):
            q = q_ref[0, r0:r0 + cq, j * LANES:(j + 1) * LANES]
            zero = jnp.zeros_like(q)
            chains.append((jnp.where(lower, q, zero), k, vt, None))
            chains.append((jnp.where(lower, zero, q), k, vt, None))
    outs = _attend_flash(chains)
    for n, r0 in enumerate(range(0, tq, cq)):
        for j in range(2):
            lo, hi = outs[4 * n + 2 * j], outs[4 * n + 2 * j + 1]
            o_ref[0, r0:r0 + cq, j * LANES:(j + 1) * LANES] = jnp.where(
                upper_rows, hi, lo).T.astype(jnp.bfloat16)


def _gqa_attention(q, k, vt, tq, cq):
    b, seq, _ = q.shape
    return pl.pallas_call(
        functools.partial(_gqa_kernel, tq=tq, cq=cq),
        grid=(b, seq // tq),
        in_specs=[pl.BlockSpec((1, tq, GROUP_WIDTH), lambda bi, i: (bi, i, 0)),
                  pl.BlockSpec((1, seq, LANES), lambda bi, i: (bi, 0, 0)),
                  pl.BlockSpec((LANES, seq), lambda bi, i: (0, bi))],
        out_specs=pl.BlockSpec((1, tq, GROUP_WIDTH), lambda bi, i: (bi, i, 0)),
        out_shape=jax.ShapeDtypeStruct((b, seq, GROUP_WIDTH), jnp.bfloat16),
        compiler_params=_cparams(("parallel", "arbitrary")),
        name="axial_gqa_attention",
    )(q, k, vt)


def _mla_kernel(q_ref, k_ref, vt_ref, o_ref, *, tq, cq):
    upper_rows = lax.broadcasted_iota(jnp.int32, (LANES, cq), 0) >= HEAD_DIM
    chains = []
    for r0 in range(0, tq, cq):
        for hh in range(4):
            hl = slice(hh * LANES, (hh + 1) * LANES)
            vl = slice((hh // 2) * LANES, (hh // 2 + 1) * LANES)
            chains.append((q_ref[0, r0:r0 + cq, hl], k_ref[0, :, hl], vt_ref[vl, :], None))
    outs = _attend_flash(chains)
    for n, r0 in enumerate(range(0, tq, cq)):
        for pr in range(2):
            lo, hi = outs[4 * n + 2 * pr], outs[4 * n + 2 * pr + 1]
            o_ref[0, r0:r0 + cq, pr * LANES:(pr + 1) * LANES] = jnp.where(
                upper_rows, hi, lo).T.astype(jnp.bfloat16)


def _mla_attention(q, k, vt, tq, cq):
    b, seq, _ = q.shape
    return pl.pallas_call(
        functools.partial(_mla_kernel, tq=tq, cq=cq),
        grid=(b, seq // tq),
        in_specs=[pl.BlockSpec((1, tq, 4 * LANES), lambda bi, i: (bi, i, 0)),
                  pl.BlockSpec((1, seq, 4 * LANES), lambda bi, i: (bi, 0, 0)),
                  pl.BlockSpec((GROUP_WIDTH, seq), lambda bi, i: (0, bi))],
        out_specs=pl.BlockSpec((1, tq, GROUP_WIDTH), lambda bi, i: (bi, i, 0)),
        out_shape=jax.ShapeDtypeStruct((b, seq, GROUP_WIDTH), jnp.bfloat16),
        compiler_params=_cparams(("parallel", "arbitrary")),
        name="latent_attention",
    )(q, k, vt)


def _softmax_parts(s, extra):
    m = jnp.maximum(jnp.max(s, axis=-1, keepdims=True), extra)
    p = jnp.exp(s - m)
    l = jnp.sum(p, axis=-1, keepdims=True) + jnp.exp(extra - m)
    return p.astype(jnp.bfloat16), 1.0 / l


def _win_kernel(sink_ref, q_ref, kp_ref, kc_ref, kn_ref, vp_ref, vc_ref, vn_ref, wb_ref, o_ref,
                *, nstep, nb):
    n = pl.program_id(1)
    kext = jnp.concatenate([kp_ref[0], kc_ref[0], kn_ref[0]], axis=0)
    vext = jnp.concatenate([vp_ref[0], vc_ref[0], vn_ref[0]], axis=0)
    lane = lax.broadcasted_iota(jnp.int32, (BLOCK, LANES), 1)
    col = lax.broadcasted_iota(jnp.int32, (BLOCK, 3 * BLOCK), 1)
    units = [(blk, j) for blk in range(nb) for j in range(2)]

    def scores(blk, j):
        q = q_ref[0, blk * BLOCK:(blk + 1) * BLOCK, j * LANES:(j + 1) * LANES]
        zero = jnp.zeros_like(q)
        qs = jnp.concatenate([jnp.where(lane < HEAD_DIM, q, zero),
                              jnp.where(lane >= HEAD_DIM, q, zero)], axis=0)
        return lax.dot_general(qs, kext[blk * BLOCK:(blk + 3) * BLOCK], _NT,
                               preferred_element_type=jnp.float32)

    def softmax(blk, j, s):
        ps, rs = [], []
        for g in range(2):
            head = j + 2 * g
            sg = s[g * BLOCK:(g + 1) * BLOCK] + wb_ref[head]
            if blk == 0:
                sg = jnp.where((col < BLOCK) & (n == 0), NEG_BIG, sg)
            if blk == nb - 1:
                sg = jnp.where((col >= 2 * BLOCK) & (n == nstep - 1), NEG_BIG, sg)
            p, r = _softmax_parts(sg, sink_ref[head])
            ps.append(p)
            rs.append(r)
        return jnp.concatenate(ps, axis=0), rs

    def pv(blk, j, p, rs):
        o = jnp.dot(p, vext[blk * BLOCK:(blk + 3) * BLOCK], preferred_element_type=jnp.float32)
        out = jnp.where(lane < HEAD_DIM, o[:BLOCK] * rs[0], o[BLOCK:] * rs[1])
        o_ref[0, blk * BLOCK:(blk + 1) * BLOCK, j * LANES:(j + 1) * LANES] = out.astype(jnp.bfloat16)

    s, pr = {}, {}
    for t in range(len(units) + 2):
        if t < len(units):
            s[t] = scores(*units[t])
        if 0 <= t - 2 < len(units):
            pv(*units[t - 2], *pr.pop(t - 2))
        if 0 <= t - 1 < len(units):
            pr[t - 1] = softmax(*units[t - 1], s.pop(t - 1))


def _win_attention(q, k, v, sink, wbias, tw):
    b, seq, _ = q.shape
    nblk = seq // BLOCK
    nb = tw // BLOCK
    prev = lambda bi, n: (bi, jnp.maximum(n * nb - 1, 0), 0)
    cur = lambda bi, n: (bi, n, 0)
    nxt = lambda bi, n: (bi, jnp.minimum((n + 1) * nb, nblk - 1), 0)
    halo = lambda f: pl.BlockSpec((1, BLOCK, LANES), f)
    body = pl.BlockSpec((1, tw, LANES), cur)
    return pl.pallas_call(
        functools.partial(_win_kernel, nstep=seq // tw, nb=nb),
        grid=(b, seq // tw),
        in_specs=[pl.BlockSpec(memory_space=pltpu.SMEM),
                  pl.BlockSpec((1, tw, GROUP_WIDTH), cur),
                  halo(prev), body, halo(nxt), halo(prev), body, halo(nxt),
                  pl.BlockSpec((4, BLOCK, 3 * BLOCK), lambda bi, n: (0, 0, 0))],
        out_specs=pl.BlockSpec((1, tw, GROUP_WIDTH), cur),
        out_shape=jax.ShapeDtypeStruct((b, seq, GROUP_WIDTH), jnp.bfloat16),
        compiler_params=_cparams(("parallel", "arbitrary")),
        name="window_attention",
    )(sink, q, k, k, k, v, v, v, wbias)


def _outproj_kernel(x_ref, a_ref, b_ref, c_ref, d_ref, w_ref, g_ref, o_ref):
    tm = x_ref.shape[0]
    halves = [slice(0, tm // 2), slice(tm // 2, tm)]

    def project(r):
        mixed = jnp.concatenate([a_ref[r, :], b_ref[r, :], c_ref[r, :], d_ref[r, :]], axis=1)
        return jnp.dot(mixed, w_ref[...], preferred_element_type=jnp.float32)

    ys = [project(r) for r in halves]
    for r, y in zip(halves, ys):
        o_ref[r, :] = x_ref[r, :] + _rms(y, g_ref[...])


def _outproj(x2, oa, ob, oc, od, w, g, tm):
    n = x2.shape[0]
    row = lambda wd: pl.BlockSpec((tm, wd), lambda i: (i, 0))
    return pl.pallas_call(
        _outproj_kernel,
        grid=(n // tm,),
        in_specs=[row(D_MODEL), row(256), row(256), row(256), row(256),
                  pl.BlockSpec(w.shape, lambda i: (0, 0)),
                  pl.BlockSpec(g.shape, lambda i: (0, 0))],
        out_specs=row(D_MODEL),
        out_shape=jax.ShapeDtypeStruct((n, D_MODEL), jnp.float32),
        compiler_params=_cparams(("parallel",)),
        name="out_projection",
    )(x2, oa, ob, oc, od, w, g)


HALO = 8
DOWN_GROUP_ENDS = (5, 10, 11)


def _ffn_kernel(x_ref, xp_ref, xn_ref, gpre_ref, wup_ref, cw_ref, cb_ref, wdn_ref, gpost_ref,
                o_ref, h_ref, *, tm, fc, tiles_per_seq):
    i = pl.program_id(0)
    bf = jnp.bfloat16
    g = gpre_ref[...]
    first = (i % tiles_per_seq) == 0
    last = (i % tiles_per_seq) == tiles_per_seq - 1
    h_ref[0:HALO, :] = jnp.where(first, 0.0, _rms(xp_ref[...], g)).astype(bf)
    h_ref[HALO:HALO + tm, :] = _rms(x_ref[...], g).astype(bf)
    h_ref[HALO + tm:, :] = jnp.where(last, 0.0, _rms(xn_ref[...], g)).astype(bf)
    nf = D_FF // fc
    chunk = lambda c: slice(c * fc, (c + 1) * fc)

    def up(c):
        val_cols = slice(D_FF + c * fc, D_FF + (c + 1) * fc)
        gate = jnp.dot(h_ref[...], wup_ref[:, chunk(c)], preferred_element_type=jnp.float32)
        val = jnp.dot(h_ref[HALO:HALO + tm, :], wup_ref[:, val_cols], preferred_element_type=jnp.float32)
        return gate, val

    y = None
    acts, group_start = [], 0
    nxt = up(0)
    for c in range(nf):
        cs = chunk(c)
        gate, val = nxt
        if c + 1 < nf:
            nxt = up(c + 1)
        gate = (gate[HALO - 1:HALO - 1 + tm] * cw_ref[0:1, cs] + gate[HALO:HALO + tm] * cw_ref[1:2, cs]
                + gate[HALO + 1:HALO + 1 + tm] * cw_ref[2:3, cs] + cb_ref[:, cs])
        acts.append((0.5 * gate * (1.0 + lax.erf(gate * (2.0 ** -0.5))) * val).astype(bf))
        if c + 1 in DOWN_GROUP_ENDS:
            rows = slice(group_start * fc, (c + 1) * fc)
            part = jnp.dot(jnp.concatenate(acts, axis=1), wdn_ref[rows, :],
                           preferred_element_type=jnp.float32)
            y = part if y is None else y + part
            acts, group_start = [], c + 1
    o_ref[...] = x_ref[...] + _rms(y, gpost_ref[...])


def _ffn(x2, gpre, wup, cw, cb, wdn, gpost, seq, tm, fc):
    n = x2.shape[0]
    tph = tm // HALO
    nh = n // HALO
    resident = lambda a: pl.BlockSpec(a.shape, lambda i: (0,) * a.ndim,
                                      pipeline_mode=pl.Buffered(1))
    return pl.pallas_call(
        functools.partial(_ffn_kernel, tm=tm, fc=fc, tiles_per_seq=seq // tm),
        grid=(n // tm,),
        in_specs=[pl.BlockSpec((tm, D_MODEL), lambda i: (i, 0)),
                  pl.BlockSpec((HALO, D_MODEL), lambda i: (jnp.maximum(i * tph - 1, 0), 0)),
                  pl.BlockSpec((HALO, D_MODEL), lambda i: (jnp.minimum((i + 1) * tph, nh - 1), 0)),
                  resident(gpre), resident(wup), resident(cw), resident(cb),
                  resident(wdn), resident(gpost)],
        out_specs=pl.BlockSpec((tm, D_MODEL), lambda i: (i, 0)),
        out_shape=jax.ShapeDtypeStruct((n, D_MODEL), jnp.float32),
        scratch_shapes=[pltpu.VMEM((tm + 2 * HALO, D_MODEL), jnp.bfloat16)],
        compiler_params=_cparams(("parallel",)),
        name="conv_glu",
    )(x2, x2, x2, gpre, wup, cw, cb, wdn, gpost)


_GQA_HEAD_ORDER = (0, 2, 1, 3)


def _reorder_heads(w, order):
    lead = w.shape[:-1]
    nh = w.shape[-1] // HEAD_DIM
    w = w.reshape(lead + (nh, HEAD_DIM))
    return jnp.stack([w[..., o, :] for o in order], axis=-2).reshape(lead + (nh * HEAD_DIM,))


def _axial_perm(w):
    lead = w.shape[:-1]
    nh = w.shape[-1] // HEAD_DIM
    w = w.reshape(lead + (nh, 2, 2, 16))
    return jnp.swapaxes(w, -3, -2).reshape(lead + (nh * HEAD_DIM,))


def _prep_w_in(w_in):
    w_in = w_in.astype(jnp.bfloat16)
    bq = _reorder_heads(w_in[..., 768:1024], _GQA_HEAD_ORDER)
    cq = _axial_perm(_reorder_heads(w_in[..., 1280:1536], _GQA_HEAD_ORDER))
    ck = _axial_perm(w_in[..., 1536:1664])
    kr = w_in[..., 2176:2208]
    z = lambda wd: jnp.zeros(w_in.shape[:-1] + (wd,), w_in.dtype)
    return jnp.concatenate(
        [w_in[..., 0:768], bq, w_in[..., 1024:1280], cq, ck, w_in[..., 1664:1792],
         w_in[..., 1792:2176], z(MLA_NOPE), kr, z(LANES - MLA_NOPE - MLA_ROPE)], axis=-1)


def _rope_tables(cos, sin, lead_ones, trail_ones):
    s = cos.shape[0]
    one = lambda wd: jnp.ones((s, wd), jnp.float32)
    zero = lambda wd: jnp.zeros((s, wd), jnp.float32)
    c = jnp.concatenate([one(lead_ones), cos, cos, one(trail_ones)], axis=1)
    sn = jnp.concatenate([zero(lead_ones), -sin, zero(cos.shape[1]), zero(trail_ones)], axis=1)
    sp = jnp.concatenate([zero(lead_ones), zero(cos.shape[1]), sin, zero(trail_ones)], axis=1)
    return jnp.stack([c, sn, sp])


def _angles(pos, dim):
    inv = ROPE_THETA ** (-jnp.arange(0, dim, 2, dtype=jnp.float32) / dim)
    ang = pos.astype(jnp.float32)[:, None] * inv[None, :]
    return jnp.cos(ang), jnp.sin(ang)


def _forward(x, rel_bias, attn_pre_norm, w_in, diff_lambda, diff_subln, win_sink, ax_q_norm, ax_k_norm,
             mla_q_norm, mla_kv_norm, mla_w_uq, mla_w_ukv, w_out, attn_post_norm, ffn_pre_norm,
             ffn_w_up, ffn_conv_w, ffn_conv_b, ffn_w_down, ffn_post_norm, *, tq_a, tq, cq, tw, tm, tf, fc):
    bsz, seq, _ = x.shape
    depth = w_in.shape[0]
    n = bsz * seq
    bf = jnp.bfloat16

    pos = jnp.arange(seq, dtype=jnp.int32)
    rows = seq // GRID_W
    row_ids = jnp.repeat(jnp.arange(rows, dtype=jnp.int32), GRID_W)
    col_ids = jnp.tile(jnp.arange(GRID_W, dtype=jnp.int32), rows)
    rcos, rsin = _angles(row_ids, HEAD_DIM // 2)
    ccos, csin = _angles(col_ids, HEAD_DIM // 2)
    ctab = _rope_tables(jnp.concatenate([rcos, ccos], 1), jnp.concatenate([rsin, csin], 1), 0, 0)
    ctab = jnp.tile(ctab, (1, 1, 4))
    mcos, msin = _angles(pos, MLA_ROPE)
    dtab = _rope_tables(mcos, msin, MLA_NOPE, LANES - MLA_NOPE - MLA_ROPE)

    rel_a = (jnp.arange(cq + 2 * NEAR, dtype=jnp.int32)[:, None] - NEAR
             - jnp.arange(cq, dtype=jnp.int32)[None, :])
    gt = _bias_tiles(rel_bias, _bucket(rel_a), 0, 4, LOG2E)
    rel_b = jnp.arange(3 * BLOCK, dtype=jnp.int32)[None, :] - BLOCK - jnp.arange(BLOCK, dtype=jnp.int32)[:, None]
    idx_b = jnp.where(jnp.abs(rel_b) <= WINDOW, _bucket(rel_b), -1)
    wbias = _bias_tiles(rel_bias, idx_b, 4, 4, 1.0)

    w_in_p = _prep_w_in(w_in)
    cq_g = jnp.tile(_axial_perm(ax_q_norm), (1, 4))[:, None, :]
    ck_g = jnp.tile(_axial_perm(ax_k_norm), (1, 2))[:, None, :]
    wuq = jnp.pad(mla_w_uq.reshape(depth, MLA_Q_RANK, 4, MLA_NOPE + MLA_ROPE),
                  ((0, 0), (0, 0), (0, 0), (0, LANES - MLA_NOPE - MLA_ROPE))
                  ).reshape(depth, MLA_Q_RANK, 4 * LANES).astype(bf)
    wukv = mla_w_ukv.reshape(depth, MLA_KV_RANK, 4, 2, HEAD_DIM)
    wukvk = jnp.pad(wukv[:, :, :, 0], ((0, 0), (0, 0), (0, 0), (0, HEAD_DIM))
                    ).reshape(depth, MLA_KV_RANK, 4 * LANES).astype(bf)
    wukvv = wukv[:, :, :, 1].reshape(depth, MLA_KV_RANK, GROUP_WIDTH).astype(bf)
    wo = w_out.astype(bf).reshape(depth, 4, 4, HEAD_DIM, D_MODEL)
    slots = [(0, h) for h in range(4)] + [(1, h) for h in _GQA_HEAD_ORDER] \
        + [(2, h) for h in _GQA_HEAD_ORDER] + [(3, h) for h in range(4)]
    wo = jnp.stack([wo[:, m, h] for m, h in slots], axis=1).reshape(depth, D_MODEL, D_MODEL)
    wup = ffn_w_up.astype(bf)
    wdn = ffn_w_down.astype(bf)
    sub = jnp.tile(diff_subln, (1, 2))[:, None, :]

    x2 = x.reshape(n, D_MODEL)
    r3 = lambda a: a.reshape(bsz, seq, a.shape[-1])
    for l in range(depth):
        lam_init = 0.8 - 0.6 * math.exp(-0.3 * l)
        (aq, ak, avt, bq, bk, bv, cq_, ck, cvt, dq, dk, dvt) = _pre(
            x2, attn_pre_norm[l][None], w_in_p[l], cq_g[l], ck_g[l], ctab,
            mla_q_norm[l][None], mla_kv_norm[l][None], wuq[l], wukvk[l], wukvv[l], dtab, seq, tm)
        oa = _diff_attention(r3(aq), r3(ak), avt, gt, diff_lambda[l][None], sub[l], lam_init,
                             tq_a, cq)
        ob = _win_attention(r3(bq), r3(bk), r3(bv), win_sink[l], wbias, tw)
        oc = _gqa_attention(r3(cq_), r3(ck), cvt, tq, cq)
        od = _mla_attention(r3(dq), r3(dk), dvt, tq, cq)
        x2 = _outproj(x2, oa.reshape(n, -1), ob.reshape(n, -1), oc.reshape(n, -1),
                      od.reshape(n, -1), wo[l], attn_post_norm[l][None], tm)
        x2 = _ffn(x2, ffn_pre_norm[l][None], wup[l], ffn_conv_w[l], ffn_conv_b[l][None],
                  wdn[l], ffn_post_norm[l][None], seq, tf, fc)
    return x2.reshape(bsz, seq, D_MODEL)


def kernel(x, rel_bias, attn_pre_norm, w_in, diff_lambda, diff_subln, win_sink, ax_q_norm, ax_k_norm,
           mla_q_norm, mla_kv_norm, mla_w_uq, mla_w_ukv, w_out, attn_post_norm, ffn_pre_norm,
           ffn_w_up, ffn_conv_w, ffn_conv_b, ffn_w_down, ffn_post_norm):
    return _forward(x, rel_bias, attn_pre_norm, w_in, diff_lambda, diff_subln, win_sink, ax_q_norm,
                    ax_k_norm, mla_q_norm, mla_kv_norm, mla_w_uq, mla_w_ukv, w_out, attn_post_norm,
                    ffn_pre_norm, ffn_w_up, ffn_conv_w, ffn_conv_b, ffn_w_down, ffn_post_norm,
                    tq_a=512, tq=1024, cq=256, tw=1024, tm=1024, tf=512, fc=256)
```

```python
import functools
import math

import jax
import jax.numpy as jnp
from jax import lax
from jax.experimental import pallas as pl
from jax.experimental.pallas import tpu as pltpu

D_MODEL = 1024
HEAD_DIM = 64
GROUP_WIDTH = 256
BLOCK = 128
DIFF_HALF = 32
WINDOW = 128
GRID_W = 64
ROPE_THETA = 10000.0
MLA_Q_RANK = 256
MLA_KV_RANK = 128
MLA_NOPE = 64
MLA_ROPE = 32
REL_BUCKETS = 32
REL_MAX_DIST = 128
D_FF = 2816
EPS = 1e-6
LANES = 128
LOG2E = 1.4426950408889634
NEG_BIG = -1e30
VMEM_LIMIT = 56 * 1024 * 1024
NEAR = 2 * BLOCK


def _cparams(sem):
    return pltpu.CompilerParams(dimension_semantics=sem, vmem_limit_bytes=VMEM_LIMIT)


def _rms(x, g):
    return x * lax.rsqrt(jnp.mean(x * x, axis=-1, keepdims=True) + EPS) * g


def _group_rms(x, ngroups):
    grp = lax.broadcasted_iota(jnp.int32, x.shape, 1) // HEAD_DIM
    x2 = x * x
    inv = jnp.zeros_like(x)
    for h in range(ngroups):
        m = grp == h
        ss = jnp.sum(jnp.where(m, x2, 0.0), axis=-1, keepdims=True)
        inv = jnp.where(m, lax.rsqrt(ss * (1.0 / HEAD_DIM) + EPS), inv)
    return x * inv


def _rope(x, c, sn, sp, half):
    w = x.shape[-1]
    return x * c + pltpu.roll(x, w - half, 1) * sn + pltpu.roll(x, half, 1) * sp


def _bias_kernel(tab_ref, idx_ref, o_ref, *, head0, scale):
    h = pl.program_id(0)
    idx = idx_ref[...]
    acc = jnp.zeros(idx.shape, jnp.float32)
    for b in range(REL_BUCKETS):
        acc = jnp.where(idx == b, tab_ref[b, head0 + h], acc)
    o_ref[0] = jnp.where(idx < 0, NEG_BIG, acc * scale)


def _bias_tiles(rel_bias, idx, head0, nheads, scale):
    r, c = idx.shape
    return pl.pallas_call(
        functools.partial(_bias_kernel, head0=head0, scale=scale),
        grid=(nheads,),
        in_specs=[pl.BlockSpec(memory_space=pltpu.SMEM),
                  pl.BlockSpec((r, c), lambda h: (0, 0))],
        out_specs=pl.BlockSpec((1, r, c), lambda h: (h, 0, 0)),
        out_shape=jax.ShapeDtypeStruct((nheads, r, c), jnp.float32),
        compiler_params=_cparams(("arbitrary",)),
        name="rel_bias_tiles",
    )(rel_bias, idx)


def _bucket(rel):
    half = REL_BUCKETS // 2
    max_exact = half // 2
    n = jnp.abs(rel)
    nf = jnp.maximum(n, 1).astype(jnp.float32)
    large = max_exact + (jnp.log(nf / max_exact) / math.log(REL_MAX_DIST / max_exact)
                         * (half - max_exact)).astype(jnp.int32)
    large = jnp.minimum(large, half - 1)
    return jnp.where(rel > 0, half, 0) + jnp.where(n < max_exact, n, large)


def _pre_kernel(x_ref, g_ref, w_ref, cq_g_ref, ck_g_ref, ctab_ref, dq_g_ref, dkv_g_ref,
                wuq_ref, wukvk_ref, wukvv_ref, dtab_ref,
                aq_ref, ak_ref, avt_ref, bq_ref, bk_ref, bv_ref, cq_ref, ck_ref, cvt_ref,
                dq_ref, dk_ref, dvt_ref):
    bf = jnp.bfloat16
    tm = x_ref.shape[0]
    rep = lambda t: jnp.concatenate([t] * 4, axis=1)

    def epi_c(r, p):
        cc, csn, csp = ctab_ref[0, r], ctab_ref[1, r], ctab_ref[2, r]
        q = _group_rms(p[:, 0:256], 4) * cq_g_ref[...]
        cq_ref[r, :] = (_rope(q, cc, csn, csp, 16) * (HEAD_DIM ** -0.5 * LOG2E)).astype(bf)
        k = _group_rms(p[:, 256:384], 2) * ck_g_ref[...]
        ck_ref[r, :] = _rope(k, cc[:, :128], csn[:, :128], csp[:, :128], 16).astype(bf)
        cvt_ref[:, r] = p[:, 384:512].T.astype(bf)

    def epi_d(r, p):
        dc, dsn, dsp = dtab_ref[0, r], dtab_ref[1, r], dtab_ref[2, r]
        cq = _rms(p[:, 0:256], dq_g_ref[...]).astype(bf)
        q = jnp.dot(cq, wuq_ref[...], preferred_element_type=jnp.float32)
        q = _rope(q, rep(dc), rep(dsn), rep(dsp), 16)
        dq_ref[r, :] = (q * ((MLA_NOPE + MLA_ROPE) ** -0.5 * LOG2E)).astype(bf)
        ckv = _rms(p[:, 256:384], dkv_g_ref[...]).astype(bf)
        kr = _rope(p[:, 384:512], dc, dsn, dsp, 16)
        kn = jnp.dot(ckv, wukvk_ref[...], preferred_element_type=jnp.float32)
        dk_ref[r, :] = (kn + rep(kr)).astype(bf)
        dvt_ref[:, r] = jnp.dot(ckv, wukvv_ref[...], preferred_element_type=jnp.float32).T.astype(bf)

    def epi_a(r, p):
        aq_ref[r, :] = (p[:, 0:256] * (DIFF_HALF ** -0.5 * LOG2E)).astype(bf)
        ak_ref[r, :] = p[:, 256:512].astype(bf)
        avt_ref[:, r] = p[:, 512:768].T.astype(bf)

    def epi_b(r, p):
        bq_ref[r, :] = (p[:, 0:256] * 0.125).astype(bf)
        bk_ref[r, :] = p[:, 256:384].astype(bf)
        bv_ref[r, :] = p[:, 384:512].astype(bf)

    mixers = [((1280, 1792), epi_c), ((1792, 2304), epi_d), ((0, 768), epi_a), ((768, 1280), epi_b)]
    halves = [slice(0, tm // 2), slice(tm // 2, tm)]
    pending = None
    for r in halves:
        h = _rms(x_ref[r, :], g_ref[...]).astype(bf)
        for (lo, hi), epilogue in mixers:
            p = jnp.dot(h, w_ref[:, lo:hi], preferred_element_type=jnp.float32)
            if pending is not None:
                pending[0](pending[1], pending[2])
            pending = (epilogue, r, p)
    pending[0](pending[1], pending[2])


def _pre(x2, g, w, cq_g, ck_g, ctab, dq_g, dkv_g, wuq, wukvk, wukvv, dtab, seq, tm):
    n = x2.shape[0]
    nt = seq // tm
    row = lambda wd: pl.BlockSpec((tm, wd), lambda i: (i, 0))
    col = lambda wd: pl.BlockSpec((wd, tm), lambda i: (0, i))
    full = lambda a: pl.BlockSpec(a.shape, lambda i: (0,) * a.ndim)
    bf = jnp.bfloat16
    rows = lambda wd: (row(wd), jax.ShapeDtypeStruct((n, wd), bf))
    cols = lambda wd: (col(wd), jax.ShapeDtypeStruct((wd, n), bf))
    outs = [rows(256), rows(256), cols(256), rows(256), rows(128), rows(128),
            rows(256), rows(128), cols(128), rows(512), rows(512), cols(256)]
    return pl.pallas_call(
        _pre_kernel,
        grid=(n // tm,),
        in_specs=[row(D_MODEL), full(g), full(w), full(cq_g), full(ck_g),
                  pl.BlockSpec((3, tm, 256), lambda i: (0, i % nt, 0)),
                  full(dq_g), full(dkv_g), full(wuq), full(wukvk), full(wukvv),
                  pl.BlockSpec((3, tm, 128), lambda i: (0, i % nt, 0))],
        out_specs=[o[0] for o in outs],
        out_shape=[o[1] for o in outs],
        compiler_params=_cparams(("parallel",)),
        name="pre_attention",
    )(x2, g, w, cq_g, ck_g, ctab, dq_g, dkv_g, wuq, wukvk, wukvv, dtab)


_NT = (((1,), (1,)), ((), ()))


FLASH_KEYS = 512
FLASH_LAG = 4


def _attend_flash(chains):
    n = len(chains)
    nblk = chains[0][1].shape[0] // FLASH_KEYS
    fold = lambda a: a.reshape(a.shape[0] // 8, 8, a.shape[1])
    state = [None] * n

    def scores(b, c):
        q, k, _, bias = chains[c]
        rows = slice(b * FLASH_KEYS, (b + 1) * FLASH_KEYS)
        s = lax.dot_general(k[rows], q, _NT, preferred_element_type=jnp.float32)
        if bias is not None:
            per = FLASH_KEYS // BLOCK
            s = s + jnp.concatenate([bias(b * per + j) for j in range(per)], axis=0)
        return s

    def update(b, c, s):
        vt = chains[c][2][:, b * FLASH_KEYS:(b + 1) * FLASH_KEYS]
        mb = jnp.max(jnp.max(fold(s), axis=0), axis=0, keepdims=True)
        if state[c] is None:
            m = mb
            p = jnp.exp2(s - m)
            l = jnp.sum(jnp.sum(fold(p), axis=0), axis=0, keepdims=True)
            acc = jnp.dot(vt, p.astype(jnp.bfloat16), preferred_element_type=jnp.float32)
        else:
            m0, l0, acc0 = state[c]
            m = jnp.maximum(m0, mb)
            alpha = jnp.exp2(m0 - m)
            p = jnp.exp2(s - m)
            l = alpha * l0 + jnp.sum(jnp.sum(fold(p), axis=0), axis=0, keepdims=True)
            acc = alpha * acc0 + jnp.dot(vt, p.astype(jnp.bfloat16), preferred_element_type=jnp.float32)
        state[c] = (m, l, acc)

    units = [(b, c) for b in range(nblk) for c in range(n)]
    pending = []
    for u in units:
        pending.append((*u, scores(*u)))
        if len(pending) > FLASH_LAG:
            update(*pending.pop(0))
    for item in pending:
        update(*item)
    return [acc * (1.0 / l) for _, l, acc in state]


def _diff_kernel(q_ref, k_ref, vt_ref, gt_ref, dl_ref, sub_ref, o_ref, *, lam_init, seq, tq, cq):
    i = pl.program_id(1)
    lane = lax.broadcasted_iota(jnp.int32, (cq, LANES), 1)
    upper_rows = lax.broadcasted_iota(jnp.int32, (LANES, cq), 0) >= HEAD_DIM
    dl = dl_ref[0]
    lam = (jnp.exp(jnp.sum(dl[0:1] * dl[1:2], axis=-1, keepdims=True))
           - jnp.exp(jnp.sum(dl[2:3] * dl[3:4], axis=-1, keepdims=True)) + lam_init)
    chains = []
    for r0 in range(0, tq, cq):
        offs = [pl.multiple_of(jnp.clip(jb * BLOCK - (i * tq + r0), -NEAR, cq + BLOCK) + NEAR, BLOCK)
                for jb in range(seq // BLOCK)]
        for pr in range(2):
            pl_ = slice(pr * LANES, (pr + 1) * LANES)
            q, k, vt = q_ref[0, r0:r0 + cq, pl_], k_ref[0, :, pl_], vt_ref[pl_, :]
            for hh in range(2):
                bias_t = lambda jb, h=2 * pr + hh, offs=offs: gt_ref[h, pl.ds(offs[jb], BLOCK), :]
                for c in range(2):
                    lo = hh * HEAD_DIM + c * DIFF_HALF
                    qm = jnp.where((lane >= lo) & (lane < lo + DIFF_HALF), q, jnp.zeros_like(q))
                    chains.append((qm, k, vt, bias_t))
    outs = _attend_flash(chains)
    for n, r0 in enumerate(range(0, tq, cq)):
        for pr in range(2):
            heads = []
            for hh in range(2):
                first = 8 * n + 4 * pr + 2 * hh
                oh = outs[first] - lam * outs[first + 1]
                mh = upper_rows if hh else jnp.logical_not(upper_rows)
                ss = jnp.sum(jnp.where(mh, oh * oh, 0.0), axis=0, keepdims=True) * (1.0 / HEAD_DIM)
                heads.append(oh * lax.rsqrt(ss + EPS))
            out = jnp.where(upper_rows, heads[1], heads[0]).T * sub_ref[...] * (1.0 - lam_init)
            o_ref[0, r0:r0 + cq, pr * LANES:(pr + 1) * LANES] = out.astype(jnp.bfloat16)


def _diff_attention(q, k, vt, gt, dl, sub, lam_init, tq, cq):
    b, seq, _ = q.shape
    return pl.pallas_call(
        functools.partial(_diff_kernel, lam_init=lam_init, seq=seq, tq=tq, cq=cq),
        grid=(b, seq // tq),
        in_specs=[pl.BlockSpec((1, tq, GROUP_WIDTH), lambda bi, i: (bi, i, 0)),
                  pl.BlockSpec((1, seq, GROUP_WIDTH), lambda bi, i: (bi, 0, 0)),
                  pl.BlockSpec((GROUP_WIDTH, seq), lambda bi, i: (0, bi)),
                  pl.BlockSpec(gt.shape, lambda bi, i: (0, 0, 0)),
                  pl.BlockSpec((1, 4, DIFF_HALF), lambda bi, i: (0, 0, 0)),
                  pl.BlockSpec((1, LANES), lambda bi, i: (0, 0))],
        out_specs=pl.BlockSpec((1, tq, GROUP_WIDTH), lambda bi, i: (bi, i, 0)),
        out_shape=jax.ShapeDtypeStruct((b, seq, GROUP_WIDTH), jnp.bfloat16),
        compiler_params=_cparams(("parallel", "arbitrary")),
        name="diff_attention",
    )(q, k, vt, gt, dl, sub)


def _gqa_kernel(q_ref, k_ref, vt_ref, o_ref, *, tq, cq):
    k, vt = k_ref[0], vt_ref[...]
    lower = lax.broadcasted_iota(jnp.int32, (cq, LANES), 1) < HEAD_DIM
    upper_rows = lax.broadcasted_iota(jnp.int32, (LANES, cq), 0) >= HEAD_DIM
    chains = []
    for r0 in range(0, tq, cq):
        for j in range(2):
            q = q_ref[0, r0:r0 + cq, j * LANES:(j + 1) * LANES]
            zero = jnp.zeros_like(q)
            chains.append((jnp.where(lower, q, zero), k, vt, None))
            chains.append((jnp.where(lower, zero, q), k, vt, None))
    outs = _attend_flash(chains)
    for n, r0 in enumerate(range(0, tq, cq)):
        for j in range(2):
            lo, hi = outs[4 * n + 2 * j], outs[4 * n + 2 * j + 1]
            o_ref[0, r0:r0 + cq, j * LANES:(j + 1) * LANES] = jnp.where(
                upper_rows, hi, lo).T.astype(jnp.bfloat16)


def _gqa_attention(q, k, vt, tq, cq):
    b, seq, _ = q.shape
    return pl.pallas_call(
        functools.partial(_gqa_kernel, tq=tq, cq=cq),
        grid=(b, seq // tq),
        in_specs=[pl.BlockSpec((1, tq, GROUP_WIDTH), lambda bi, i: (bi, i, 0)),
                  pl.BlockSpec((1, seq, LANES), lambda bi, i: (bi, 0, 0)),
                  pl.BlockSpec((LANES, seq), lambda bi, i: (0, bi))],
        out_specs=pl.BlockSpec((1, tq, GROUP_WIDTH), lambda bi, i: (bi, i, 0)),
        out_shape=jax.ShapeDtypeStruct((b, seq, GROUP_WIDTH), jnp.bfloat16),
        compiler_params=_cparams(("parallel", "arbitrary")),
        name="axial_gqa_attention",
    )(q, k, vt)


def _mla_kernel(q_ref, k_ref, vt_ref, o_ref, *, tq, cq):
    upper_rows = lax.broadcasted_iota(jnp.int32, (LANES, cq), 0) >= HEAD_DIM
    chains = []
    for r0 in range(0, tq, cq):
        for hh in range(4):
            hl = slice(hh * LANES, (hh + 1) * LANES)
            vl = slice((hh // 2) * LANES, (hh // 2 + 1) * LANES)
            chains.append((q_ref[0, r0:r0 + cq, hl], k_ref[0, :, hl], vt_ref[vl, :], None))
    outs = _attend_flash(chains)
    for n, r0 in enumerate(range(0, tq, cq)):
        for pr in range(2):
            lo, hi = outs[4 * n + 2 * pr], outs[4 * n + 2 * pr + 1]
            o_ref[0, r0:r0 + cq, pr * LANES:(pr + 1) * LANES] = jnp.where(
                upper_rows, hi, lo).T.astype(jnp.bfloat16)


def _mla_attention(q, k, vt, tq, cq):
    b, seq, _ = q.shape
    return pl.pallas_call(
        functools.partial(_mla_kernel, tq=tq, cq=cq),
        grid=(b, seq // tq),
        in_specs=[pl.BlockSpec((1, tq, 4 * LANES), lambda bi, i: (bi, i, 0)),
                  pl.BlockSpec((1, seq, 4 * LANES), lambda bi, i: (bi, 0, 0)),
                  pl.BlockSpec((GROUP_WIDTH, seq), lambda bi, i: (0, bi))],
        out_specs=pl.BlockSpec((1, tq, GROUP_WIDTH), lambda bi, i: (bi, i, 0)),
        out_shape=jax.ShapeDtypeStruct((b, seq, GROUP_WIDTH), jnp.bfloat16),
        compiler_params=_cparams(("parallel", "arbitrary")),
        name="latent_attention",
    )(q, k, vt)


def _softmax_parts(s, extra):
    m = jnp.maximum(jnp.max(s, axis=-1, keepdims=True), extra)
    p = jnp.exp(s - m)
    l = jnp.sum(p, axis=-1, keepdims=True) + jnp.exp(extra - m)
    return p.astype(jnp.bfloat16), 1.0 / l


def _win_kernel(sink_ref, q_ref, kp_ref, kc_ref, kn_ref, vp_ref, vc_ref, vn_ref, wb_ref, o_ref,
                *, nstep, nb):
    n = pl.program_id(1)
    kext = jnp.concatenate([kp_ref[0], kc_ref[0], kn_ref[0]], axis=0)
    vext = jnp.concatenate([vp_ref[0], vc_ref[0], vn_ref[0]], axis=0)
    lane = lax.broadcasted_iota(jnp.int32, (BLOCK, LANES), 1)
    col = lax.broadcasted_iota(jnp.int32, (BLOCK, 3 * BLOCK), 1)
    units = [(blk, j) for blk in range(nb) for j in range(2)]

    def scores(blk, j):
        q = q_ref[0, blk * BLOCK:(blk + 1) * BLOCK, j * LANES:(j + 1) * LANES]
        zero = jnp.zeros_like(q)
        qs = jnp.concatenate([jnp.where(lane < HEAD_DIM, q, zero),
                              jnp.where(lane >= HEAD_DIM, q, zero)], axis=0)
        return lax.dot_general(qs, kext[blk * BLOCK:(blk + 3) * BLOCK], _NT,
                               preferred_element_type=jnp.float32)

    def softmax(blk, j, s):
        ps, rs = [], []
        for g in range(2):
            head = j + 2 * g
            sg = s[g * BLOCK:(g + 1) * BLOCK] + wb_ref[head]
            if blk == 0:
                sg = jnp.where((col < BLOCK) & (n == 0), NEG_BIG, sg)
            if blk == nb - 1:
                sg = jnp.where((col >= 2 * BLOCK) & (n == nstep - 1), NEG_BIG, sg)
            p, r = _softmax_parts(sg, sink_ref[head])
            ps.append(p)
            rs.append(r)
        return jnp.concatenate(ps, axis=0), rs

    def pv(blk, j, p, rs):
        o = jnp.dot(p, vext[blk * BLOCK:(blk + 3) * BLOCK], preferred_element_type=jnp.float32)
        out = jnp.where(lane < HEAD_DIM, o[:BLOCK] * rs[0], o[BLOCK:] * rs[1])
        o_ref[0, blk * BLOCK:(blk + 1) * BLOCK, j * LANES:(j + 1) * LANES] = out.astype(jnp.bfloat16)

    s, pr = {}, {}
    for t in range(len(units) + 2):
        if t < len(units):
            s[t] = scores(*units[t])
        if 0 <= t - 2 < len(units):
            pv(*units[t - 2], *pr.pop(t - 2))
        if 0 <= t - 1 < len(units):
            pr[t - 1] = softmax(*units[t - 1], s.pop(t - 1))


def _win_attention(q, k, v, sink, wbias, tw):
    b, seq, _ = q.shape
    nblk = seq // BLOCK
    nb = tw // BLOCK
    prev = lambda bi, n: (bi, jnp.maximum(n * nb - 1, 0), 0)
    cur = lambda bi, n: (bi, n, 0)
    nxt = lambda bi, n: (bi, jnp.minimum((n + 1) * nb, nblk - 1), 0)
    halo = lambda f: pl.BlockSpec((1, BLOCK, LANES), f)
    body = pl.BlockSpec((1, tw, LANES), cur)
    return pl.pallas_call(
        functools.partial(_win_kernel, nstep=seq // tw, nb=nb),
        grid=(b, seq // tw),
        in_specs=[pl.BlockSpec(memory_space=pltpu.SMEM),
                  pl.BlockSpec((1, tw, GROUP_WIDTH), cur),
                  halo(prev), body, halo(nxt), halo(prev), body, halo(nxt),
                  pl.BlockSpec((4, BLOCK, 3 * BLOCK), lambda bi, n: (0, 0, 0))],
        out_specs=pl.BlockSpec((1, tw, GROUP_WIDTH), cur),
        out_shape=jax.ShapeDtypeStruct((b, seq, GROUP_WIDTH), jnp.bfloat16),
        compiler_params=_cparams(("parallel", "arbitrary")),
        name="window_attention",
    )(sink, q, k, k, k, v, v, v, wbias)


def _outproj_kernel(x_ref, a_ref, b_ref, c_ref, d_ref, w_ref, g_ref, o_ref):
    tm = x_ref.shape[0]
    halves = [slice(0, tm // 2), slice(tm // 2, tm)]

    def project(r):
        mixed = jnp.concatenate([a_ref[r, :], b_ref[r, :], c_ref[r, :], d_ref[r, :]], axis=1)
        return jnp.dot(mixed, w_ref[...], preferred_element_type=jnp.float32)

    ys = [project(r) for r in halves]
    for r, y in zip(halves, ys):
        o_ref[r, :] = x_ref[r, :] + _rms(y, g_ref[...])


def _outproj(x2, oa, ob, oc, od, w, g, tm):
    n = x2.shape[0]
    row = lambda wd: pl.BlockSpec((tm, wd), lambda i: (i, 0))
    return pl.pallas_call(
        _outproj_kernel,
        grid=(n // tm,),
        in_specs=[row(D_MODEL), row(256), row(256), row(256), row(256),
                  pl.BlockSpec(w.shape, lambda i: (0, 0)),
                  pl.BlockSpec(g.shape, lambda i: (0, 0))],
        out_specs=row(D_MODEL),
        out_shape=jax.ShapeDtypeStruct((n, D_MODEL), jnp.float32),
        compiler_params=_cparams(("parallel",)),
        name="out_projection",
    )(x2, oa, ob, oc, od, w, g)


HALO = 8
DOWN_GROUP_ENDS = (5, 10, 11)


def _ffn_kernel(x_ref, xp_ref, xn_ref, gpre_ref, wup_ref, cw_ref, cb_ref, wdn_ref, gpost_ref,
                o_ref, h_ref, *, tm, fc, tiles_per_seq):
    i = pl.program_id(0)
    bf = jnp.bfloat16
    g = gpre_ref[...]
    first = (i % tiles_per_seq) == 0
    last = (i % tiles_per_seq) == tiles_per_seq - 1
    h_ref[0:HALO, :] = jnp.where(first, 0.0, _rms(xp_ref[...], g)).astype(bf)
    h_ref[HALO:HALO + tm, :] = _rms(x_ref[...], g).astype(bf)
    h_ref[HALO + tm:, :] = jnp.where(last, 0.0, _rms(xn_ref[...], g)).astype(bf)
    nf = D_FF // fc
    chunk = lambda c: slice(c * fc, (c + 1) * fc)

    def up(c):
        val_cols = slice(D_FF + c * fc, D_FF + (c + 1) * fc)
        gate = jnp.dot(h_ref[...], wup_ref[:, chunk(c)], preferred_element_type=jnp.float32)
        val = jnp.dot(h_ref[HALO:HALO + tm, :], wup_ref[:, val_cols], preferred_element_type=jnp.float32)
        return gate, val

    y = None
    acts, group_start = [], 0
    nxt = up(0)
    for c in range(nf):
        cs = chunk(c)
        gate, val = nxt
        if c + 1 < nf:
            nxt = up(c + 1)
        gate = (gate[HALO - 1:HALO - 1 + tm] * cw_ref[0:1, cs] + gate[HALO:HALO + tm] * cw_ref[1:2, cs]
                + gate[HALO + 1:HALO + 1 + tm] * cw_ref[2:3, cs] + cb_ref[:, cs])
        acts.append((0.5 * gate * (1.0 + lax.erf(gate * (2.0 ** -0.5))) * val).astype(bf))
        if c + 1 in DOWN_GROUP_ENDS:
            rows = slice(group_start * fc, (c + 1) * fc)
            part = jnp.dot(jnp.concatenate(acts, axis=1), wdn_ref[rows, :],
                           preferred_element_type=jnp.float32)
            y = part if y is None else y + part
            acts, group_start = [], c + 1
    o_ref[...] = x_ref[...] + _rms(y, gpost_ref[...])


def _ffn(x2, gpre, wup, cw, cb, wdn, gpost, seq, tm, fc):
    n = x2.shape[0]
    tph = tm // HALO
    nh = n // HALO
    resident = lambda a: pl.BlockSpec(a.shape, lambda i: (0,) * a.ndim,
                                      pipeline_mode=pl.Buffered(1))
    return pl.pallas_call(
        functools.partial(_ffn_kernel, tm=tm, fc=fc, tiles_per_seq=seq // tm),
        grid=(n // tm,),
        in_specs=[pl.BlockSpec((tm, D_MODEL), lambda i: (i, 0)),
                  pl.BlockSpec((HALO, D_MODEL), lambda i: (jnp.maximum(i * tph - 1, 0), 0)),
                  pl.BlockSpec((HALO, D_MODEL), lambda i: (jnp.minimum((i + 1) * tph, nh - 1), 0)),
                  resident(gpre), resident(wup), resident(cw), resident(cb),
                  resident(wdn), resident(gpost)],
        out_specs=pl.BlockSpec((tm, D_MODEL), lambda i: (i, 0)),
        out_shape=jax.ShapeDtypeStruct((n, D_MODEL), jnp.float32),
        scratch_shapes=[pltpu.VMEM((tm + 2 * HALO, D_MODEL), jnp.bfloat16)],
        compiler_params=_cparams(("parallel",)),
        name="conv_glu",
    )(x2, x2, x2, gpre, wup, cw, cb, wdn, gpost)


_GQA_HEAD_ORDER = (0, 2, 1, 3)


def _reorder_heads(w, order):
    lead = w.shape[:-1]
    nh = w.shape[-1] // HEAD_DIM
    w = w.reshape(lead + (nh, HEAD_DIM))
    return jnp.stack([w[..., o, :] for o in order], axis=-2).reshape(lead + (nh * HEAD_DIM,))


def _prep_w_in(w_in):
    w_in = w_in.astype(jnp.bfloat16)
    bq = _reorder_heads(w_in[..., 768:1024], _GQA_HEAD_ORDER)
    cq = _reorder_heads(w_in[..., 1280:1536], _GQA_HEAD_ORDER)
    ck = w_in[..., 1536:1664]
    kr = w_in[..., 2176:2208]
    z = lambda wd: jnp.zeros(w_in.shape[:-1] + (wd,), w_in.dtype)
    return jnp.concatenate(
        [w_in[..., 0:768], bq, w_in[..., 1024:1280], cq, ck, w_in[..., 1664:1792],
         w_in[..., 1792:2176], z(MLA_NOPE), kr, z(LANES - MLA_NOPE - MLA_ROPE)], axis=-1)


def _rope_tables(cos, sin, lead_ones, trail_ones):
    s = cos.shape[0]
    one = lambda wd: jnp.ones((s, wd), jnp.float32)
    zero = lambda wd: jnp.zeros((s, wd), jnp.float32)
    c = jnp.concatenate([one(lead_ones), cos, cos, one(trail_ones)], axis=1)
    sn = jnp.concatenate([zero(lead_ones), -sin, zero(cos.shape[1]), zero(trail_ones)], axis=1)
    sp = jnp.concatenate([zero(lead_ones), zero(cos.shape[1]), sin, zero(trail_ones)], axis=1)
    return jnp.stack([c, sn, sp])


def _angles(pos, dim):
    inv = ROPE_THETA ** (-jnp.arange(0, dim, 2, dtype=jnp.float32) / dim)
    ang = pos.astype(jnp.float32)[:, None] * inv[None, :]
    return jnp.cos(ang), jnp.sin(ang)


def _forward(x, rel_bias, attn_pre_norm, w_in, diff_lambda, diff_subln, win_sink, ax_q_norm, ax_k_norm,
             mla_q_norm, mla_kv_norm, mla_w_uq, mla_w_ukv, w_out, attn_post_norm, ffn_pre_norm,
             ffn_w_up, ffn_conv_w, ffn_conv_b, ffn_w_down, ffn_post_norm, *, tq_a, tq, cq, tw, tm, to, tf, fc):
    bsz, seq, _ = x.shape
    depth = w_in.shape[0]
    n = bsz * seq
    bf = jnp.bfloat16

    pos = jnp.arange(seq, dtype=jnp.int32)
    rows = seq // GRID_W
    row_ids = jnp.repeat(jnp.arange(rows, dtype=jnp.int32), GRID_W)
    col_ids = jnp.tile(jnp.arange(GRID_W, dtype=jnp.int32), rows)
    rcos, rsin = _angles(row_ids, HEAD_DIM // 2)
    ccos, csin = _angles(col_ids, HEAD_DIM // 2)
    ctab = jnp.concatenate([_rope_tables(rcos, rsin, 0, 0), _rope_tables(ccos, csin, 0, 0)], axis=2)
    ctab = jnp.tile(ctab, (1, 1, 4))
    mcos, msin = _angles(pos, MLA_ROPE)
    dtab = _rope_tables(mcos, msin, MLA_NOPE, LANES - MLA_NOPE - MLA_ROPE)

    rel_a = (jnp.arange(cq + 2 * NEAR, dtype=jnp.int32)[:, None] - NEAR
             - jnp.arange(cq, dtype=jnp.int32)[None, :])
    gt = _bias_tiles(rel_bias, _bucket(rel_a), 0, 4, LOG2E)
    rel_b = jnp.arange(3 * BLOCK, dtype=jnp.int32)[None, :] - BLOCK - jnp.arange(BLOCK, dtype=jnp.int32)[:, None]
    idx_b = jnp.where(jnp.abs(rel_b) <= WINDOW, _bucket(rel_b), -1)
    wbias = _bias_tiles(rel_bias, idx_b, 4, 4, 1.0)

    w_in_p = _prep_w_in(w_in)
    cq_g = jnp.tile(ax_q_norm, (1, 4))[:, None, :]
    ck_g = jnp.tile(ax_k_norm, (1, 2))[:, None, :]
    wuq = jnp.pad(mla_w_uq.reshape(depth, MLA_Q_RANK, 4, MLA_NOPE + MLA_ROPE),
                  ((0, 0), (0, 0), (0, 0), (0, LANES - MLA_NOPE - MLA_ROPE))
                  ).reshape(depth, MLA_Q_RANK, 4 * LANES).astype(bf)
    wukv = mla_w_ukv.reshape(depth, MLA_KV_RANK, 4, 2, HEAD_DIM)
    wukvk = jnp.pad(wukv[:, :, :, 0], ((0, 0), (0, 0), (0, 0), (0, HEAD_DIM))
                    ).reshape(depth, MLA_KV_RANK, 4 * LANES).astype(bf)
    wukvv = wukv[:, :, :, 1].reshape(depth, MLA_KV_RANK, GROUP_WIDTH).astype(bf)
    wo = w_out.astype(bf).reshape(depth, 4, 4, HEAD_DIM, D_MODEL)
    slots = [(0, h) for h in range(4)] + [(1, h) for h in _GQA_HEAD_ORDER] \
        + [(2, h) for h in _GQA_HEAD_ORDER] + [(3, h) for h in range(4)]
    wo = jnp.stack([wo[:, m, h] for m, h in slots], axis=1).reshape(depth, D_MODEL, D_MODEL)
    wup = ffn_w_up.astype(bf)
    wdn = ffn_w_down.astype(bf)
    sub = jnp.tile(diff_subln, (1, 2))[:, None, :]

    x2 = x.reshape(n, D_MODEL)
    r3 = lambda a: a.reshape(bsz, seq, a.shape[-1])
    for l in range(depth):
        lam_init = 0.8 - 0.6 * math.exp(-0.3 * l)
        (aq, ak, avt, bq, bk, bv, cq_, ck, cvt, dq, dk, dvt) = _pre(
            x2, attn_pre_norm[l][None], w_in_p[l], cq_g[l], ck_g[l], ctab,
            mla_q_norm[l][None], mla_kv_norm[l][None], wuq[l], wukvk[l], wukvv[l], dtab, seq, tm)
        oa = _diff_attention(r3(aq), r3(ak), avt, gt, diff_lambda[l][None], sub[l], lam_init,
                             tq_a, cq)
        ob = _win_attention(r3(bq), r3(bk), r3(bv), win_sink[l], wbias, tw)
        oc = _gqa_attention(r3(cq_), r3(ck), cvt, tq, cq)
        od = _mla_attention(r3(dq), r3(dk), dvt, tq, cq)
        x2 = _outproj(x2, oa.reshape(n, -1), ob.reshape(n, -1), oc.reshape(n, -1),
                      od.reshape(n, -1), wo[l], attn_post_norm[l][None], to)
        x2 = _ffn(x2, ffn_pre_norm[l][None], wup[l], ffn_conv_w[l], ffn_conv_b[l][None],
                  wdn[l], ffn_post_norm[l][None], seq, tf, fc)
    return x2.reshape(bsz, seq, D_MODEL)


def kernel(x, rel_bias, attn_pre_norm, w_in, diff_lambda, diff_subln, win_sink, ax_q_norm, ax_k_norm,
           mla_q_norm, mla_kv_norm, mla_w_uq, mla_w_ukv, w_out, attn_post_norm, ffn_pre_norm,
           ffn_w_up, ffn_conv_w, ffn_conv_b, ffn_w_down, ffn_post_norm):
    return _forward(x, rel_bias, attn_pre_norm, w_in, diff_lambda, diff_subln, win_sink, ax_q_norm,
                    ax_k_norm, mla_q_norm, mla_kv_norm, mla_w_uq, mla_w_ukv, w_out, attn_post_norm,
                    ffn_pre_norm, ffn_w_up, ffn_conv_w, ffn_conv_b, ffn_w_down, ffn_post_norm,
                    tq_a=512, tq=1024, cq=256, tw=1024, tm=512, to=1024, tf=512, fc=256)
```

```python
import functools
import math

import jax
import jax.numpy as jnp
from jax import lax
from jax.experimental import pallas as pl
from jax.experimental.pallas import tpu as pltpu

D_MODEL = 1024
HEAD_DIM = 64
GROUP_WIDTH = 256
BLOCK = 128
DIFF_HALF = 32
WINDOW = 128
GRID_W = 64
ROPE_THETA = 10000.0
MLA_Q_RANK = 256
MLA_KV_RANK = 128
MLA_NOPE = 64
MLA_ROPE = 32
REL_BUCKETS = 32
REL_MAX_DIST = 128
D_FF = 2816
EPS = 1e-6
LANES = 128
LOG2E = 1.4426950408889634
NEG_BIG = -1e30
VMEM_LIMIT = 56 * 1024 * 1024
NEAR = 2 * BLOCK


def _layer_spec(a, layer, **kw):
    return pl.BlockSpec((None,) + a.shape[1:], lambda *_: (layer,) + (0,) * (a.ndim - 1), **kw)


def _cparams(sem):
    return pltpu.CompilerParams(dimension_semantics=sem, vmem_limit_bytes=VMEM_LIMIT)


def _rms(x, g):
    return x * lax.rsqrt(jnp.mean(x * x, axis=-1, keepdims=True) + EPS) * g


def _group_rms(x, ngroups):
    grp = lax.broadcasted_iota(jnp.int32, x.shape, 1) // HEAD_DIM
    x2 = x * x
    inv = jnp.zeros_like(x)
    for h in range(ngroups):
        m = grp == h
        ss = jnp.sum(jnp.where(m, x2, 0.0), axis=-1, keepdims=True)
        inv = jnp.where(m, lax.rsqrt(ss * (1.0 / HEAD_DIM) + EPS), inv)
    return x * inv


def _rope(x, c, sn, sp, half):
    w = x.shape[-1]
    return x * c + pltpu.roll(x, w - half, 1) * sn + pltpu.roll(x, half, 1) * sp


def _bias_kernel(tab_ref, idx_ref, o_ref, *, head0, scale):
    h = pl.program_id(0)
    idx = idx_ref[...]
    acc = jnp.zeros(idx.shape, jnp.float32)
    for b in range(REL_BUCKETS):
        acc = jnp.where(idx == b, tab_ref[b, head0 + h], acc)
    o_ref[0] = jnp.where(idx < 0, NEG_BIG, acc * scale)


def _bias_tiles(rel_bias, idx, head0, nheads, scale):
    r, c = idx.shape
    return pl.pallas_call(
        functools.partial(_bias_kernel, head0=head0, scale=scale),
        grid=(nheads,),
        in_specs=[pl.BlockSpec(memory_space=pltpu.SMEM),
                  pl.BlockSpec((r, c), lambda h: (0, 0))],
        out_specs=pl.BlockSpec((1, r, c), lambda h: (h, 0, 0)),
        out_shape=jax.ShapeDtypeStruct((nheads, r, c), jnp.float32),
        compiler_params=_cparams(("arbitrary",)),
        name="rel_bias_tiles",
    )(rel_bias, idx)


def _bucket(rel):
    half = REL_BUCKETS // 2
    max_exact = half // 2
    n = jnp.abs(rel)
    nf = jnp.maximum(n, 1).astype(jnp.float32)
    large = max_exact + (jnp.log(nf / max_exact) / math.log(REL_MAX_DIST / max_exact)
                         * (half - max_exact)).astype(jnp.int32)
    large = jnp.minimum(large, half - 1)
    return jnp.where(rel > 0, half, 0) + jnp.where(n < max_exact, n, large)


def _pre_kernel(x_ref, g_ref, w_ref, cq_g_ref, ck_g_ref, ctab_ref, dq_g_ref, dkv_g_ref,
                wuq_ref, wukvk_ref, wukvv_ref, dtab_ref,
                aq_ref, ak_ref, avt_ref, bq_ref, bk_ref, bv_ref, cq_ref, ck_ref, cvt_ref,
                dq_ref, dk_ref, dvt_ref):
    bf = jnp.bfloat16
    tm = x_ref.shape[0]
    rep = lambda t: jnp.concatenate([t] * 4, axis=1)

    def epi_c(r, p):
        cc, csn, csp = ctab_ref[0, r], ctab_ref[1, r], ctab_ref[2, r]
        q = _group_rms(p[:, 0:256], 4) * cq_g_ref[...]
        cq_ref[r, :] = (_rope(q, cc, csn, csp, 16) * (HEAD_DIM ** -0.5 * LOG2E)).astype(bf)
        k = _group_rms(p[:, 256:384], 2) * ck_g_ref[...]
        ck_ref[r, :] = _rope(k, cc[:, :128], csn[:, :128], csp[:, :128], 16).astype(bf)
        cvt_ref[:, r] = p[:, 384:512].T.astype(bf)

    def epi_d(r, p):
        dc, dsn, dsp = dtab_ref[0, r], dtab_ref[1, r], dtab_ref[2, r]
        cq = _rms(p[:, 0:256], dq_g_ref[...]).astype(bf)
        q = jnp.dot(cq, wuq_ref[...], preferred_element_type=jnp.float32)
        q = _rope(q, rep(dc), rep(dsn), rep(dsp), 16)
        dq_ref[r, :] = (q * ((MLA_NOPE + MLA_ROPE) ** -0.5 * LOG2E)).astype(bf)
        ckv = _rms(p[:, 256:384], dkv_g_ref[...]).astype(bf)
        kr = _rope(p[:, 384:512], dc, dsn, dsp, 16)
        kn = jnp.dot(ckv, wukvk_ref[...], preferred_element_type=jnp.float32)
        dk_ref[r, :] = (kn + rep(kr)).astype(bf)
        dvt_ref[:, r] = jnp.dot(ckv, wukvv_ref[...], preferred_element_type=jnp.float32).T.astype(bf)

    def epi_a(r, p):
        aq_ref[r, :] = (p[:, 0:256] * (DIFF_HALF ** -0.5 * LOG2E)).astype(bf)
        ak_ref[r, :] = p[:, 256:512].astype(bf)
        avt_ref[:, r] = p[:, 512:768].T.astype(bf)

    def epi_b(r, p):
        bq_ref[r, :] = (p[:, 0:256] * 0.125).astype(bf)
        bk_ref[r, :] = p[:, 256:384].astype(bf)
        bv_ref[r, :] = p[:, 384:512].astype(bf)

    mixers = [((1280, 1792), epi_c), ((1792, 2304), epi_d), ((0, 768), epi_a), ((768, 1280), epi_b)]
    halves = [slice(0, tm // 2), slice(tm // 2, tm)]
    pending = None
    for r in halves:
        h = _rms(x_ref[r, :], g_ref[...]).astype(bf)
        for (lo, hi), epilogue in mixers:
            p = jnp.dot(h, w_ref[:, lo:hi], preferred_element_type=jnp.float32)
            if pending is not None:
                pending[0](pending[1], pending[2])
            pending = (epilogue, r, p)
    pending[0](pending[1], pending[2])


def _pre(x2, g, w, cq_g, ck_g, ctab, dq_g, dkv_g, wuq, wukvk, wukvv, dtab, seq, tm, layer):
    n = x2.shape[0]
    nt = seq // tm
    row = lambda wd: pl.BlockSpec((tm, wd), lambda i: (i, 0))
    col = lambda wd: pl.BlockSpec((wd, tm), lambda i: (0, i))
    full = lambda a: pl.BlockSpec(a.shape, lambda i: (0,) * a.ndim)
    bf = jnp.bfloat16
    rows = lambda wd: (row(wd), jax.ShapeDtypeStruct((n, wd), bf))
    cols = lambda wd: (col(wd), jax.ShapeDtypeStruct((wd, n), bf))
    outs = [rows(256), rows(256), cols(256), rows(256), rows(128), rows(128),
            rows(256), rows(128), cols(128), rows(512), rows(512), cols(256)]
    return pl.pallas_call(
        _pre_kernel,
        grid=(n // tm,),
        in_specs=[row(D_MODEL), full(g), _layer_spec(w, layer), full(cq_g), full(ck_g),
                  pl.BlockSpec((3, tm, 256), lambda i: (0, i % nt, 0)),
                  full(dq_g), full(dkv_g), full(wuq), full(wukvk), full(wukvv),
                  pl.BlockSpec((3, tm, 128), lambda i: (0, i % nt, 0))],
        out_specs=[o[0] for o in outs],
        out_shape=[o[1] for o in outs],
        compiler_params=_cparams(("parallel",)),
        name="pre_attention",
    )(x2, g, w, cq_g, ck_g, ctab, dq_g, dkv_g, wuq, wukvk, wukvv, dtab)


_NT = (((1,), (1,)), ((), ()))


FLASH_KEYS = 512
FLASH_LAG = 4


def _attend_flash(chains):
    n = len(chains)
    nblk = chains[0][1].shape[0] // FLASH_KEYS
    fold = lambda a: a.reshape(a.shape[0] // 8, 8, a.shape[1])
    state = [None] * n

    def scores(b, c):
        q, k, _, bias = chains[c]
        rows = slice(b * FLASH_KEYS, (b + 1) * FLASH_KEYS)
        s = lax.dot_general(k[rows], q, _NT, preferred_element_type=jnp.float32)
        if bias is not None:
            per = FLASH_KEYS // BLOCK
            s = s + jnp.concatenate([bias(b * per + j) for j in range(per)], axis=0)
        return s

    def update(b, c, s):
        vt = chains[c][2][:, b * FLASH_KEYS:(b + 1) * FLASH_KEYS]
        mb = jnp.max(jnp.max(fold(s), axis=0), axis=0, keepdims=True)
        if state[c] is None:
            m = mb
            p = jnp.exp2(s - m)
            l = jnp.sum(jnp.sum(fold(p), axis=0), axis=0, keepdims=True)
            acc = jnp.dot(vt, p.astype(jnp.bfloat16), preferred_element_type=jnp.float32)
        else:
            m0, l0, acc0 = state[c]
            m = jnp.maximum(m0, mb)
            alpha = jnp.exp2(m0 - m)
            p = jnp.exp2(s - m)
            l = alpha * l0 + jnp.sum(jnp.sum(fold(p), axis=0), axis=0, keepdims=True)
            acc = alpha * acc0 + jnp.dot(vt, p.astype(jnp.bfloat16), preferred_element_type=jnp.float32)
        state[c] = (m, l, acc)

    units = [(b, c) for b in range(nblk) for c in range(n)]
    pending = []
    for u in units:
        pending.append((*u, scores(*u)))
        if len(pending) > FLASH_LAG:
            update(*pending.pop(0))
    for item in pending:
        update(*item)
    return [acc * (1.0 / l) for _, l, acc in state]


def _diff_kernel(q_ref, k_ref, vt_ref, gt_ref, dl_ref, sub_ref, o_ref, *, lam_init, seq, tq, cq):
    i = pl.program_id(1)
    lane = lax.broadcasted_iota(jnp.int32, (cq, LANES), 1)
    upper_rows = lax.broadcasted_iota(jnp.int32, (LANES, cq), 0) >= HEAD_DIM
    dl = dl_ref[0]
    lam = (jnp.exp(jnp.sum(dl[0:1] * dl[1:2], axis=-1, keepdims=True))
           - jnp.exp(jnp.sum(dl[2:3] * dl[3:4], axis=-1, keepdims=True)) + lam_init)
    chains = []
    for r0 in range(0, tq, cq):
        offs = [pl.multiple_of(jnp.clip(jb * BLOCK - (i * tq + r0), -NEAR, cq + BLOCK) + NEAR, BLOCK)
                for jb in range(seq // BLOCK)]
        for pr in range(2):
            pl_ = slice(pr * LANES, (pr + 1) * LANES)
            q, k, vt = q_ref[0, r0:r0 + cq, pl_], k_ref[0, :, pl_], vt_ref[pl_, :]
            for hh in range(2):
                bias_t = lambda jb, h=2 * pr + hh, offs=offs: gt_ref[h, pl.ds(offs[jb], BLOCK), :]
                for c in range(2):
                    lo = hh * HEAD_DIM + c * DIFF_HALF
                    qm = jnp.where((lane >= lo) & (lane < lo + DIFF_HALF), q, jnp.zeros_like(q))
                    chains.append((qm, k, vt, bias_t))
    outs = _attend_flash(chains)
    for n, r0 in enumerate(range(0, tq, cq)):
        for pr in range(2):
            heads = []
            for hh in range(2):
                first = 8 * n + 4 * pr + 2 * hh
                oh = outs[first] - lam * outs[first + 1]
                mh = upper_rows if hh else jnp.logical_not(upper_rows)
                ss = jnp.sum(jnp.where(mh, oh * oh, 0.0), axis=0, keepdims=True) * (1.0 / HEAD_DIM)
                heads.append(oh * lax.rsqrt(ss + EPS))
            out = jnp.where(upper_rows, heads[1], heads[0]).T * sub_ref[...] * (1.0 - lam_init)
            o_ref[0, r0:r0 + cq, pr * LANES:(pr + 1) * LANES] = out.astype(jnp.bfloat16)


def _diff_attention(q, k, vt, gt, dl, sub, lam_init, tq, cq):
    b, seq, _ = q.shape
    return pl.pallas_call(
        functools.partial(_diff_kernel, lam_init=lam_init, seq=seq, tq=tq, cq=cq),
        grid=(b, seq // tq),
        in_specs=[pl.BlockSpec((1, tq, GROUP_WIDTH), lambda bi, i: (bi, i, 0)),
                  pl.BlockSpec((1, seq, GROUP_WIDTH), lambda bi, i: (bi, 0, 0)),
                  pl.BlockSpec((GROUP_WIDTH, seq), lambda bi, i: (0, bi)),
                  pl.BlockSpec(gt.shape, lambda bi, i: (0, 0, 0)),
                  pl.BlockSpec((1, 4, DIFF_HALF), lambda bi, i: (0, 0, 0)),
                  pl.BlockSpec((1, LANES), lambda bi, i: (0, 0))],
        out_specs=pl.BlockSpec((1, tq, GROUP_WIDTH), lambda bi, i: (bi, i, 0)),
        out_shape=jax.ShapeDtypeStruct((b, seq, GROUP_WIDTH), jnp.bfloat16),
        compiler_params=_cparams(("parallel", "arbitrary")),
        name="diff_attention",
    )(q, k, vt, gt, dl, sub)


def _gqa_kernel(q_ref, k_ref, vt_ref, o_ref, *, tq, cq):
    k, vt = k_ref[0], vt_ref[...]
    lower = lax.broadcasted_iota(jnp.int32, (cq, LANES), 1) < HEAD_DIM
    upper_rows = lax.broadcasted_iota(jnp.int32, (LANES, cq), 0) >= HEAD_DIM
    chains = []
    for r0 in range(0, tq, cq):
        for j in range(2):
            q = q_ref[0, r0:r0 + cq, j * LANES:(j + 1) * LANES]
            zero = jnp.zeros_like(q)
            chains.append((jnp.where(lower, q, zero), k, vt, None))
            chains.append((jnp.where(lower, zero, q), k, vt, None))
    outs = _attend_flash(chains)
    for n, r0 in enumerate(range(0, tq, cq)):
        for j in range(2):
            lo, hi = outs[4 * n + 2 * j], outs[4 * n + 2 * j + 1]
            o_ref[0, r0:r0 + cq, j * LANES:(j + 1) * LANES] = jnp.where(
                upper_rows, hi, lo).T.astype(jnp.bfloat16)


def _gqa_attention(q, k, vt, tq, cq):
    b, seq, _ = q.shape
    return pl.pallas_call(
        functools.partial(_gqa_kernel, tq=tq, cq=cq),
        grid=(b, seq // tq),
        in_specs=[pl.BlockSpec((1, tq, GROUP_WIDTH), lambda bi, i: (bi, i, 0)),
                  pl.BlockSpec((1, seq, LANES), lambda bi, i: (bi, 0, 0)),
                  pl.BlockSpec((LANES, seq), lambda bi, i: (0, bi))],
        out_specs=pl.BlockSpec((1, tq, GROUP_WIDTH), lambda bi, i: (bi, i, 0)),
        out_shape=jax.ShapeDtypeStruct((b, seq, GROUP_WIDTH), jnp.bfloat16),
        compiler_params=_cparams(("parallel", "arbitrary")),
        name="axial_gqa_attention",
    )(q, k, vt)


def _mla_kernel(q_ref, k_ref, vt_ref, o_ref, *, tq, cq):
    upper_rows = lax.broadcasted_iota(jnp.int32, (LANES, cq), 0) >= HEAD_DIM
    chains = []
    for r0 in range(0, tq, cq):
        for hh in range(4):
            hl = slice(hh * LANES, (hh + 1) * LANES)
            vl = slice((hh // 2) * LANES, (hh // 2 + 1) * LANES)
            chains.append((q_ref[0, r0:r0 + cq, hl], k_ref[0, :, hl], vt_ref[vl, :], None))
    outs = _attend_flash(chains)
    for n, r0 in enumerate(range(0, tq, cq)):
        for pr in range(2):
            lo, hi = outs[4 * n + 2 * pr], outs[4 * n + 2 * pr + 1]
            o_ref[0, r0:r0 + cq, pr * LANES:(pr + 1) * LANES] = jnp.where(
                upper_rows, hi, lo).T.astype(jnp.bfloat16)


def _mla_attention(q, k, vt, tq, cq):
    b, seq, _ = q.shape
    return pl.pallas_call(
        functools.partial(_mla_kernel, tq=tq, cq=cq),
        grid=(b, seq // tq),
        in_specs=[pl.BlockSpec((1, tq, 4 * LANES), lambda bi, i: (bi, i, 0)),
                  pl.BlockSpec((1, seq, 4 * LANES), lambda bi, i: (bi, 0, 0)),
                  pl.BlockSpec((GROUP_WIDTH, seq), lambda bi, i: (0, bi))],
        out_specs=pl.BlockSpec((1, tq, GROUP_WIDTH), lambda bi, i: (bi, i, 0)),
        out_shape=jax.ShapeDtypeStruct((b, seq, GROUP_WIDTH), jnp.bfloat16),
        compiler_params=_cparams(("parallel", "arbitrary")),
        name="latent_attention",
    )(q, k, vt)


def _softmax_parts(s, extra):
    m = jnp.maximum(jnp.max(s, axis=-1, keepdims=True), extra)
    p = jnp.exp(s - m)
    l = jnp.sum(p, axis=-1, keepdims=True) + jnp.exp(extra - m)
    return p.astype(jnp.bfloat16), 1.0 / l


def _win_kernel(sink_ref, q_ref, kp_ref, kc_ref, kn_ref, vp_ref, vc_ref, vn_ref, wb_ref, o_ref,
                *, nstep, nb):
    n = pl.program_id(1)
    kext = jnp.concatenate([kp_ref[0], kc_ref[0], kn_ref[0]], axis=0)
    vext = jnp.concatenate([vp_ref[0], vc_ref[0], vn_ref[0]], axis=0)
    lane = lax.broadcasted_iota(jnp.int32, (BLOCK, LANES), 1)
    col = lax.broadcasted_iota(jnp.int32, (BLOCK, 3 * BLOCK), 1)
    units = [(blk, j) for blk in range(nb) for j in range(2)]

    def scores(blk, j):
        q = q_ref[0, blk * BLOCK:(blk + 1) * BLOCK, j * LANES:(j + 1) * LANES]
        zero = jnp.zeros_like(q)
        qs = jnp.concatenate([jnp.where(lane < HEAD_DIM, q, zero),
                              jnp.where(lane >= HEAD_DIM, q, zero)], axis=0)
        return lax.dot_general(qs, kext[blk * BLOCK:(blk + 3) * BLOCK], _NT,
                               preferred_element_type=jnp.float32)

    def softmax(blk, j, s):
        ps, rs = [], []
        for g in range(2):
            head = j + 2 * g
            sg = s[g * BLOCK:(g + 1) * BLOCK] + wb_ref[head]
            if blk == 0:
                sg = jnp.where((col < BLOCK) & (n == 0), NEG_BIG, sg)
            if blk == nb - 1:
                sg = jnp.where((col >= 2 * BLOCK) & (n == nstep - 1), NEG_BIG, sg)
            p, r = _softmax_parts(sg, sink_ref[head])
            ps.append(p)
            rs.append(r)
        return jnp.concatenate(ps, axis=0), rs

    def pv(blk, j, p, rs):
        o = jnp.dot(p, vext[blk * BLOCK:(blk + 3) * BLOCK], preferred_element_type=jnp.float32)
        out = jnp.where(lane < HEAD_DIM, o[:BLOCK] * rs[0], o[BLOCK:] * rs[1])
        o_ref[0, blk * BLOCK:(blk + 1) * BLOCK, j * LANES:(j + 1) * LANES] = out.astype(jnp.bfloat16)

    s, pr = {}, {}
    for t in range(len(units) + 2):
        if t < len(units):
            s[t] = scores(*units[t])
        if 0 <= t - 2 < len(units):
            pv(*units[t - 2], *pr.pop(t - 2))
        if 0 <= t - 1 < len(units):
            pr[t - 1] = softmax(*units[t - 1], s.pop(t - 1))


def _win_attention(q, k, v, sink, wbias, tw):
    b, seq, _ = q.shape
    nblk = seq // BLOCK
    nb = tw // BLOCK
    prev = lambda bi, n: (bi, jnp.maximum(n * nb - 1, 0), 0)
    cur = lambda bi, n: (bi, n, 0)
    nxt = lambda bi, n: (bi, jnp.minimum((n + 1) * nb, nblk - 1), 0)
    halo = lambda f: pl.BlockSpec((1, BLOCK, LANES), f)
    body = pl.BlockSpec((1, tw, LANES), cur)
    return pl.pallas_call(
        functools.partial(_win_kernel, nstep=seq // tw, nb=nb),
        grid=(b, seq // tw),
        in_specs=[pl.BlockSpec(memory_space=pltpu.SMEM),
                  pl.BlockSpec((1, tw, GROUP_WIDTH), cur),
                  halo(prev), body, halo(nxt), halo(prev), body, halo(nxt),
                  pl.BlockSpec((4, BLOCK, 3 * BLOCK), lambda bi, n: (0, 0, 0))],
        out_specs=pl.BlockSpec((1, tw, GROUP_WIDTH), cur),
        out_shape=jax.ShapeDtypeStruct((b, seq, GROUP_WIDTH), jnp.bfloat16),
        compiler_params=_cparams(("parallel", "arbitrary")),
        name="window_attention",
    )(sink, q, k, k, k, v, v, v, wbias)


def _outproj_kernel(x_ref, a_ref, b_ref, c_ref, d_ref, w_ref, g_ref, o_ref):
    tm = x_ref.shape[0]
    halves = [slice(0, tm // 2), slice(tm // 2, tm)]

    def project(r):
        mixed = jnp.concatenate([a_ref[r, :], b_ref[r, :], c_ref[r, :], d_ref[r, :]], axis=1)
        return jnp.dot(mixed, w_ref[...], preferred_element_type=jnp.float32)

    ys = [project(r) for r in halves]
    for r, y in zip(halves, ys):
        o_ref[r, :] = x_ref[r, :] + _rms(y, g_ref[...])


def _outproj(x2, oa, ob, oc, od, w, g, tm, layer):
    n = x2.shape[0]
    row = lambda wd: pl.BlockSpec((tm, wd), lambda i: (i, 0))
    return pl.pallas_call(
        _outproj_kernel,
        grid=(n // tm,),
        in_specs=[row(D_MODEL), row(256), row(256), row(256), row(256),
                  _layer_spec(w, layer),
                  pl.BlockSpec(g.shape, lambda i: (0, 0))],
        out_specs=row(D_MODEL),
        out_shape=jax.ShapeDtypeStruct((n, D_MODEL), jnp.float32),
        compiler_params=_cparams(("parallel",)),
        name="out_projection",
    )(x2, oa, ob, oc, od, w, g)


HALO = 8
DOWN_GROUP_ENDS = (5, 10, 11)


def _ffn_kernel(x_ref, xp_ref, xn_ref, gpre_ref, wup_ref, cw_ref, cb_ref, wdn_ref, gpost_ref,
                o_ref, h_ref, *, tm, fc, tiles_per_seq):
    i = pl.program_id(0)
    bf = jnp.bfloat16
    g = gpre_ref[...]
    first = (i % tiles_per_seq) == 0
    last = (i % tiles_per_seq) == tiles_per_seq - 1
    h_ref[0:HALO, :] = jnp.where(first, 0.0, _rms(xp_ref[...], g)).astype(bf)
    h_ref[HALO:HALO + tm, :] = _rms(x_ref[...], g).astype(bf)
    h_ref[HALO + tm:, :] = jnp.where(last, 0.0, _rms(xn_ref[...], g)).astype(bf)
    nf = D_FF // fc
    chunk = lambda c: slice(c * fc, (c + 1) * fc)

    def up(c):
        val_cols = slice(D_FF + c * fc, D_FF + (c + 1) * fc)
        gate = jnp.dot(h_ref[...], wup_ref[:, chunk(c)], preferred_element_type=jnp.float32)
        val = jnp.dot(h_ref[HALO:HALO + tm, :], wup_ref[:, val_cols], preferred_element_type=jnp.float32)
        return gate, val

    y = None
    acts, group_start = [], 0
    nxt = up(0)
    for c in range(nf):
        cs = chunk(c)
        gate, val = nxt
        if c + 1 < nf:
            nxt = up(c + 1)
        gate = (gate[HALO - 1:HALO - 1 + tm] * cw_ref[0:1, cs] + gate[HALO:HALO + tm] * cw_ref[1:2, cs]
                + gate[HALO + 1:HALO + 1 + tm] * cw_ref[2:3, cs] + cb_ref[:, cs])
        acts.append((0.5 * gate * (1.0 + lax.erf(gate * (2.0 ** -0.5))) * val).astype(bf))
        if c + 1 in DOWN_GROUP_ENDS:
            rows = slice(group_start * fc, (c + 1) * fc)
            part = jnp.dot(jnp.concatenate(acts, axis=1), wdn_ref[rows, :],
                           preferred_element_type=jnp.float32)
            y = part if y is None else y + part
            acts, group_start = [], c + 1
    o_ref[...] = x_ref[...] + _rms(y, gpost_ref[...])


def _ffn(x2, gpre, wup, cw, cb, wdn, gpost, seq, tm, fc, layer):
    n = x2.shape[0]
    tph = tm // HALO
    nh = n // HALO
    resident = lambda a: pl.BlockSpec(a.shape, lambda i: (0,) * a.ndim,
                                      pipeline_mode=pl.Buffered(1))
    return pl.pallas_call(
        functools.partial(_ffn_kernel, tm=tm, fc=fc, tiles_per_seq=seq // tm),
        grid=(n // tm,),
        in_specs=[pl.BlockSpec((tm, D_MODEL), lambda i: (i, 0)),
                  pl.BlockSpec((HALO, D_MODEL), lambda i: (jnp.maximum(i * tph - 1, 0), 0)),
                  pl.BlockSpec((HALO, D_MODEL), lambda i: (jnp.minimum((i + 1) * tph, nh - 1), 0)),
                  resident(gpre), _layer_spec(wup, layer, pipeline_mode=pl.Buffered(1)),
                  resident(cw), resident(cb),
                  _layer_spec(wdn, layer, pipeline_mode=pl.Buffered(1)), resident(gpost)],
        out_specs=pl.BlockSpec((tm, D_MODEL), lambda i: (i, 0)),
        out_shape=jax.ShapeDtypeStruct((n, D_MODEL), jnp.float32),
        scratch_shapes=[pltpu.VMEM((tm + 2 * HALO, D_MODEL), jnp.bfloat16)],
        compiler_params=_cparams(("parallel",)),
        name="conv_glu",
    )(x2, x2, x2, gpre, wup, cw, cb, wdn, gpost)


_GQA_HEAD_ORDER = (0, 2, 1, 3)


def _reorder_heads(w, order):
    lead = w.shape[:-1]
    nh = w.shape[-1] // HEAD_DIM
    w = w.reshape(lead + (nh, HEAD_DIM))
    return jnp.stack([w[..., o, :] for o in order], axis=-2).reshape(lead + (nh * HEAD_DIM,))


def _prep_w_in(w_in):
    w_in = w_in.astype(jnp.bfloat16)
    bq = _reorder_heads(w_in[..., 768:1024], _GQA_HEAD_ORDER)
    cq = _reorder_heads(w_in[..., 1280:1536], _GQA_HEAD_ORDER)
    ck = w_in[..., 1536:1664]
    kr = w_in[..., 2176:2208]
    z = lambda wd: jnp.zeros(w_in.shape[:-1] + (wd,), w_in.dtype)
    return jnp.concatenate(
        [w_in[..., 0:768], bq, w_in[..., 1024:1280], cq, ck, w_in[..., 1664:1792],
         w_in[..., 1792:2176], z(MLA_NOPE), kr, z(LANES - MLA_NOPE - MLA_ROPE)], axis=-1)


def _rope_tables(cos, sin, lead_ones, trail_ones):
    s = cos.shape[0]
    one = lambda wd: jnp.ones((s, wd), jnp.float32)
    zero = lambda wd: jnp.zeros((s, wd), jnp.float32)
    c = jnp.concatenate([one(lead_ones), cos, cos, one(trail_ones)], axis=1)
    sn = jnp.concatenate([zero(lead_ones), -sin, zero(cos.shape[1]), zero(trail_ones)], axis=1)
    sp = jnp.concatenate([zero(lead_ones), zero(cos.shape[1]), sin, zero(trail_ones)], axis=1)
    return jnp.stack([c, sn, sp])


def _angles(pos, dim):
    inv = ROPE_THETA ** (-jnp.arange(0, dim, 2, dtype=jnp.float32) / dim)
    ang = pos.astype(jnp.float32)[:, None] * inv[None, :]
    return jnp.cos(ang), jnp.sin(ang)


def _forward(x, rel_bias, attn_pre_norm, w_in, diff_lambda, diff_subln, win_sink, ax_q_norm, ax_k_norm,
             mla_q_norm, mla_kv_norm, mla_w_uq, mla_w_ukv, w_out, attn_post_norm, ffn_pre_norm,
             ffn_w_up, ffn_conv_w, ffn_conv_b, ffn_w_down, ffn_post_norm, *, tq_a, tq, cq, tw, tm, to, tf, fc):
    bsz, seq, _ = x.shape
    depth = w_in.shape[0]
    n = bsz * seq
    bf = jnp.bfloat16

    pos = jnp.arange(seq, dtype=jnp.int32)
    rows = seq // GRID_W
    row_ids = jnp.repeat(jnp.arange(rows, dtype=jnp.int32), GRID_W)
    col_ids = jnp.tile(jnp.arange(GRID_W, dtype=jnp.int32), rows)
    rcos, rsin = _angles(row_ids, HEAD_DIM // 2)
    ccos, csin = _angles(col_ids, HEAD_DIM // 2)
    ctab = jnp.concatenate([_rope_tables(rcos, rsin, 0, 0), _rope_tables(ccos, csin, 0, 0)], axis=2)
    ctab = jnp.tile(ctab, (1, 1, 4))
    mcos, msin = _angles(pos, MLA_ROPE)
    dtab = _rope_tables(mcos, msin, MLA_NOPE, LANES - MLA_NOPE - MLA_ROPE)

    rel_a = (jnp.arange(cq + 2 * NEAR, dtype=jnp.int32)[:, None] - NEAR
             - jnp.arange(cq, dtype=jnp.int32)[None, :])
    gt = _bias_tiles(rel_bias, _bucket(rel_a), 0, 4, LOG2E)
    rel_b = jnp.arange(3 * BLOCK, dtype=jnp.int32)[None, :] - BLOCK - jnp.arange(BLOCK, dtype=jnp.int32)[:, None]
    idx_b = jnp.where(jnp.abs(rel_b) <= WINDOW, _bucket(rel_b), -1)
    wbias = _bias_tiles(rel_bias, idx_b, 4, 4, 1.0)

    w_in_p = _prep_w_in(w_in)
    cq_g = jnp.tile(ax_q_norm, (1, 4))[:, None, :]
    ck_g = jnp.tile(ax_k_norm, (1, 2))[:, None, :]
    wuq = jnp.pad(mla_w_uq.reshape(depth, MLA_Q_RANK, 4, MLA_NOPE + MLA_ROPE),
                  ((0, 0), (0, 0), (0, 0), (0, LANES - MLA_NOPE - MLA_ROPE))
                  ).reshape(depth, MLA_Q_RANK, 4 * LANES).astype(bf)
    wukv = mla_w_ukv.reshape(depth, MLA_KV_RANK, 4, 2, HEAD_DIM)
    wukvk = jnp.pad(wukv[:, :, :, 0], ((0, 0), (0, 0), (0, 0), (0, HEAD_DIM))
                    ).reshape(depth, MLA_KV_RANK, 4 * LANES).astype(bf)
    wukvv = wukv[:, :, :, 1].reshape(depth, MLA_KV_RANK, GROUP_WIDTH).astype(bf)
    wo = w_out.astype(bf).reshape(depth, 4, 4, HEAD_DIM, D_MODEL)
    slots = [(0, h) for h in range(4)] + [(1, h) for h in _GQA_HEAD_ORDER] \
        + [(2, h) for h in _GQA_HEAD_ORDER] + [(3, h) for h in range(4)]
    wo = jnp.stack([wo[:, m, h] for m, h in slots], axis=1).reshape(depth, D_MODEL, D_MODEL)
    wup = ffn_w_up.astype(bf)
    wdn = ffn_w_down.astype(bf)
    sub = jnp.tile(diff_subln, (1, 2))[:, None, :]

    x2 = x.reshape(n, D_MODEL)
    r3 = lambda a: a.reshape(bsz, seq, a.shape[-1])
    for l in range(depth):
        lam_init = 0.8 - 0.6 * math.exp(-0.3 * l)
        (aq, ak, avt, bq, bk, bv, cq_, ck, cvt, dq, dk, dvt) = _pre(
            x2, attn_pre_norm[l][None], w_in_p, cq_g[l], ck_g[l], ctab,
            mla_q_norm[l][None], mla_kv_norm[l][None], wuq[l], wukvk[l], wukvv[l], dtab, seq, tm, l)
        oa = _diff_attention(r3(aq), r3(ak), avt, gt, diff_lambda[l][None], sub[l], lam_init,
                             tq_a, cq)
        ob = _win_attention(r3(bq), r3(bk), r3(bv), win_sink[l], wbias, tw)
        oc = _gqa_attention(r3(cq_), r3(ck), cvt, tq, cq)
        od = _mla_attention(r3(dq), r3(dk), dvt, tq, cq)
        x2 = _outproj(x2, oa.reshape(n, -1), ob.reshape(n, -1), oc.reshape(n, -1),
                      od.reshape(n, -1), wo, attn_post_norm[l][None], to, l)
        x2 = _ffn(x2, ffn_pre_norm[l][None], wup, ffn_conv_w[l], ffn_conv_b[l][None],
                  wdn, ffn_post_norm[l][None], seq, tf, fc, l)
    return x2.reshape(bsz, seq, D_MODEL)


def kernel(x, rel_bias, attn_pre_norm, w_in, diff_lambda, diff_subln, win_sink, ax_q_norm, ax_k_norm,
           mla_q_norm, mla_kv_norm, mla_w_uq, mla_w_ukv, w_out, attn_post_norm, ffn_pre_norm,
           ffn_w_up, ffn_conv_w, ffn_conv_b, ffn_w_down, ffn_post_norm):
    return _forward(x, rel_bias, attn_pre_norm, w_in, diff_lambda, diff_subln, win_sink, ax_q_norm,
                    ax_k_norm, mla_q_norm, mla_kv_norm, mla_w_uq, mla_w_ukv, w_out, attn_post_norm,
                    ffn_pre_norm, ffn_w_up, ffn_conv_w, ffn_conv_b, ffn_w_down, ffn_post_norm,
                    tq_a=512, tq=1024, cq=256, tw=1024, tm=512, to=1024, tf=512, fc=256)
```

```python
import functools
import math

import jax
import jax.numpy as jnp
from jax import lax
from jax.experimental import pallas as pl
from jax.experimental.pallas import tpu as pltpu

D_MODEL = 1024
HEAD_DIM = 64
GROUP_WIDTH = 256
BLOCK = 128
DIFF_HALF = 32
WINDOW = 128
GRID_W = 64
ROPE_THETA = 10000.0
MLA_Q_RANK = 256
MLA_KV_RANK = 128
MLA_NOPE = 64
MLA_ROPE = 32
REL_BUCKETS = 32
REL_MAX_DIST = 128
D_FF = 2816
EPS = 1e-6
LANES = 128
LOG2E = 1.4426950408889634
NEG_BIG = -1e30
VMEM_LIMIT = 56 * 1024 * 1024
NEAR = 2 * BLOCK


def _layer_spec(a, layer, **kw):
    return pl.BlockSpec((None,) + a.shape[1:], lambda *_: (layer,) + (0,) * (a.ndim - 1), **kw)


def _cparams(sem):
    return pltpu.CompilerParams(dimension_semantics=sem, vmem_limit_bytes=VMEM_LIMIT)


def _rms(x, g):
    return x * lax.rsqrt(jnp.mean(x * x, axis=-1, keepdims=True) + EPS) * g


def _group_rms(x, ngroups):
    grp = lax.broadcasted_iota(jnp.int32, x.shape, 1) // HEAD_DIM
    x2 = x * x
    inv = jnp.zeros_like(x)
    for h in range(ngroups):
        m = grp == h
        ss = jnp.sum(jnp.where(m, x2, 0.0), axis=-1, keepdims=True)
        inv = jnp.where(m, lax.rsqrt(ss * (1.0 / HEAD_DIM) + EPS), inv)
    return x * inv


def _rope(x, c, sn, sp, half):
    w = x.shape[-1]
    return x * c + pltpu.roll(x, w - half, 1) * sn + pltpu.roll(x, half, 1) * sp


def _bias_kernel(tab_ref, idx_ref, o_ref, *, head0, scale):
    h = pl.program_id(0)
    idx = idx_ref[...]
    acc = jnp.zeros(idx.shape, jnp.float32)
    for b in range(REL_BUCKETS):
        acc = jnp.where(idx == b, tab_ref[b, head0 + h], acc)
    o_ref[0] = jnp.where(idx < 0, NEG_BIG, acc * scale)


def _bias_tiles(rel_bias, idx, head0, nheads, scale):
    r, c = idx.shape
    return pl.pallas_call(
        functools.partial(_bias_kernel, head0=head0, scale=scale),
        grid=(nheads,),
        in_specs=[pl.BlockSpec(memory_space=pltpu.SMEM),
                  pl.BlockSpec((r, c), lambda h: (0, 0))],
        out_specs=pl.BlockSpec((1, r, c), lambda h: (h, 0, 0)),
        out_shape=jax.ShapeDtypeStruct((nheads, r, c), jnp.float32),
        compiler_params=_cparams(("arbitrary",)),
        name="rel_bias_tiles",
    )(rel_bias, idx)


def _bucket(rel):
    half = REL_BUCKETS // 2
    max_exact = half // 2
    n = jnp.abs(rel)
    nf = jnp.maximum(n, 1).astype(jnp.float32)
    large = max_exact + (jnp.log(nf / max_exact) / math.log(REL_MAX_DIST / max_exact)
                         * (half - max_exact)).astype(jnp.int32)
    large = jnp.minimum(large, half - 1)
    return jnp.where(rel > 0, half, 0) + jnp.where(n < max_exact, n, large)


def _pre_kernel(x_ref, g_ref, w_ref, cq_g_ref, ck_g_ref, ctab_ref, dq_g_ref, dkv_g_ref,
                wuq_ref, wukvk_ref, wukvv_ref, dtab_ref,
                aq_ref, ak_ref, avt_ref, bq_ref, bk_ref, bv_ref, cq_ref, ck_ref, cvt_ref,
                dq_ref, dk_ref, dvt_ref):
    bf = jnp.bfloat16
    tm = x_ref.shape[0]
    rep = lambda t: jnp.concatenate([t] * 4, axis=1)

    def epi_c(r, p):
        cc, csn, csp = ctab_ref[0, r], ctab_ref[1, r], ctab_ref[2, r]
        q = _group_rms(p[:, 0:256], 4) * cq_g_ref[...]
        cq_ref[r, :] = (_rope(q, cc, csn, csp, 16) * (HEAD_DIM ** -0.5 * LOG2E)).astype(bf)
        k = _group_rms(p[:, 256:384], 2) * ck_g_ref[...]
        ck_ref[r, :] = _rope(k, cc[:, :128], csn[:, :128], csp[:, :128], 16).astype(bf)
        cvt_ref[:, r] = p[:, 384:512].T.astype(bf)

    def epi_d(r, p):
        dc, dsn, dsp = dtab_ref[0, r], dtab_ref[1, r], dtab_ref[2, r]
        cq = _rms(p[:, 0:256], dq_g_ref[...]).astype(bf)
        q = jnp.dot(cq, wuq_ref[...], preferred_element_type=jnp.float32)
        q = _rope(q, rep(dc), rep(dsn), rep(dsp), 16)
        dq_ref[r, :] = (q * ((MLA_NOPE + MLA_ROPE) ** -0.5 * LOG2E)).astype(bf)
        ckv = _rms(p[:, 256:384], dkv_g_ref[...]).astype(bf)
        kr = _rope(p[:, 384:512], dc, dsn, dsp, 16)
        kn = jnp.dot(ckv, wukvk_ref[...], preferred_element_type=jnp.float32)
        dk_ref[r, :] = (kn + rep(kr)).astype(bf)
        dvt_ref[:, r] = jnp.dot(ckv, wukvv_ref[...], preferred_element_type=jnp.float32).T.astype(bf)

    def epi_a(r, p):
        aq_ref[r, :] = (p[:, 0:256] * (DIFF_HALF ** -0.5 * LOG2E)).astype(bf)
        ak_ref[r, :] = p[:, 256:512].astype(bf)
        avt_ref[:, r] = p[:, 512:768].T.astype(bf)

    def epi_b(r, p):
        bq_ref[r, :] = (p[:, 0:256] * 0.125).astype(bf)
        bk_ref[r, :] = p[:, 256:384].astype(bf)
        bv_ref[r, :] = p[:, 384:512].astype(bf)

    mixers = [((1280, 1792), epi_c), ((1792, 2304), epi_d), ((0, 768), epi_a), ((768, 1280), epi_b)]
    halves = [slice(0, tm // 2), slice(tm // 2, tm)]
    pending = None
    for r in halves:
        h = _rms(x_ref[r, :], g_ref[...]).astype(bf)
        for (lo, hi), epilogue in mixers:
            p = jnp.dot(h, w_ref[:, lo:hi], preferred_element_type=jnp.float32)
            if pending is not None:
                pending[0](pending[1], pending[2])
            pending = (epilogue, r, p)
    pending[0](pending[1], pending[2])


def _pre(x2, g, w, cq_g, ck_g, ctab, dq_g, dkv_g, wuq, wukvk, wukvv, dtab, seq, tm, layer):
    n = x2.shape[0]
    nt = seq // tm
    row = lambda wd: pl.BlockSpec((tm, wd), lambda i: (i, 0))
    col = lambda wd: pl.BlockSpec((wd, tm), lambda i: (0, i))
    full = lambda a: pl.BlockSpec(a.shape, lambda i: (0,) * a.ndim)
    bf = jnp.bfloat16
    rows = lambda wd: (row(wd), jax.ShapeDtypeStruct((n, wd), bf))
    cols = lambda wd: (col(wd), jax.ShapeDtypeStruct((wd, n), bf))
    outs = [rows(256), rows(256), cols(256), rows(256), rows(128), rows(128),
            rows(256), rows(128), cols(128), rows(512), rows(512), cols(256)]
    return pl.pallas_call(
        _pre_kernel,
        grid=(n // tm,),
        in_specs=[row(D_MODEL), full(g), _layer_spec(w, layer), full(cq_g), full(ck_g),
                  pl.BlockSpec((3, tm, 256), lambda i: (0, i % nt, 0)),
                  full(dq_g), full(dkv_g), full(wuq), full(wukvk), full(wukvv),
                  pl.BlockSpec((3, tm, 128), lambda i: (0, i % nt, 0))],
        out_specs=[o[0] for o in outs],
        out_shape=[o[1] for o in outs],
        compiler_params=_cparams(("parallel",)),
        name="pre_attention",
    )(x2, g, w, cq_g, ck_g, ctab, dq_g, dkv_g, wuq, wukvk, wukvv, dtab)


_NT = (((1,), (1,)), ((), ()))


FLASH_KEYS = 512
FLASH_LAG = 4


def _attend_flash(chains):
    n = len(chains)
    nblk = chains[0][1].shape[0] // FLASH_KEYS
    fold = lambda a: a.reshape(a.shape[0] // 8, 8, a.shape[1])
    state = [None] * n

    def scores(b, c):
        q, k, _, bias = chains[c]
        rows = slice(b * FLASH_KEYS, (b + 1) * FLASH_KEYS)
        s = lax.dot_general(k[rows], q, _NT, preferred_element_type=jnp.float32)
        if bias is not None:
            per = FLASH_KEYS // BLOCK
            s = s + jnp.concatenate([bias(b * per + j) for j in range(per)], axis=0)
        return s

    def update(b, c, s):
        vt = chains[c][2][:, b * FLASH_KEYS:(b + 1) * FLASH_KEYS]
        mb = jnp.max(jnp.max(fold(s), axis=0), axis=0, keepdims=True)
        if state[c] is None:
            m = mb
            p = jnp.exp2(s - m)
            l = jnp.sum(jnp.sum(fold(p), axis=0), axis=0, keepdims=True)
            acc = jnp.dot(vt, p.astype(jnp.bfloat16), preferred_element_type=jnp.float32)
        else:
            m0, l0, acc0 = state[c]
            m = jnp.maximum(m0, mb)
            alpha = jnp.exp2(m0 - m)
            p = jnp.exp2(s - m)
            l = alpha * l0 + jnp.sum(jnp.sum(fold(p), axis=0), axis=0, keepdims=True)
            acc = alpha * acc0 + jnp.dot(vt, p.astype(jnp.bfloat16), preferred_element_type=jnp.float32)
        state[c] = (m, l, acc)

    units = [(b, c) for b in range(nblk) for c in range(n)]
    pending = []
    for u in units:
        pending.append((*u, scores(*u)))
        if len(pending) > FLASH_LAG:
            update(*pending.pop(0))
    for item in pending:
        update(*item)
    return [acc * (1.0 / l) for _, l, acc in state]


def _diff_kernel(q_ref, k_ref, vt_ref, gt_ref, dl_ref, sub_ref, o_ref, *, lam_init, seq, tq, cq):
    i = pl.program_id(1)
    lane = lax.broadcasted_iota(jnp.int32, (cq, LANES), 1)
    upper_rows = lax.broadcasted_iota(jnp.int32, (LANES, cq), 0) >= HEAD_DIM
    dl = dl_ref[0]
    lam = (jnp.exp(jnp.sum(dl[0:1] * dl[1:2], axis=-1, keepdims=True))
           - jnp.exp(jnp.sum(dl[2:3] * dl[3:4], axis=-1, keepdims=True)) + lam_init)
    chains = []
    for r0 in range(0, tq, cq):
        offs = [pl.multiple_of(jnp.clip(jb * BLOCK - (i * tq + r0), -NEAR, cq + BLOCK) + NEAR, BLOCK)
                for jb in range(seq // BLOCK)]
        for pr in range(2):
            pl_ = slice(pr * LANES, (pr + 1) * LANES)
            q, k, vt = q_ref[0, r0:r0 + cq, pl_], k_ref[0, :, pl_], vt_ref[pl_, :]
            for hh in range(2):
                bias_t = lambda jb, h=2 * pr + hh, offs=offs: gt_ref[h, pl.ds(offs[jb], BLOCK), :]
                for c in range(2):
                    lo = hh * HEAD_DIM + c * DIFF_HALF
                    qm = jnp.where((lane >= lo) & (lane < lo + DIFF_HALF), q, jnp.zeros_like(q))
                    chains.append((qm, k, vt, bias_t))
    outs = _attend_flash(chains)
    for n, r0 in enumerate(range(0, tq, cq)):
        for pr in range(2):
            heads = []
            for hh in range(2):
                first = 8 * n + 4 * pr + 2 * hh
                oh = outs[first] - lam * outs[first + 1]
                mh = upper_rows if hh else jnp.logical_not(upper_rows)
                ss = jnp.sum(jnp.where(mh, oh * oh, 0.0), axis=0, keepdims=True) * (1.0 / HEAD_DIM)
                heads.append(oh * lax.rsqrt(ss + EPS))
            out = jnp.where(upper_rows, heads[1], heads[0]).T * sub_ref[...] * (1.0 - lam_init)
            o_ref[0, r0:r0 + cq, pr * LANES:(pr + 1) * LANES] = out.astype(jnp.bfloat16)


def _diff_attention(q, k, vt, gt, dl, sub, lam_init, tq, cq):
    b, seq, _ = q.shape
    return pl.pallas_call(
        functools.partial(_diff_kernel, lam_init=lam_init, seq=seq, tq=tq, cq=cq),
        grid=(b, seq // tq),
        in_specs=[pl.BlockSpec((1, tq, GROUP_WIDTH), lambda bi, i: (bi, i, 0)),
                  pl.BlockSpec((1, seq, GROUP_WIDTH), lambda bi, i: (bi, 0, 0)),
                  pl.BlockSpec((GROUP_WIDTH, seq), lambda bi, i: (0, bi)),
                  pl.BlockSpec(gt.shape, lambda bi, i: (0, 0, 0)),
                  pl.BlockSpec((1, 4, DIFF_HALF), lambda bi, i: (0, 0, 0)),
                  pl.BlockSpec((1, LANES), lambda bi, i: (0, 0))],
        out_specs=pl.BlockSpec((1, tq, GROUP_WIDTH), lambda bi, i: (bi, i, 0)),
        out_shape=jax.ShapeDtypeStruct((b, seq, GROUP_WIDTH), jnp.bfloat16),
        compiler_params=_cparams(("parallel", "arbitrary")),
        name="diff_attention",
    )(q, k, vt, gt, dl, sub)


def _gqa_kernel(q_ref, k_ref, vt_ref, o_ref, *, tq, cq):
    k, vt = k_ref[0], vt_ref[...]
    lower = lax.broadcasted_iota(jnp.int32, (cq, LANES), 1) < HEAD_DIM
    upper_rows = lax.broadcasted_iota(jnp.int32, (LANES, cq), 0) >= HEAD_DIM
    chains = []
    for r0 in range(0, tq, cq):
        for j in range(2):
            q = q_ref[0, r0:r0 + cq, j * LANES:(j + 1) * LANES]
            zero = jnp.zeros_like(q)
            chains.append((jnp.where(lower, q, zero), k, vt, None))
            chains.append((jnp.where(lower, zero, q), k, vt, None))
    outs = _attend_flash(chains)
    for n, r0 in enumerate(range(0, tq, cq)):
        for j in range(2):
            lo, hi = outs[4 * n + 2 * j], outs[4 * n + 2 * j + 1]
            o_ref[0, r0:r0 + cq, j * LANES:(j + 1) * LANES] = jnp.where(
                upper_rows, hi, lo).T.astype(jnp.bfloat16)


def _gqa_attention(q, k, vt, tq, cq):
    b, seq, _ = q.shape
    return pl.pallas_call(
        functools.partial(_gqa_kernel, tq=tq, cq=cq),
        grid=(b, seq // tq),
        in_specs=[pl.BlockSpec((1, tq, GROUP_WIDTH), lambda bi, i: (bi, i, 0)),
                  pl.BlockSpec((1, seq, LANES), lambda bi, i: (bi, 0, 0)),
                  pl.BlockSpec((LANES, seq), lambda bi, i: (0, bi))],
        out_specs=pl.BlockSpec((1, tq, GROUP_WIDTH), lambda bi, i: (bi, i, 0)),
        out_shape=jax.ShapeDtypeStruct((b, seq, GROUP_WIDTH), jnp.bfloat16),
        compiler_params=_cparams(("parallel", "arbitrary")),
        name="axial_gqa_attention",
    )(q, k, vt)


def _mla_kernel(q_ref, k_ref, vt_ref, o_ref, *, tq, cq):
    upper_rows = lax.broadcasted_iota(jnp.int32, (LANES, cq), 0) >= HEAD_DIM
    chains = []
    for r0 in range(0, tq, cq):
        for hh in range(4):
            hl = slice(hh * LANES, (hh + 1) * LANES)
            vl = slice((hh // 2) * LANES, (hh // 2 + 1) * LANES)
            chains.append((q_ref[0, r0:r0 + cq, hl], k_ref[0, :, hl], vt_ref[vl, :], None))
    outs = _attend_flash(chains)
    for n, r0 in enumerate(range(0, tq, cq)):
        for pr in range(2):
            lo, hi = outs[4 * n + 2 * pr], outs[4 * n + 2 * pr + 1]
            o_ref[0, r0:r0 + cq, pr * LANES:(pr + 1) * LANES] = jnp.where(
                upper_rows, hi, lo).T.astype(jnp.bfloat16)


def _mla_attention(q, k, vt, tq, cq):
    b, seq, _ = q.shape
    return pl.pallas_call(
        functools.partial(_mla_kernel, tq=tq, cq=cq),
        grid=(b, seq // tq),
        in_specs=[pl.BlockSpec((1, tq, 4 * LANES), lambda bi, i: (bi, i, 0)),
                  pl.BlockSpec((1, seq, 4 * LANES), lambda bi, i: (bi, 0, 0)),
                  pl.BlockSpec((GROUP_WIDTH, seq), lambda bi, i: (0, bi))],
        out_specs=pl.BlockSpec((1, tq, GROUP_WIDTH), lambda bi, i: (bi, i, 0)),
        out_shape=jax.ShapeDtypeStruct((b, seq, GROUP_WIDTH), jnp.bfloat16),
        compiler_params=_cparams(("parallel", "arbitrary")),
        name="latent_attention",
    )(q, k, vt)


def _softmax_parts(s, extra):
    m = jnp.maximum(jnp.max(s, axis=-1, keepdims=True), extra)
    p = jnp.exp(s - m)
    l = jnp.sum(p, axis=-1, keepdims=True) + jnp.exp(extra - m)
    return p.astype(jnp.bfloat16), 1.0 / l


def _win_kernel(sink_ref, q_ref, kp_ref, kc_ref, kn_ref, vp_ref, vc_ref, vn_ref, wb_ref, o_ref,
                *, nstep, nb):
    n = pl.program_id(1)
    kext = jnp.concatenate([kp_ref[0], kc_ref[0], kn_ref[0]], axis=0)
    vext = jnp.concatenate([vp_ref[0], vc_ref[0], vn_ref[0]], axis=0)
    lane = lax.broadcasted_iota(jnp.int32, (BLOCK, LANES), 1)
    col = lax.broadcasted_iota(jnp.int32, (BLOCK, 3 * BLOCK), 1)
    units = [(blk, j) for blk in range(nb) for j in range(2)]

    def scores(blk, j):
        q = q_ref[0, blk * BLOCK:(blk + 1) * BLOCK, j * LANES:(j + 1) * LANES]
        zero = jnp.zeros_like(q)
        qs = jnp.concatenate([jnp.where(lane < HEAD_DIM, q, zero),
                              jnp.where(lane >= HEAD_DIM, q, zero)], axis=0)
        return lax.dot_general(qs, kext[blk * BLOCK:(blk + 3) * BLOCK], _NT,
                               preferred_element_type=jnp.float32)

    def softmax(blk, j, s):
        ps, rs = [], []
        for g in range(2):
            head = j + 2 * g
            sg = s[g * BLOCK:(g + 1) * BLOCK] + wb_ref[head]
            if blk == 0:
                sg = jnp.where((col < BLOCK) & (n == 0), NEG_BIG, sg)
            if blk == nb - 1:
                sg = jnp.where((col >= 2 * BLOCK) & (n == nstep - 1), NEG_BIG, sg)
            p, r = _softmax_parts(sg, sink_ref[head])
            ps.append(p)
            rs.append(r)
        return jnp.concatenate(ps, axis=0), rs

    def pv(blk, j, p, rs):
        o = jnp.dot(p, vext[blk * BLOCK:(blk + 3) * BLOCK], preferred_element_type=jnp.float32)
        out = jnp.where(lane < HEAD_DIM, o[:BLOCK] * rs[0], o[BLOCK:] * rs[1])
        o_ref[0, blk * BLOCK:(blk + 1) * BLOCK, j * LANES:(j + 1) * LANES] = out.astype(jnp.bfloat16)

    s, pr = {}, {}
    for t in range(len(units) + 2):
        if t < len(units):
            s[t] = scores(*units[t])
        if 0 <= t - 2 < len(units):
            pv(*units[t - 2], *pr.pop(t - 2))
        if 0 <= t - 1 < len(units):
            pr[t - 1] = softmax(*units[t - 1], s.pop(t - 1))


def _win_attention(q, k, v, sink, wbias, tw):
    b, seq, _ = q.shape
    nblk = seq // BLOCK
    nb = tw // BLOCK
    prev = lambda bi, n: (bi, jnp.maximum(n * nb - 1, 0), 0)
    cur = lambda bi, n: (bi, n, 0)
    nxt = lambda bi, n: (bi, jnp.minimum((n + 1) * nb, nblk - 1), 0)
    halo = lambda f: pl.BlockSpec((1, BLOCK, LANES), f)
    body = pl.BlockSpec((1, tw, LANES), cur)
    return pl.pallas_call(
        functools.partial(_win_kernel, nstep=seq // tw, nb=nb),
        grid=(b, seq // tw),
        in_specs=[pl.BlockSpec(memory_space=pltpu.SMEM),
                  pl.BlockSpec((1, tw, GROUP_WIDTH), cur),
                  halo(prev), body, halo(nxt), halo(prev), body, halo(nxt),
                  pl.BlockSpec((4, BLOCK, 3 * BLOCK), lambda bi, n: (0, 0, 0))],
        out_specs=pl.BlockSpec((1, tw, GROUP_WIDTH), cur),
        out_shape=jax.ShapeDtypeStruct((b, seq, GROUP_WIDTH), jnp.bfloat16),
        compiler_params=_cparams(("parallel", "arbitrary")),
        name="window_attention",
    )(sink, q, k, k, k, v, v, v, wbias)


def _outproj_kernel(x_ref, a_ref, b_ref, c_ref, d_ref, w_ref, g_ref, o_ref):
    tm = x_ref.shape[0]
    halves = [slice(0, tm // 2), slice(tm // 2, tm)]

    def project(r):
        mixed = jnp.concatenate([a_ref[r, :], b_ref[r, :], c_ref[r, :], d_ref[r, :]], axis=1)
        return jnp.dot(mixed, w_ref[...], preferred_element_type=jnp.float32)

    ys = [project(r) for r in halves]
    for r, y in zip(halves, ys):
        o_ref[r, :] = x_ref[r, :] + _rms(y, g_ref[...])


def _outproj(x2, oa, ob, oc, od, w, g, tm, layer):
    n = x2.shape[0]
    row = lambda wd: pl.BlockSpec((tm, wd), lambda i: (i, 0))
    return pl.pallas_call(
        _outproj_kernel,
        grid=(n // tm,),
        in_specs=[row(D_MODEL), row(256), row(256), row(256), row(256),
                  _layer_spec(w, layer),
                  pl.BlockSpec(g.shape, lambda i: (0, 0))],
        out_specs=row(D_MODEL),
        out_shape=jax.ShapeDtypeStruct((n, D_MODEL), jnp.float32),
        compiler_params=_cparams(("parallel",)),
        name="out_projection",
    )(x2, oa, ob, oc, od, w, g)


HALO = 8
DOWN_GROUP_ENDS = (5, 10, 11)


def _ffn_kernel(x_ref, xp_ref, xn_ref, gpre_ref, wup_ref, cw_ref, cb_ref, wdn_ref, gpost_ref,
                o_ref, h_ref, *, tm, fc, tiles_per_seq):
    i = pl.program_id(0)
    bf = jnp.bfloat16
    g = gpre_ref[...]
    first = (i % tiles_per_seq) == 0
    last = (i % tiles_per_seq) == tiles_per_seq - 1
    h_ref[0:HALO, :] = jnp.where(first, 0.0, _rms(xp_ref[...], g)).astype(bf)
    h_ref[HALO:HALO + tm, :] = _rms(x_ref[...], g).astype(bf)
    h_ref[HALO + tm:, :] = jnp.where(last, 0.0, _rms(xn_ref[...], g)).astype(bf)
    nf = D_FF // fc
    chunk = lambda c: slice(c * fc, (c + 1) * fc)

    def up(c):
        val_cols = slice(D_FF + c * fc, D_FF + (c + 1) * fc)
        gate = jnp.dot(h_ref[...], wup_ref[:, chunk(c)], preferred_element_type=jnp.float32)
        val = jnp.dot(h_ref[HALO:HALO + tm, :], wup_ref[:, val_cols], preferred_element_type=jnp.float32)
        return gate, val

    y = None
    acts, group_start = [], 0
    nxt = up(0)
    for c in range(nf):
        cs = chunk(c)
        gate, val = nxt
        if c + 1 < nf:
            nxt = up(c + 1)
        gate = (gate[HALO - 1:HALO - 1 + tm] * cw_ref[0:1, cs] + gate[HALO:HALO + tm] * cw_ref[1:2, cs]
                + gate[HALO + 1:HALO + 1 + tm] * cw_ref[2:3, cs] + cb_ref[:, cs])
        acts.append((0.5 * gate * (1.0 + lax.erf(gate * (2.0 ** -0.5))) * val).astype(bf))
        if c + 1 in DOWN_GROUP_ENDS:
            rows = slice(group_start * fc, (c + 1) * fc)
            part = jnp.dot(jnp.concatenate(acts, axis=1), wdn_ref[rows, :],
                           preferred_element_type=jnp.float32)
            y = part if y is None else y + part
            acts, group_start = [], c + 1
    o_ref[...] = x_ref[...] + _rms(y, gpost_ref[...])


def _ffn(x2, gpre, wup, cw, cb, wdn, gpost, seq, tm, fc, layer):
    n = x2.shape[0]
    tph = tm // HALO
    nh = n // HALO
    resident = lambda a: pl.BlockSpec(a.shape, lambda i: (0,) * a.ndim,
                                      pipeline_mode=pl.Buffered(1))
    return pl.pallas_call(
        functools.partial(_ffn_kernel, tm=tm, fc=fc, tiles_per_seq=seq // tm),
        grid=(n // tm,),
        in_specs=[pl.BlockSpec((tm, D_MODEL), lambda i: (i, 0)),
                  pl.BlockSpec((HALO, D_MODEL), lambda i: (jnp.maximum(i * tph - 1, 0), 0)),
                  pl.BlockSpec((HALO, D_MODEL), lambda i: (jnp.minimum((i + 1) * tph, nh - 1), 0)),
                  resident(gpre), _layer_spec(wup, layer, pipeline_mode=pl.Buffered(1)),
                  resident(cw), resident(cb),
                  _layer_spec(wdn, layer, pipeline_mode=pl.Buffered(1)), resident(gpost)],
        out_specs=pl.BlockSpec((tm, D_MODEL), lambda i: (i, 0)),
        out_shape=jax.ShapeDtypeStruct((n, D_MODEL), jnp.float32),
        scratch_shapes=[pltpu.VMEM((tm + 2 * HALO, D_MODEL), jnp.bfloat16)],
        compiler_params=_cparams(("parallel",)),
        name="conv_glu",
    )(x2, x2, x2, gpre, wup, cw, cb, wdn, gpost)


_GQA_HEAD_ORDER = (0, 2, 1, 3)


def _reorder_heads(w, order):
    lead = w.shape[:-1]
    nh = w.shape[-1] // HEAD_DIM
    w = w.reshape(lead + (nh, HEAD_DIM))
    return jnp.stack([w[..., o, :] for o in order], axis=-2).reshape(lead + (nh * HEAD_DIM,))


def _prep_w_in(w_in):
    w_in = w_in.astype(jnp.bfloat16)
    bq = _reorder_heads(w_in[..., 768:1024], _GQA_HEAD_ORDER)
    cq = _reorder_heads(w_in[..., 1280:1536], _GQA_HEAD_ORDER)
    ck = w_in[..., 1536:1664]
    kr = w_in[..., 2176:2208]
    z = lambda wd: jnp.zeros(w_in.shape[:-1] + (wd,), w_in.dtype)
    return jnp.concatenate(
        [w_in[..., 0:768], bq, w_in[..., 1024:1280], cq, ck, w_in[..., 1664:1792],
         w_in[..., 1792:2176], z(MLA_NOPE), kr, z(LANES - MLA_NOPE - MLA_ROPE)], axis=-1)


def _rope_tables(cos, sin, lead_ones, trail_ones):
    s = cos.shape[0]
    one = lambda wd: jnp.ones((s, wd), jnp.float32)
    zero = lambda wd: jnp.zeros((s, wd), jnp.float32)
    c = jnp.concatenate([one(lead_ones), cos, cos, one(trail_ones)], axis=1)
    sn = jnp.concatenate([zero(lead_ones), -sin, zero(cos.shape[1]), zero(trail_ones)], axis=1)
    sp = jnp.concatenate([zero(lead_ones), zero(cos.shape[1]), sin, zero(trail_ones)], axis=1)
    return jnp.stack([c, sn, sp])


def _angles(pos, dim):
    inv = ROPE_THETA ** (-jnp.arange(0, dim, 2, dtype=jnp.float32) / dim)
    ang = pos.astype(jnp.float32)[:, None] * inv[None, :]
    return jnp.cos(ang), jnp.sin(ang)


def _forward(x, rel_bias, attn_pre_norm, w_in, diff_lambda, diff_subln, win_sink, ax_q_norm, ax_k_norm,
             mla_q_norm, mla_kv_norm, mla_w_uq, mla_w_ukv, w_out, attn_post_norm, ffn_pre_norm,
             ffn_w_up, ffn_conv_w, ffn_conv_b, ffn_w_down, ffn_post_norm, *, tq_a, tq, cq, tw, tm, to, tf, fc):
    bsz, seq, _ = x.shape
    depth = w_in.shape[0]
    n = bsz * seq
    bf = jnp.bfloat16

    with jax.ensure_compile_time_eval():
        pos = jnp.arange(seq, dtype=jnp.int32)
        rows = seq // GRID_W
        row_ids = jnp.repeat(jnp.arange(rows, dtype=jnp.int32), GRID_W)
        col_ids = jnp.tile(jnp.arange(GRID_W, dtype=jnp.int32), rows)
        rcos, rsin = _angles(row_ids, HEAD_DIM // 2)
        ccos, csin = _angles(col_ids, HEAD_DIM // 2)
        ctab = jnp.concatenate([_rope_tables(rcos, rsin, 0, 0), _rope_tables(ccos, csin, 0, 0)], axis=2)
        ctab = jnp.tile(ctab, (1, 1, 4))
        mcos, msin = _angles(pos, MLA_ROPE)
        dtab = _rope_tables(mcos, msin, MLA_NOPE, LANES - MLA_NOPE - MLA_ROPE)

    rel_a = (jnp.arange(cq + 2 * NEAR, dtype=jnp.int32)[:, None] - NEAR
             - jnp.arange(cq, dtype=jnp.int32)[None, :])
    gt = _bias_tiles(rel_bias, _bucket(rel_a), 0, 4, LOG2E)
    rel_b = jnp.arange(3 * BLOCK, dtype=jnp.int32)[None, :] - BLOCK - jnp.arange(BLOCK, dtype=jnp.int32)[:, None]
    idx_b = jnp.where(jnp.abs(rel_b) <= WINDOW, _bucket(rel_b), -1)
    wbias = _bias_tiles(rel_bias, idx_b, 4, 4, 1.0)

    w_in_p = _prep_w_in(w_in)
    cq_g = jnp.tile(ax_q_norm, (1, 4))[:, None, :]
    ck_g = jnp.tile(ax_k_norm, (1, 2))[:, None, :]
    wuq = jnp.pad(mla_w_uq.reshape(depth, MLA_Q_RANK, 4, MLA_NOPE + MLA_ROPE),
                  ((0, 0), (0, 0), (0, 0), (0, LANES - MLA_NOPE - MLA_ROPE))
                  ).reshape(depth, MLA_Q_RANK, 4 * LANES).astype(bf)
    wukv = mla_w_ukv.reshape(depth, MLA_KV_RANK, 4, 2, HEAD_DIM)
    wukvk = jnp.pad(wukv[:, :, :, 0], ((0, 0), (0, 0), (0, 0), (0, HEAD_DIM))
                    ).reshape(depth, MLA_KV_RANK, 4 * LANES).astype(bf)
    wukvv = wukv[:, :, :, 1].reshape(depth, MLA_KV_RANK, GROUP_WIDTH).astype(bf)
    wo = w_out.astype(bf).reshape(depth, 4, 4, HEAD_DIM, D_MODEL)
    slots = [(0, h) for h in range(4)] + [(1, h) for h in _GQA_HEAD_ORDER] \
        + [(2, h) for h in _GQA_HEAD_ORDER] + [(3, h) for h in range(4)]
    wo = jnp.stack([wo[:, m, h] for m, h in slots], axis=1).reshape(depth, D_MODEL, D_MODEL)
    wup = ffn_w_up.astype(bf)
    wdn = ffn_w_down.astype(bf)
    sub = jnp.tile(diff_subln, (1, 2))[:, None, :]

    x2 = x.reshape(n, D_MODEL)
    r3 = lambda a: a.reshape(bsz, seq, a.shape[-1])
    for l in range(depth):
        lam_init = 0.8 - 0.6 * math.exp(-0.3 * l)
        (aq, ak, avt, bq, bk, bv, cq_, ck, cvt, dq, dk, dvt) = _pre(
            x2, attn_pre_norm[l][None], w_in_p, cq_g[l], ck_g[l], ctab,
            mla_q_norm[l][None], mla_kv_norm[l][None], wuq[l], wukvk[l], wukvv[l], dtab, seq, tm, l)
        oa = _diff_attention(r3(aq), r3(ak), avt, gt, diff_lambda[l][None], sub[l], lam_init,
                             tq_a, cq)
        ob = _win_attention(r3(bq), r3(bk), r3(bv), win_sink[l], wbias, tw)
        oc = _gqa_attention(r3(cq_), r3(ck), cvt, tq, cq)
        od = _mla_attention(r3(dq), r3(dk), dvt, tq, cq)
        x2 = _outproj(x2, oa.reshape(n, -1), ob.reshape(n, -1), oc.reshape(n, -1),
                      od.reshape(n, -1), wo, attn_post_norm[l][None], to, l)
        x2 = _ffn(x2, ffn_pre_norm[l][None], wup, ffn_conv_w[l], ffn_conv_b[l][None],
                  wdn, ffn_post_norm[l][None], seq, tf, fc, l)
    return x2.reshape(bsz, seq, D_MODEL)


def kernel(x, rel_bias, attn_pre_norm, w_in, diff_lambda, diff_subln, win_sink, ax_q_norm, ax_k_norm,
           mla_q_norm, mla_kv_norm, mla_w_uq, mla_w_ukv, w_out, attn_post_norm, ffn_pre_norm,
           ffn_w_up, ffn_conv_w, ffn_conv_b, ffn_w_down, ffn_post_norm):
    return _forward(x, rel_bias, attn_pre_norm, w_in, diff_lambda, diff_subln, win_sink, ax_q_norm,
                    ax_k_norm, mla_q_norm, mla_kv_norm, mla_w_uq, mla_w_ukv, w_out, attn_post_norm,
                    ffn_pre_norm, ffn_w_up, ffn_conv_w, ffn_conv_b, ffn_w_down, ffn_post_norm,
                    tq_a=512, tq=1024, cq=256, tw=1024, tm=512, to=1024, tf=512, fc=256)
```

```python
import functools
import math

import jax
import jax.numpy as jnp
from jax import lax
from jax.experimental import pallas as pl
from jax.experimental.pallas import tpu as pltpu

D_MODEL = 1024
HEAD_DIM = 64
GROUP_WIDTH = 256
BLOCK = 128
DIFF_HALF = 32
WINDOW = 128
GRID_W = 64
ROPE_THETA = 10000.0
MLA_Q_RANK = 256
MLA_KV_RANK = 128
MLA_NOPE = 64
MLA_ROPE = 32
REL_BUCKETS = 32
REL_MAX_DIST = 128
D_FF = 2816
EPS = 1e-6
LANES = 128
LOG2E = 1.4426950408889634
NEG_BIG = -1e30
VMEM_LIMIT = 56 * 1024 * 1024
NEAR = 2 * BLOCK


def _layer_spec(a, layer, **kw):
    return pl.BlockSpec((None,) + a.shape[1:], lambda *_: (layer,) + (0,) * (a.ndim - 1), **kw)


def _cparams(sem):
    return pltpu.CompilerParams(dimension_semantics=sem, vmem_limit_bytes=VMEM_LIMIT)


def _rms(x, g):
    return x * lax.rsqrt(jnp.mean(x * x, axis=-1, keepdims=True) + EPS) * g


def _group_rms(x, ngroups):
    grp = lax.broadcasted_iota(jnp.int32, x.shape, 1) // HEAD_DIM
    x2 = x * x
    inv = jnp.zeros_like(x)
    for h in range(ngroups):
        m = grp == h
        ss = jnp.sum(jnp.where(m, x2, 0.0), axis=-1, keepdims=True)
        inv = jnp.where(m, lax.rsqrt(ss * (1.0 / HEAD_DIM) + EPS), inv)
    return x * inv


def _rope(x, c, sn, sp, half):
    w = x.shape[-1]
    return x * c + pltpu.roll(x, w - half, 1) * sn + pltpu.roll(x, half, 1) * sp


def _bias_kernel(tab_ref, idx_ref, o_ref, *, head0, scale):
    h = pl.program_id(0)
    idx = idx_ref[...]
    acc = jnp.zeros(idx.shape, jnp.float32)
    for b in range(REL_BUCKETS):
        acc = jnp.where(idx == b, tab_ref[b, head0 + h], acc)
    o_ref[0] = jnp.where(idx < 0, NEG_BIG, acc * scale)


def _bias_tiles(rel_bias, idx, head0, nheads, scale):
    r, c = idx.shape
    return pl.pallas_call(
        functools.partial(_bias_kernel, head0=head0, scale=scale),
        grid=(nheads,),
        in_specs=[pl.BlockSpec(memory_space=pltpu.SMEM),
                  pl.BlockSpec((r, c), lambda h: (0, 0))],
        out_specs=pl.BlockSpec((1, r, c), lambda h: (h, 0, 0)),
        out_shape=jax.ShapeDtypeStruct((nheads, r, c), jnp.float32),
        compiler_params=_cparams(("arbitrary",)),
        name="rel_bias_tiles",
    )(rel_bias, idx)


def _bucket(rel):
    half = REL_BUCKETS // 2
    max_exact = half // 2
    n = jnp.abs(rel)
    nf = jnp.maximum(n, 1).astype(jnp.float32)
    large = max_exact + (jnp.log(nf / max_exact) / math.log(REL_MAX_DIST / max_exact)
                         * (half - max_exact)).astype(jnp.int32)
    large = jnp.minimum(large, half - 1)
    return jnp.where(rel > 0, half, 0) + jnp.where(n < max_exact, n, large)


def _pre_kernel(x_ref, g_ref, w_ref, cq_g_ref, ck_g_ref, ctab_ref, dq_g_ref, dkv_g_ref,
                wuq_ref, wukvk_ref, wukvv_ref, dtab_ref,
                aq_ref, ak_ref, avt_ref, bq_ref, bk_ref, bv_ref, cq_ref, ck_ref, cvt_ref,
                dq_ref, dk_ref, dvt_ref):
    bf = jnp.bfloat16
    tm = x_ref.shape[0]
    rep = lambda t: jnp.concatenate([t] * 4, axis=1)

    def epi_c(r, p):
        cc, csn, csp = ctab_ref[0, r], ctab_ref[1, r], ctab_ref[2, r]
        q = _group_rms(p[:, 0:256], 4) * cq_g_ref[...]
        cq_ref[r, :] = (_rope(q, cc, csn, csp, 16) * (HEAD_DIM ** -0.5 * LOG2E)).astype(bf)
        k = _group_rms(p[:, 256:384], 2) * ck_g_ref[...]
        ck_ref[r, :] = _rope(k, cc[:, :128], csn[:, :128], csp[:, :128], 16).astype(bf)
        cvt_ref[:, r] = p[:, 384:512].T.astype(bf)

    def epi_d(r, p):
        dc, dsn, dsp = dtab_ref[0, r], dtab_ref[1, r], dtab_ref[2, r]
        cq = _rms(p[:, 0:256], dq_g_ref[...]).astype(bf)
        q = jnp.dot(cq, wuq_ref[...], preferred_element_type=jnp.float32)
        q = _rope(q, rep(dc), rep(dsn), rep(dsp), 16)
        dq_ref[r, :] = (q * ((MLA_NOPE + MLA_ROPE) ** -0.5 * LOG2E)).astype(bf)
        ckv = _rms(p[:, 256:384], dkv_g_ref[...]).astype(bf)
        kr = _rope(p[:, 384:512], dc, dsn, dsp, 16)
        kn = jnp.dot(ckv, wukvk_ref[...], preferred_element_type=jnp.float32)
        dk_ref[r, :] = (kn + rep(kr)).astype(bf)
        dvt_ref[:, r] = jnp.dot(ckv, wukvv_ref[...], preferred_element_type=jnp.float32).T.astype(bf)

    def epi_a(r, p):
        aq_ref[r, :] = (p[:, 0:256] * (DIFF_HALF ** -0.5 * LOG2E)).astype(bf)
        ak_ref[r, :] = p[:, 256:512].astype(bf)
        avt_ref[:, r] = p[:, 512:768].T.astype(bf)

    def epi_b(r, p):
        bq_ref[r, :] = (p[:, 0:256] * 0.125).astype(bf)
        bk_ref[r, :] = p[:, 256:384].astype(bf)
        bv_ref[r, :] = p[:, 384:512].astype(bf)

    mixers = [((1280, 1792), epi_c), ((1792, 2304), epi_d), ((0, 768), epi_a), ((768, 1280), epi_b)]
    halves = [slice(0, tm // 2), slice(tm // 2, tm)]
    pending = None
    for r in halves:
        h = _rms(x_ref[r, :], g_ref[...]).astype(bf)
        for (lo, hi), epilogue in mixers:
            p = jnp.dot(h, w_ref[:, lo:hi], preferred_element_type=jnp.float32)
            if pending is not None:
                pending[0](pending[1], pending[2])
            pending = (epilogue, r, p)
    pending[0](pending[1], pending[2])


def _pre(x2, g, w, cq_g, ck_g, ctab, dq_g, dkv_g, wuq, wukvk, wukvv, dtab, seq, tm, layer):
    n = x2.shape[0]
    nt = seq // tm
    row = lambda wd: pl.BlockSpec((tm, wd), lambda i: (i, 0))
    col = lambda wd: pl.BlockSpec((wd, tm), lambda i: (0, i))
    full = lambda a: pl.BlockSpec(a.shape, lambda i: (0,) * a.ndim)
    bf = jnp.bfloat16
    rows = lambda wd: (row(wd), jax.ShapeDtypeStruct((n, wd), bf))
    cols = lambda wd: (col(wd), jax.ShapeDtypeStruct((wd, n), bf))
    outs = [rows(256), rows(256), cols(256), rows(256), rows(128), rows(128),
            rows(256), rows(128), cols(128), rows(512), rows(512), cols(256)]
    return pl.pallas_call(
        _pre_kernel,
        grid=(n // tm,),
        in_specs=[row(D_MODEL), full(g), _layer_spec(w, layer), full(cq_g), full(ck_g),
                  pl.BlockSpec((3, tm, 256), lambda i: (0, i % nt, 0)),
                  full(dq_g), full(dkv_g), full(wuq), full(wukvk), full(wukvv),
                  pl.BlockSpec((3, tm, 128), lambda i: (0, i % nt, 0))],
        out_specs=[o[0] for o in outs],
        out_shape=[o[1] for o in outs],
        compiler_params=_cparams(("parallel",)),
        name="pre_attention",
    )(x2, g, w, cq_g, ck_g, ctab, dq_g, dkv_g, wuq, wukvk, wukvv, dtab)


_NT = (((1,), (1,)), ((), ()))


FLASH_KEYS = 512
FLASH_LAG = 4


def _attend_flash(chains):
    n = len(chains)
    nblk = chains[0][1].shape[0] // FLASH_KEYS
    fold = lambda a: a.reshape(a.shape[0] // 8, 8, a.shape[1])
    state = [None] * n

    def scores(b, c):
        q, k, _, bias = chains[c]
        rows = slice(b * FLASH_KEYS, (b + 1) * FLASH_KEYS)
        s = lax.dot_general(k[rows], q, _NT, preferred_element_type=jnp.float32)
        if bias is not None:
            per = FLASH_KEYS // BLOCK
            s = s + jnp.concatenate([bias(b * per + j) for j in range(per)], axis=0)
        return s

    def update(b, c, s):
        vt = chains[c][2][:, b * FLASH_KEYS:(b + 1) * FLASH_KEYS]
        mb = jnp.max(jnp.max(fold(s), axis=0), axis=0, keepdims=True)
        if state[c] is None:
            m = mb
            p = jnp.exp2(s - m)
            l = jnp.sum(jnp.sum(fold(p), axis=0), axis=0, keepdims=True)
            acc = jnp.dot(vt, p.astype(jnp.bfloat16), preferred_element_type=jnp.float32)
        else:
            m0, l0, acc0 = state[c]
            m = jnp.maximum(m0, mb)
            alpha = jnp.exp2(m0 - m)
            p = jnp.exp2(s - m)
            l = alpha * l0 + jnp.sum(jnp.sum(fold(p), axis=0), axis=0, keepdims=True)
            acc = alpha * acc0 + jnp.dot(vt, p.astype(jnp.bfloat16), preferred_element_type=jnp.float32)
        state[c] = (m, l, acc)

    units = [(b, c) for b in range(nblk) for c in range(n)]
    pending = []
    for u in units:
        pending.append((*u, scores(*u)))
        if len(pending) > FLASH_LAG:
            update(*pending.pop(0))
    for item in pending:
        update(*item)
    return [acc * (1.0 / l) for _, l, acc in state]


def _diff_kernel(q_ref, k_ref, vt_ref, gt_ref, dl_ref, sub_ref, o_ref, *, lam_init, seq, tq, cq):
    i = pl.program_id(1)
    lane = lax.broadcasted_iota(jnp.int32, (cq, LANES), 1)
    upper_rows = lax.broadcasted_iota(jnp.int32, (LANES, cq), 0) >= HEAD_DIM
    dl = dl_ref[0]
    lam = (jnp.exp(jnp.sum(dl[0:1] * dl[1:2], axis=-1, keepdims=True))
           - jnp.exp(jnp.sum(dl[2:3] * dl[3:4], axis=-1, keepdims=True)) + lam_init)
    chains = []
    for r0 in range(0, tq, cq):
        offs = [pl.multiple_of(jnp.clip(jb * BLOCK - (i * tq + r0), -NEAR, cq + BLOCK) + NEAR, BLOCK)
                for jb in range(seq // BLOCK)]
        for pr in range(2):
            pl_ = slice(pr * LANES, (pr + 1) * LANES)
            q, k, vt = q_ref[0, r0:r0 + cq, pl_], k_ref.at[0, :, pl_], vt_ref.at[pl_, :]
            for hh in range(2):
                bias_t = lambda jb, h=2 * pr + hh, offs=offs: gt_ref[h, pl.ds(offs[jb], BLOCK), :]
                for c in range(2):
                    lo = hh * HEAD_DIM + c * DIFF_HALF
                    qm = jnp.where((lane >= lo) & (lane < lo + DIFF_HALF), q, jnp.zeros_like(q))
                    chains.append((qm, k, vt, bias_t))
    outs = _attend_flash(chains)
    for n, r0 in enumerate(range(0, tq, cq)):
        for pr in range(2):
            heads = []
            for hh in range(2):
                first = 8 * n + 4 * pr + 2 * hh
                oh = outs[first] - lam * outs[first + 1]
                mh = upper_rows if hh else jnp.logical_not(upper_rows)
                ss = jnp.sum(jnp.where(mh, oh * oh, 0.0), axis=0, keepdims=True) * (1.0 / HEAD_DIM)
                heads.append(oh * lax.rsqrt(ss + EPS))
            out = jnp.where(upper_rows, heads[1], heads[0]).T * sub_ref[...] * (1.0 - lam_init)
            o_ref[0, r0:r0 + cq, pr * LANES:(pr + 1) * LANES] = out.astype(jnp.bfloat16)


def _diff_attention(q, k, vt, gt, dl, sub, lam_init, tq, cq):
    b, seq, _ = q.shape
    return pl.pallas_call(
        functools.partial(_diff_kernel, lam_init=lam_init, seq=seq, tq=tq, cq=cq),
        grid=(b, seq // tq),
        in_specs=[pl.BlockSpec((1, tq, GROUP_WIDTH), lambda bi, i: (bi, i, 0)),
                  pl.BlockSpec((1, seq, GROUP_WIDTH), lambda bi, i: (bi, 0, 0)),
                  pl.BlockSpec((GROUP_WIDTH, seq), lambda bi, i: (0, bi)),
                  pl.BlockSpec(gt.shape, lambda bi, i: (0, 0, 0)),
                  pl.BlockSpec((1, 4, DIFF_HALF), lambda bi, i: (0, 0, 0)),
                  pl.BlockSpec((1, LANES), lambda bi, i: (0, 0))],
        out_specs=pl.BlockSpec((1, tq, GROUP_WIDTH), lambda bi, i: (bi, i, 0)),
        out_shape=jax.ShapeDtypeStruct((b, seq, GROUP_WIDTH), jnp.bfloat16),
        compiler_params=_cparams(("parallel", "arbitrary")),
        name="diff_attention",
    )(q, k, vt, gt, dl, sub)


def _gqa_kernel(q_ref, k_ref, vt_ref, o_ref, *, tq, cq):
    k, vt = k_ref.at[0], vt_ref
    lower = lax.broadcasted_iota(jnp.int32, (cq, LANES), 1) < HEAD_DIM
    upper_rows = lax.broadcasted_iota(jnp.int32, (LANES, cq), 0) >= HEAD_DIM
    chains = []
    for r0 in range(0, tq, cq):
        for j in range(2):
            q = q_ref[0, r0:r0 + cq, j * LANES:(j + 1) * LANES]
            zero = jnp.zeros_like(q)
            chains.append((jnp.where(lower, q, zero), k, vt, None))
            chains.append((jnp.where(lower, zero, q), k, vt, None))
    outs = _attend_flash(chains)
    for n, r0 in enumerate(range(0, tq, cq)):
        for j in range(2):
            lo, hi = outs[4 * n + 2 * j], outs[4 * n + 2 * j + 1]
            o_ref[0, r0:r0 + cq, j * LANES:(j + 1) * LANES] = jnp.where(
                upper_rows, hi, lo).T.astype(jnp.bfloat16)


def _gqa_attention(q, k, vt, tq, cq):
    b, seq, _ = q.shape
    return pl.pallas_call(
        functools.partial(_gqa_kernel, tq=tq, cq=cq),
        grid=(b, seq // tq),
        in_specs=[pl.BlockSpec((1, tq, GROUP_WIDTH), lambda bi, i: (bi, i, 0)),
                  pl.BlockSpec((1, seq, LANES), lambda bi, i: (bi, 0, 0)),
                  pl.BlockSpec((LANES, seq), lambda bi, i: (0, bi))],
        out_specs=pl.BlockSpec((1, tq, GROUP_WIDTH), lambda bi, i: (bi, i, 0)),
        out_shape=jax.ShapeDtypeStruct((b, seq, GROUP_WIDTH), jnp.bfloat16),
        compiler_params=_cparams(("parallel", "arbitrary")),
        name="axial_gqa_attention",
    )(q, k, vt)


def _mla_kernel(q_ref, k_ref, vt_ref, o_ref, *, tq, cq):
    upper_rows = lax.broadcasted_iota(jnp.int32, (LANES, cq), 0) >= HEAD_DIM
    chains = []
    for r0 in range(0, tq, cq):
        for hh in range(4):
            hl = slice(hh * LANES, (hh + 1) * LANES)
            vl = slice((hh // 2) * LANES, (hh // 2 + 1) * LANES)
            chains.append((q_ref[0, r0:r0 + cq, hl], k_ref.at[0, :, hl], vt_ref.at[vl, :], None))
    outs = _attend_flash(chains)
    for n, r0 in enumerate(range(0, tq, cq)):
        for pr in range(2):
            lo, hi = outs[4 * n + 2 * pr], outs[4 * n + 2 * pr + 1]
            o_ref[0, r0:r0 + cq, pr * LANES:(pr + 1) * LANES] = jnp.where(
                upper_rows, hi, lo).T.astype(jnp.bfloat16)


def _mla_attention(q, k, vt, tq, cq):
    b, seq, _ = q.shape
    return pl.pallas_call(
        functools.partial(_mla_kernel, tq=tq, cq=cq),
        grid=(b, seq // tq),
        in_specs=[pl.BlockSpec((1, tq, 4 * LANES), lambda bi, i: (bi, i, 0)),
                  pl.BlockSpec((1, seq, 4 * LANES), lambda bi, i: (bi, 0, 0)),
                  pl.BlockSpec((GROUP_WIDTH, seq), lambda bi, i: (0, bi))],
        out_specs=pl.BlockSpec((1, tq, GROUP_WIDTH), lambda bi, i: (bi, i, 0)),
        out_shape=jax.ShapeDtypeStruct((b, seq, GROUP_WIDTH), jnp.bfloat16),
        compiler_params=_cparams(("parallel", "arbitrary")),
        name="latent_attention",
    )(q, k, vt)


def _softmax_parts(s, extra):
    m = jnp.maximum(jnp.max(s, axis=-1, keepdims=True), extra)
    p = jnp.exp(s - m)
    l = jnp.sum(p, axis=-1, keepdims=True) + jnp.exp(extra - m)
    return p.astype(jnp.bfloat16), 1.0 / l


def _win_kernel(sink_ref, q_ref, kp_ref, kc_ref, kn_ref, vp_ref, vc_ref, vn_ref, wb_ref, o_ref,
                *, nstep, nb):
    n = pl.program_id(1)
    kext = jnp.concatenate([kp_ref[0], kc_ref[0], kn_ref[0]], axis=0)
    vext = jnp.concatenate([vp_ref[0], vc_ref[0], vn_ref[0]], axis=0)
    lane = lax.broadcasted_iota(jnp.int32, (BLOCK, LANES), 1)
    col = lax.broadcasted_iota(jnp.int32, (BLOCK, 3 * BLOCK), 1)
    units = [(blk, j) for blk in range(nb) for j in range(2)]

    def scores(blk, j):
        q = q_ref[0, blk * BLOCK:(blk + 1) * BLOCK, j * LANES:(j + 1) * LANES]
        zero = jnp.zeros_like(q)
        qs = jnp.concatenate([jnp.where(lane < HEAD_DIM, q, zero),
                              jnp.where(lane >= HEAD_DIM, q, zero)], axis=0)
        return lax.dot_general(qs, kext[blk * BLOCK:(blk + 3) * BLOCK], _NT,
                               preferred_element_type=jnp.float32)

    def softmax(blk, j, s):
        ps, rs = [], []
        for g in range(2):
            head = j + 2 * g
            sg = s[g * BLOCK:(g + 1) * BLOCK] + wb_ref[head]
            if blk == 0:
                sg = jnp.where((col < BLOCK) & (n == 0), NEG_BIG, sg)
            if blk == nb - 1:
                sg = jnp.where((col >= 2 * BLOCK) & (n == nstep - 1), NEG_BIG, sg)
            p, r = _softmax_parts(sg, sink_ref[head])
            ps.append(p)
            rs.append(r)
        return jnp.concatenate(ps, axis=0), rs

    def pv(blk, j, p, rs):
        o = jnp.dot(p, vext[blk * BLOCK:(blk + 3) * BLOCK], preferred_element_type=jnp.float32)
        out = jnp.where(lane < HEAD_DIM, o[:BLOCK] * rs[0], o[BLOCK:] * rs[1])
        o_ref[0, blk * BLOCK:(blk + 1) * BLOCK, j * LANES:(j + 1) * LANES] = out.astype(jnp.bfloat16)

    s, pr = {}, {}
    for t in range(len(units) + 2):
        if t < len(units):
            s[t] = scores(*units[t])
        if 0 <= t - 2 < len(units):
            pv(*units[t - 2], *pr.pop(t - 2))
        if 0 <= t - 1 < len(units):
            pr[t - 1] = softmax(*units[t - 1], s.pop(t - 1))


def _win_attention(q, k, v, sink, wbias, tw):
    b, seq, _ = q.shape
    nblk = seq // BLOCK
    nb = tw // BLOCK
    prev = lambda bi, n: (bi, jnp.maximum(n * nb - 1, 0), 0)
    cur = lambda bi, n: (bi, n, 0)
    nxt = lambda bi, n: (bi, jnp.minimum((n + 1) * nb, nblk - 1), 0)
    halo = lambda f: pl.BlockSpec((1, BLOCK, LANES), f)
    body = pl.BlockSpec((1, tw, LANES), cur)
    return pl.pallas_call(
        functools.partial(_win_kernel, nstep=seq // tw, nb=nb),
        grid=(b, seq // tw),
        in_specs=[pl.BlockSpec(memory_space=pltpu.SMEM),
                  pl.BlockSpec((1, tw, GROUP_WIDTH), cur),
                  halo(prev), body, halo(nxt), halo(prev), body, halo(nxt),
                  pl.BlockSpec((4, BLOCK, 3 * BLOCK), lambda bi, n: (0, 0, 0))],
        out_specs=pl.BlockSpec((1, tw, GROUP_WIDTH), cur),
        out_shape=jax.ShapeDtypeStruct((b, seq, GROUP_WIDTH), jnp.bfloat16),
        compiler_params=_cparams(("parallel", "arbitrary")),
        name="window_attention",
    )(sink, q, k, k, k, v, v, v, wbias)


def _outproj_kernel(x_ref, a_ref, b_ref, c_ref, d_ref, w_ref, g_ref, o_ref):
    tm = x_ref.shape[0]
    halves = [slice(0, tm // 2), slice(tm // 2, tm)]

    def project(r):
        mixed = jnp.concatenate([a_ref[r, :], b_ref[r, :], c_ref[r, :], d_ref[r, :]], axis=1)
        return jnp.dot(mixed, w_ref[...], preferred_element_type=jnp.float32)

    ys = [project(r) for r in halves]
    for r, y in zip(halves, ys):
        o_ref[r, :] = x_ref[r, :] + _rms(y, g_ref[...])


def _outproj(x2, oa, ob, oc, od, w, g, tm, layer):
    n = x2.shape[0]
    row = lambda wd: pl.BlockSpec((tm, wd), lambda i: (i, 0))
    return pl.pallas_call(
        _outproj_kernel,
        grid=(n // tm,),
        in_specs=[row(D_MODEL), row(256), row(256), row(256), row(256),
                  _layer_spec(w, layer),
                  pl.BlockSpec(g.shape, lambda i: (0, 0))],
        out_specs=row(D_MODEL),
        out_shape=jax.ShapeDtypeStruct((n, D_MODEL), jnp.float32),
        compiler_params=_cparams(("parallel",)),
        name="out_projection",
    )(x2, oa, ob, oc, od, w, g)


HALO = 8
DOWN_GROUP_ENDS = (5, 10, 11)


def _ffn_kernel(x_ref, xp_ref, xn_ref, gpre_ref, wup_ref, cw_ref, cb_ref, wdn_ref, gpost_ref,
                o_ref, h_ref, *, tm, fc, tiles_per_seq):
    i = pl.program_id(0)
    bf = jnp.bfloat16
    g = gpre_ref[...]
    first = (i % tiles_per_seq) == 0
    last = (i % tiles_per_seq) == tiles_per_seq - 1
    h_ref[0:HALO, :] = jnp.where(first, 0.0, _rms(xp_ref[...], g)).astype(bf)
    h_ref[HALO:HALO + tm, :] = _rms(x_ref[...], g).astype(bf)
    h_ref[HALO + tm:, :] = jnp.where(last, 0.0, _rms(xn_ref[...], g)).astype(bf)
    nf = D_FF // fc
    chunk = lambda c: slice(c * fc, (c + 1) * fc)

    def up(c):
        val_cols = slice(D_FF + c * fc, D_FF + (c + 1) * fc)
        gate = jnp.dot(h_ref[...], wup_ref[:, chunk(c)], preferred_element_type=jnp.float32)
        val = jnp.dot(h_ref[HALO:HALO + tm, :], wup_ref[:, val_cols], preferred_element_type=jnp.float32)
        return gate, val

    y = None
    acts, group_start = [], 0
    nxt = up(0)
    for c in range(nf):
        cs = chunk(c)
        gate, val = nxt
        if c + 1 < nf:
            nxt = up(c + 1)
        gate = (gate[HALO - 1:HALO - 1 + tm] * cw_ref[0:1, cs] + gate[HALO:HALO + tm] * cw_ref[1:2, cs]
                + gate[HALO + 1:HALO + 1 + tm] * cw_ref[2:3, cs] + cb_ref[:, cs])
        acts.append((0.5 * gate * (1.0 + lax.erf(gate * (2.0 ** -0.5))) * val).astype(bf))
        if c + 1 in DOWN_GROUP_ENDS:
            rows = slice(group_start * fc, (c + 1) * fc)
            part = jnp.dot(jnp.concatenate(acts, axis=1), wdn_ref[rows, :],
                           preferred_element_type=jnp.float32)
            y = part if y is None else y + part
            acts, group_start = [], c + 1
    o_ref[...] = x_ref[...] + _rms(y, gpost_ref[...])


def _ffn(x2, gpre, wup, cw, cb, wdn, gpost, seq, tm, fc, layer):
    n = x2.shape[0]
    tph = tm // HALO
    nh = n // HALO
    resident = lambda a: pl.BlockSpec(a.shape, lambda i: (0,) * a.ndim,
                                      pipeline_mode=pl.Buffered(1))
    return pl.pallas_call(
        functools.partial(_ffn_kernel, tm=tm, fc=fc, tiles_per_seq=seq // tm),
        grid=(n // tm,),
        in_specs=[pl.BlockSpec((tm, D_MODEL), lambda i: (i, 0)),
                  pl.BlockSpec((HALO, D_MODEL), lambda i: (jnp.maximum(i * tph - 1, 0), 0)),
                  pl.BlockSpec((HALO, D_MODEL), lambda i: (jnp.minimum((i + 1) * tph, nh - 1), 0)),
                  resident(gpre), _layer_spec(wup, layer, pipeline_mode=pl.Buffered(1)),
                  resident(cw), resident(cb),
                  _layer_spec(wdn, layer, pipeline_mode=pl.Buffered(1)), resident(gpost)],
        out_specs=pl.BlockSpec((tm, D_MODEL), lambda i: (i, 0)),
        out_shape=jax.ShapeDtypeStruct((n, D_MODEL), jnp.float32),
        scratch_shapes=[pltpu.VMEM((tm + 2 * HALO, D_MODEL), jnp.bfloat16)],
        compiler_params=_cparams(("parallel",)),
        name="conv_glu",
    )(x2, x2, x2, gpre, wup, cw, cb, wdn, gpost)


_GQA_HEAD_ORDER = (0, 2, 1, 3)


def _reorder_heads(w, order):
    lead = w.shape[:-1]
    nh = w.shape[-1] // HEAD_DIM
    w = w.reshape(lead + (nh, HEAD_DIM))
    return jnp.stack([w[..., o, :] for o in order], axis=-2).reshape(lead + (nh * HEAD_DIM,))


def _prep_w_in(w_in):
    w_in = w_in.astype(jnp.bfloat16)
    bq = _reorder_heads(w_in[..., 768:1024], _GQA_HEAD_ORDER)
    cq = _reorder_heads(w_in[..., 1280:1536], _GQA_HEAD_ORDER)
    ck = w_in[..., 1536:1664]
    kr = w_in[..., 2176:2208]
    z = lambda wd: jnp.zeros(w_in.shape[:-1] + (wd,), w_in.dtype)
    return jnp.concatenate(
        [w_in[..., 0:768], bq, w_in[..., 1024:1280], cq, ck, w_in[..., 1664:1792],
         w_in[..., 1792:2176], z(MLA_NOPE), kr, z(LANES - MLA_NOPE - MLA_ROPE)], axis=-1)


def _rope_tables(cos, sin, lead_ones, trail_ones):
    s = cos.shape[0]
    one = lambda wd: jnp.ones((s, wd), jnp.float32)
    zero = lambda wd: jnp.zeros((s, wd), jnp.float32)
    c = jnp.concatenate([one(lead_ones), cos, cos, one(trail_ones)], axis=1)
    sn = jnp.concatenate([zero(lead_ones), -sin, zero(cos.shape[1]), zero(trail_ones)], axis=1)
    sp = jnp.concatenate([zero(lead_ones), zero(cos.shape[1]), sin, zero(trail_ones)], axis=1)
    return jnp.stack([c, sn, sp])


def _angles(pos, dim):
    inv = ROPE_THETA ** (-jnp.arange(0, dim, 2, dtype=jnp.float32) / dim)
    ang = pos.astype(jnp.float32)[:, None] * inv[None, :]
    return jnp.cos(ang), jnp.sin(ang)


def _forward(x, rel_bias, attn_pre_norm, w_in, diff_lambda, diff_subln, win_sink, ax_q_norm, ax_k_norm,
             mla_q_norm, mla_kv_norm, mla_w_uq, mla_w_ukv, w_out, attn_post_norm, ffn_pre_norm,
             ffn_w_up, ffn_conv_w, ffn_conv_b, ffn_w_down, ffn_post_norm, *, tq_a, tq, cq, tw, tm, to, tf, fc):
    bsz, seq, _ = x.shape
    depth = w_in.shape[0]
    n = bsz * seq
    bf = jnp.bfloat16

    with jax.ensure_compile_time_eval():
        pos = jnp.arange(seq, dtype=jnp.int32)
        rows = seq // GRID_W
        row_ids = jnp.repeat(jnp.arange(rows, dtype=jnp.int32), GRID_W)
        col_ids = jnp.tile(jnp.arange(GRID_W, dtype=jnp.int32), rows)
        rcos, rsin = _angles(row_ids, HEAD_DIM // 2)
        ccos, csin = _angles(col_ids, HEAD_DIM // 2)
        ctab = jnp.concatenate([_rope_tables(rcos, rsin, 0, 0), _rope_tables(ccos, csin, 0, 0)], axis=2)
        ctab = jnp.tile(ctab, (1, 1, 4))
        mcos, msin = _angles(pos, MLA_ROPE)
        dtab = _rope_tables(mcos, msin, MLA_NOPE, LANES - MLA_NOPE - MLA_ROPE)

    rel_a = (jnp.arange(cq + 2 * NEAR, dtype=jnp.int32)[:, None] - NEAR
             - jnp.arange(cq, dtype=jnp.int32)[None, :])
    gt = _bias_tiles(rel_bias, _bucket(rel_a), 0, 4, LOG2E)
    rel_b = jnp.arange(3 * BLOCK, dtype=jnp.int32)[None, :] - BLOCK - jnp.arange(BLOCK, dtype=jnp.int32)[:, None]
    idx_b = jnp.where(jnp.abs(rel_b) <= WINDOW, _bucket(rel_b), -1)
    wbias = _bias_tiles(rel_bias, idx_b, 4, 4, 1.0)

    w_in_p = _prep_w_in(w_in)
    cq_g = jnp.tile(ax_q_norm, (1, 4))[:, None, :]
    ck_g = jnp.tile(ax_k_norm, (1, 2))[:, None, :]
    wuq = jnp.pad(mla_w_uq.reshape(depth, MLA_Q_RANK, 4, MLA_NOPE + MLA_ROPE),
                  ((0, 0), (0, 0), (0, 0), (0, LANES - MLA_NOPE - MLA_ROPE))
                  ).reshape(depth, MLA_Q_RANK, 4 * LANES).astype(bf)
    wukv = mla_w_ukv.reshape(depth, MLA_KV_RANK, 4, 2, HEAD_DIM)
    wukvk = jnp.pad(wukv[:, :, :, 0], ((0, 0), (0, 0), (0, 0), (0, HEAD_DIM))
                    ).reshape(depth, MLA_KV_RANK, 4 * LANES).astype(bf)
    wukvv = wukv[:, :, :, 1].reshape(depth, MLA_KV_RANK, GROUP_WIDTH).astype(bf)
    wo = w_out.astype(bf).reshape(depth, 4, 4, HEAD_DIM, D_MODEL)
    slots = [(0, h) for h in range(4)] + [(1, h) for h in _GQA_HEAD_ORDER] \
        + [(2, h) for h in _GQA_HEAD_ORDER] + [(3, h) for h in range(4)]
    wo = jnp.stack([wo[:, m, h] for m, h in slots], axis=1).reshape(depth, D_MODEL, D_MODEL)
    wup = ffn_w_up.astype(bf)
    wdn = ffn_w_down.astype(bf)
    sub = jnp.tile(diff_subln, (1, 2))[:, None, :]

    x2 = x.reshape(n, D_MODEL)
    r3 = lambda a: a.reshape(bsz, seq, a.shape[-1])
    for l in range(depth):
        lam_init = 0.8 - 0.6 * math.exp(-0.3 * l)
        (aq, ak, avt, bq, bk, bv, cq_, ck, cvt, dq, dk, dvt) = _pre(
            x2, attn_pre_norm[l][None], w_in_p, cq_g[l], ck_g[l], ctab,
            mla_q_norm[l][None], mla_kv_norm[l][None], wuq[l], wukvk[l], wukvv[l], dtab, seq, tm, l)
        oa = _diff_attention(r3(aq), r3(ak), avt, gt, diff_lambda[l][None], sub[l], lam_init,
                             tq_a, cq)
        ob = _win_attention(r3(bq), r3(bk), r3(bv), win_sink[l], wbias, tw)
        oc = _gqa_attention(r3(cq_), r3(ck), cvt, tq, cq)
        od = _mla_attention(r3(dq), r3(dk), dvt, tq, cq)
        x2 = _outproj(x2, oa.reshape(n, -1), ob.reshape(n, -1), oc.reshape(n, -1),
                      od.reshape(n, -1), wo, attn_post_norm[l][None], to, l)
        x2 = _ffn(x2, ffn_pre_norm[l][None], wup, ffn_conv_w[l], ffn_conv_b[l][None],
                  wdn, ffn_post_norm[l][None], seq, tf, fc, l)
    return x2.reshape(bsz, seq, D_MODEL)


def kernel(x, rel_bias, attn_pre_norm, w_in, diff_lambda, diff_subln, win_sink, ax_q_norm, ax_k_norm,
           mla_q_norm, mla_kv_norm, mla_w_uq, mla_w_ukv, w_out, attn_post_norm, ffn_pre_norm,
           ffn_w_up, ffn_conv_w, ffn_conv_b, ffn_w_down, ffn_post_norm):
    return _forward(x, rel_bias, attn_pre_norm, w_in, diff_lambda, diff_subln, win_sink, ax_q_norm,
                    ax_k_norm, mla_q_norm, mla_kv_norm, mla_w_uq, mla_w_ukv, w_out, attn_post_norm,
                    ffn_pre_norm, ffn_w_up, ffn_conv_w, ffn_conv_b, ffn_w_down, ffn_post_norm,
                    tq_a=512, tq=1024, cq=256, tw=1024, tm=512, to=1024, tf=512, fc=256)
```
